```python
import jax, jax.numpy as jnp
from jax import lax
import numpy as np

D_MODEL = 2048
BATCH = 1
SEQ = 8192
DEPTH = 1
DEC_BATCH = 128
DEC_SEQ = 1
PAST_LEN = 2048
PAGE_SIZE = 128

HEAD_DIM = 128
HEADS_PER_GROUP = 4
WINDOWS = (128, 512, 2048)
DILATIONS = (1, 4, 16)
N_GROUPS = 3
N_HEADS = N_GROUPS * HEADS_PER_GROUP
ATTN_WIDTH = N_HEADS * HEAD_DIM
ATTN_OUT = HEADS_PER_GROUP * HEAD_DIM
ATTN_BLOCK = 128
ROT_DIM = HEAD_DIM // 4
ROPE_THETA = 500000.0
D_RNN = D_MODEL
RNN_BLOCKS = 16
RNN_BLOCK_W = D_RNN // RNN_BLOCKS
CONV_W = 4
LRU_C = 8.0
N_EXPERTS = 32
TOP_K = 4
D_FF = D_MODEL
SWIGLU_LIMIT = 7.0
SWIGLU_ALPHA = 1.702
MOE_BLOCK = 128
EPS = 1e-6
IN_WIDTH = 3 * ATTN_WIDTH + 2 * D_RNN + 2 * D_MODEL

kernel_name = "dilated_attn_rglru_moe_hybrid_step"


def rms_norm(x, g):
    xf = x.astype(jnp.float32)
    y = xf * lax.rsqrt(jnp.mean(xf * xf, axis=-1, keepdims=True) + EPS)
    return (y * g.astype(jnp.float32)).astype(x.dtype)


def partial_rope(x, pos):
    half = ROT_DIM // 2
    inv = ROPE_THETA ** (-2.0 * jnp.arange(half, dtype=jnp.float32) / ROT_DIM)
    ang = pos.astype(jnp.float32)[:, None] * inv[None, :]
    cos = jnp.cos(ang)[None, :, None, :]
    sin = jnp.sin(ang)[None, :, None, :]
    xr = x[..., :ROT_DIM].astype(jnp.float32)
    x1, x2 = xr[..., :half], xr[..., half:]
    rot = jnp.concatenate([x1 * cos - x2 * sin, x2 * cos + x1 * sin], axis=-1)
    return jnp.concatenate([rot.astype(x.dtype), x[..., ROT_DIM:]], axis=-1)


def dilated_attention_prompt(q, k, v, dil, steps):
    B, S, H, Dh = q.shape
    span = dil * ATTN_BLOCK
    s_pad = -(-S // span) * span
    L = s_pad // dil
    nb = L // ATTN_BLOCK

    def to_blocks(a):
        a = jnp.pad(a.astype(jnp.float32), ((0, 0), (0, s_pad - S), (0, 0), (0, 0)))
        a = a.reshape(B, L, dil, H, Dh).transpose(0, 2, 1, 3, 4)
        return a.reshape(B, dil, nb, ATTN_BLOCK, H, Dh)

    def with_prev(a):
        prev = jnp.pad(a, ((0, 0), (0, 0), (1, 0), (0, 0), (0, 0), (0, 0)))[:, :, :-1]
        return jnp.concatenate([prev, a], axis=3)

    qb = to_blocks(q)
    kk = with_prev(to_blocks(k))
    vv = with_prev(to_blocks(v))
    s = jnp.einsum('brnqhd,brnkhd->brnhqk', qb, kk) * (HEAD_DIM ** -0.5)
    qi = jnp.arange(ATTN_BLOCK)[:, None]
    kj = jnp.arange(2 * ATTN_BLOCK)[None, :]
    dist = ATTN_BLOCK + qi - kj
    band = (dist >= 0) & (dist <= steps)
    exists = (jnp.arange(nb)[:, None, None] > 0) | (kj >= ATTN_BLOCK)[None]
    mask = band[None] & exists
    s = jnp.where(mask[None, None, :, None], s, -jnp.inf)
    m = jnp.max(s, axis=-1, keepdims=True)
    p = jnp.exp(s - m)
    den = jnp.sum(p, axis=-1)
    o = jnp.einsum('brnhqk,brnkhd->brnhqd', p, vv) / den[..., None]
    lse = m[..., 0] + jnp.log(den)
    o = o.transpose(0, 2, 4, 1, 3, 5).reshape(B, s_pad, H, Dh)[:, :S]
    lse = lse.transpose(0, 2, 4, 1, 3).reshape(B, s_pad, H)[:, :S]
    return o, lse


def dilated_attention_sample(q, k_new, v_new, k_buf, v_buf, dil, steps):
    T = q.shape[1]
    wb = k_buf.shape[1]
    kk = jnp.concatenate([k_buf.astype(k_new.dtype), k_new], axis=1)
    vv = jnp.concatenate([v_buf.astype(v_new.dtype), v_new], axis=1)
    idx = wb + jnp.arange(T)[:, None] - dil * jnp.arange(steps + 1)[None, :]
    valid = idx >= 0
    idx_c = jnp.maximum(idx, 0)
    kg = kk[:, idx_c].astype(jnp.float32)
    vg = vv[:, idx_c].astype(jnp.float32)
    s = jnp.einsum('bthd,btjhd->bthj', q.astype(jnp.float32), kg) * (HEAD_DIM ** -0.5)
    s = jnp.where(valid[None, :, None, :], s, -jnp.inf)
    m = jnp.max(s, axis=-1, keepdims=True)
    p = jnp.exp(s - m)
    den = jnp.sum(p, axis=-1)
    o = jnp.einsum('bthj,btjhd->bthd', p, vg) / den[..., None]
    lse = m[..., 0] + jnp.log(den)
    return o, lse, kk[:, kk.shape[1] - wb:], vv[:, vv.shape[1] - wb:]


def _linear_combine(c1, c2):
    a1, b1 = c1
    a2, b2 = c2
    return a1 * a2, a2 * b1 + b2


def rglru_branch(xr, gate_in, conv_buf, h0, lp):
    B, T, _ = xr.shape
    xp = jnp.concatenate([conv_buf.astype(xr.dtype), xr], axis=1)
    xc = lp['conv_b'] + sum(xp[:, j:j + T] * lp['conv_w'][j] for j in range(CONV_W))
    new_conv = xp[:, xp.shape[1] - (CONV_W - 1):]
    xb = xc.reshape(B, T, RNN_BLOCKS, RNN_BLOCK_W)
    r = jax.nn.sigmoid((jnp.einsum('btnc,ncd->btnd', xb, lp['w_rg_a']).reshape(B, T, D_RNN)
                        + lp['b_rg_a']).astype(jnp.float32))
    i = jax.nn.sigmoid((jnp.einsum('btnc,ncd->btnd', xb, lp['w_rg_x']).reshape(B, T, D_RNN)
                        + lp['b_rg_x']).astype(jnp.float32))
    log_a = -LRU_C * r * jax.nn.softplus(-lp['lru_lambda'].astype(jnp.float32))
    a = jnp.exp(log_a)
    b = jnp.sqrt(-jnp.expm1(2.0 * log_a)) * i * xc.astype(jnp.float32)
    b = b.at[:, 0].add(a[:, 0] * h0.astype(jnp.float32))
    _, h = lax.associative_scan(_linear_combine, (a, b), axis=1)
    new_h = h[:, -1]
    y = h.astype(xr.dtype) * jax.nn.gelu(gate_in)
    return y @ lp['w_rnn_proj'], new_conv, new_h


def moe(h, lp):
    lead = h.shape[:-1]
    xf = h.reshape(-1, D_MODEL)
    T = xf.shape[0]
    logits = xf.astype(jnp.float32) @ lp['w_router'].astype(jnp.float32) + lp['b_router'].astype(jnp.float32)
    top_v, top_i = lax.top_k(logits, TOP_K)
    gates = jax.nn.softmax(top_v, axis=-1)
    n_slots = T * TOP_K
    flat_e = top_i.reshape(-1)
    order = jnp.argsort(flat_e)
    sorted_e = flat_e[order]
    counts = jnp.bincount(flat_e, length=N_EXPERTS)
    padded = (counts + MOE_BLOCK - 1) // MOE_BLOCK * MOE_BLOCK
    pad_end = jnp.cumsum(padded)
    pad_start = pad_end - padded
    grp_start = jnp.cumsum(counts) - counts
    dest = pad_start[sorted_e] + (jnp.arange(n_slots) - grp_start[sorted_e])
    n_blocks = -(-n_slots // MOE_BLOCK) + N_EXPERTS
    tok = order // TOP_K
    buf = jnp.zeros((n_blocks * MOE_BLOCK, D_MODEL), xf.dtype).at[dest].set(xf[tok])
    blk_e = jnp.minimum(jnp.searchsorted(pad_end, jnp.arange(n_blocks) * MOE_BLOCK, side='right'),
                        N_EXPERTS - 1)

    def expert_block(args):
        xb, e = args
        g = xb @ lp['w_gate'][e] + lp['b_gate'][e]
        u = xb @ lp['w_up'][e] + lp['b_up'][e]
        g = jnp.minimum(g, SWIGLU_LIMIT)
        u = jnp.clip(u, -SWIGLU_LIMIT, SWIGLU_LIMIT)
        act = g * jax.nn.sigmoid(SWIGLU_ALPHA * g) * (u + 1.0)
        return act @ lp['w_down'][e] + lp['b_down'][e]

    out = lax.map(expert_block, (buf.reshape(n_blocks, MOE_BLOCK, D_MODEL), blk_e)).reshape(-1, D_MODEL)
    contrib = out[dest] * gates.reshape(-1)[order][:, None].astype(out.dtype)
    y = jnp.zeros_like(xf).at[tok].add(contrib)
    return y.reshape(*lead, D_MODEL)


def trunk_layer(x, pos, kv_bufs, conv_buf, h0, lp):
    B, T, _ = x.shape
    xn = rms_norm(x, lp['g_mix'])
    p = xn @ lp['w_in']
    o1, o2, o3 = ATTN_WIDTH, 2 * ATTN_WIDTH, 3 * ATTN_WIDTH
    o4, o5, o6 = o3 + D_RNN, o3 + 2 * D_RNN, o3 + 2 * D_RNN + D_MODEL
    q = p[..., :o1].reshape(B, T, N_HEADS, HEAD_DIM)
    k = p[..., o1:o2].reshape(B, T, N_HEADS, HEAD_DIM)
    v = p[..., o2:o3].reshape(B, T, N_HEADS, HEAD_DIM)
    rnn_x, rnn_gate = p[..., o3:o4], p[..., o4:o5]
    gate_attn, gate_rnn = p[..., o5:o6], p[..., o6:]
    q = partial_rope(rms_norm(q, lp['q_norm']), pos)
    k = partial_rope(rms_norm(k, lp['k_norm']), pos)

    outs, lses, new_kv = [], [], []
    for g in range(N_GROUPS):
        sl = slice(g * HEADS_PER_GROUP, (g + 1) * HEADS_PER_GROUP)
        steps = WINDOWS[g] // DILATIONS[g]
        qg, kg, vg = q[:, :, sl], k[:, :, sl], v[:, :, sl]
        if kv_bufs is None:
            o, lse = dilated_attention_prompt(qg, kg, vg, DILATIONS[g], steps)
            keep = min(WINDOWS[g], T)
            new_kv += [kg[:, T - keep:], vg[:, T - keep:]]
        else:
            o, lse, nk, nv = dilated_attention_sample(qg, kg, vg, kv_bufs[2 * g], kv_bufs[2 * g + 1],
                                                      DILATIONS[g], steps)
            new_kv += [nk, nv]
        outs.append(o)
        lses.append(lse)
    wts = jax.nn.softmax(jnp.stack(lses), axis=0)
    attn = jnp.sum(wts[..., None] * jnp.stack(outs), axis=0)
    attn_d = attn.reshape(B, T, ATTN_OUT).astype(x.dtype) @ lp['w_attn_proj']

    rnn_d, new_conv, new_h = rglru_branch(rnn_x, rnn_gate, conv_buf, h0, lp)
    mix = jax.nn.sigmoid(gate_attn) * attn_d + jax.nn.sigmoid(gate_rnn) * rnn_d
    x = x + mix @ lp['w_out']
    x = x + moe(rms_norm(x, lp['g_ffn']), lp)
    return x, new_kv, new_conv, new_h


def setup_inputs(seed: int = 0) -> dict:
    key = jax.random.key(seed)
    ks = jax.random.split(key, 40)

    def nrm(k, shape, scale):
        return jax.random.normal(k, shape, jnp.float32) * scale

    inp = {}
    inp['x_prompt'] = nrm(ks[0], (BATCH, SEQ, D_MODEL), 1.0)
    inp['x_sample'] = nrm(ks[1], (DEC_BATCH, DEC_SEQ, D_MODEL), 1.0)
    for g, w in enumerate(WINDOWS):
        wb = min(w, PAST_LEN)
        inp['cache_k_w%d' % w] = nrm(ks[2 + 2 * g], (DEPTH, DEC_BATCH, wb, HEADS_PER_GROUP, HEAD_DIM), 1.0)
        inp['cache_v_w%d' % w] = nrm(ks[3 + 2 * g], (DEPTH, DEC_BATCH, wb, HEADS_PER_GROUP, HEAD_DIM), 1.0)
    inp['state_conv'] = nrm(ks[8], (DEPTH, DEC_BATCH, CONV_W - 1, D_RNN), 1.0)
    inp['state_rglru'] = nrm(ks[9], (DEPTH, DEC_BATCH, D_RNN), 0.5)
    inp['g_mix'] = 1.0 + nrm(ks[10], (DEPTH, D_MODEL), 0.1)
    inp['w_in'] = nrm(ks[11], (DEPTH, D_MODEL, IN_WIDTH), D_MODEL ** -0.5)
    inp['q_norm'] = 1.0 + nrm(ks[12], (DEPTH, HEAD_DIM), 0.1)
    inp['k_norm'] = 1.0 + nrm(ks[13], (DEPTH, HEAD_DIM), 0.1)
    inp['w_attn_proj'] = nrm(ks[14], (DEPTH, ATTN_OUT, D_MODEL), ATTN_OUT ** -0.5)
    inp['conv_w'] = nrm(ks[15], (DEPTH, CONV_W, D_RNN), CONV_W ** -0.5)
    inp['conv_b'] = nrm(ks[16], (DEPTH, D_RNN), 0.01)
    inp['w_rg_a'] = nrm(ks[17], (DEPTH, RNN_BLOCKS, RNN_BLOCK_W, RNN_BLOCK_W), RNN_BLOCK_W ** -0.5)
    inp['b_rg_a'] = nrm(ks[18], (DEPTH, D_RNN), 0.01)
    inp['w_rg_x'] = nrm(ks[19], (DEPTH, RNN_BLOCKS, RNN_BLOCK_W, RNN_BLOCK_W), RNN_BLOCK_W ** -0.5)
    inp['b_rg_x'] = nrm(ks[20], (DEPTH, D_RNN), 0.01)
    u = jax.random.uniform(ks[21], (DEPTH, D_RNN), jnp.float32, minval=0.9, maxval=0.999)
    s = u ** (1.0 / LRU_C)
    inp['lru_lambda'] = jnp.log(s) - jnp.log1p(-s)
    inp['w_rnn_proj'] = nrm(ks[22], (DEPTH, D_RNN, D_MODEL), D_RNN ** -0.5)
    inp['w_out'] = nrm(ks[23], (DEPTH, D_MODEL, D_MODEL), D_MODEL ** -0.5)
    inp['g_ffn'] = 1.0 + nrm(ks[24], (DEPTH, D_MODEL), 0.1)
    inp['w_router'] = nrm(ks[25], (DEPTH, D_MODEL, N_EXPERTS), D_MODEL ** -0.5)
    inp['b_router'] = nrm(ks[26], (DEPTH, N_EXPERTS), 0.01)
    inp['w_gate'] = nrm(ks[27], (DEPTH, N_EXPERTS, D_MODEL, D_FF), D_MODEL ** -0.5)
    inp['b_gate'] = nrm(ks[28], (DEPTH, N_EXPERTS, D_FF), 0.01)
    inp['w_up'] = nrm(ks[29], (DEPTH, N_EXPERTS, D_MODEL, D_FF), D_MODEL ** -0.5)
    inp['b_up'] = nrm(ks[30], (DEPTH, N_EXPERTS, D_FF), 0.01)
    inp['w_down'] = nrm(ks[31], (DEPTH, N_EXPERTS, D_FF, D_MODEL), D_FF ** -0.5)
    inp['b_down'] = nrm(ks[32], (DEPTH, N_EXPERTS, D_MODEL), 0.01)
    return inp


def reference(x_prompt, x_sample, cache_k_w128, cache_v_w128, cache_k_w512, cache_v_w512,
              cache_k_w2048, cache_v_w2048, state_conv, state_rglru, g_mix, w_in, q_norm, k_norm,
              w_attn_proj, conv_w, conv_b, w_rg_a, b_rg_a, w_rg_x, b_rg_x, lru_lambda, w_rnn_proj,
              w_out, g_ffn, w_router, b_router, w_gate, b_gate, w_up, b_up, w_down, b_down):
    caches = (cache_k_w128, cache_v_w128, cache_k_w512, cache_v_w512, cache_k_w2048, cache_v_w2048)
    pos_p = jnp.arange(x_prompt.shape[1])
    pos_s = PAST_LEN + jnp.arange(x_sample.shape[1])
    bp = x_prompt.shape[0]
    yp, ys = x_prompt, x_sample
    st_p, st_s = [], []
    for l in range(DEPTH):
        lp = dict(g_mix=g_mix[l], w_in=w_in[l], q_norm=q_norm[l], k_norm=k_norm[l],
                  w_attn_proj=w_attn_proj[l], conv_w=conv_w[l], conv_b=conv_b[l], w_rg_a=w_rg_a[l],
                  b_rg_a=b_rg_a[l], w_rg_x=w_rg_x[l], b_rg_x=b_rg_x[l], lru_lambda=lru_lambda[l],
                  w_rnn_proj=w_rnn_proj[l], w_out=w_out[l], g_ffn=g_ffn[l], w_router=w_router[l],
                  b_router=b_router[l], w_gate=w_gate[l], b_gate=b_gate[l], w_up=w_up[l], b_up=b_up[l],
                  w_down=w_down[l], b_down=b_down[l])
        zero_conv = jnp.zeros((bp, CONV_W - 1, D_RNN), x_prompt.dtype)
        zero_h = jnp.zeros((bp, D_RNN), jnp.float32)
        yp, kv_p, conv_p, h_p = trunk_layer(yp, pos_p, None, zero_conv, zero_h, lp)
        bufs = [c[l] for c in caches]
        ys, kv_s, conv_s, h_s = trunk_layer(ys, pos_s, bufs, state_conv[l], state_rglru[l], lp)
        st_p.append(kv_p + [conv_p, h_p])
        st_s.append(kv_s + [conv_s, h_s])
    (k128_p, v128_p, k512_p, v512_p, k2048_p, v2048_p, conv_p, rglru_p) = [
        jnp.stack([s[i] for s in st_p]) for i in range(8)]
    (k128_s, v128_s, k512_s, v512_s, k2048_s, v2048_s, conv_s, rglru_s) = [
        jnp.stack([s[i] for s in st_s]) for i in range(8)]
    return (yp, ys, k128_p, v128_p, k512_p, v512_p, k2048_p, v2048_p, conv_p, rglru_p,
            k128_s, v128_s, k512_s, v512_s, k2048_s, v2048_s, conv_s, rglru_s)
```

```python
import functools

import jax
import jax.numpy as jnp
from jax import lax
from jax.experimental import pallas as pl
from jax.experimental.pallas import tpu as pltpu

F32 = jnp.float32
BF16 = jnp.bfloat16
I32 = jnp.int32

D_MODEL = 2048
SEQ = 8192
DEC_BATCH = 128
PAST_LEN = 2048
T_ALL = SEQ + DEC_BATCH

HEAD_DIM = 128
HEADS_PER_GROUP = 4
WINDOWS = (128, 512, 2048)
DILATIONS = (1, 4, 16)
N_GROUPS = 3
N_HEADS = N_GROUPS * HEADS_PER_GROUP
ATTN_WIDTH = N_HEADS * HEAD_DIM
ATTN_OUT = HEADS_PER_GROUP * HEAD_DIM
ATTN_BLOCK = 128
ROT_DIM = HEAD_DIM // 4
ROPE_THETA = 500000.0
D_RNN = D_MODEL
RNN_BLOCK_W = 128
CONV_W = 4
LRU_C = 8.0
N_EXPERTS = 32
TOP_K = 4
D_FF = D_MODEL
SWIGLU_LIMIT = 7.0
SWIGLU_ALPHA = 1.702
EPS = 1e-6
IN_WIDTH = 3 * ATTN_WIDTH + 2 * D_RNN + 2 * D_MODEL

LANES = 128
SUBLANES = 8
VMEM_LIMIT_BYTES = 56 * 1024 * 1024

COL_TILE = 512
COL_RNN_X = (3 * ATTN_WIDTH) // COL_TILE
COL_RNN_GATE = (3 * ATTN_WIDTH + D_RNN) // COL_TILE
COL_GATE_ATTN = (3 * ATTN_WIDTH + 2 * D_RNN) // COL_TILE
COL_GATE_RNN = (3 * ATTN_WIDTH + 2 * D_RNN + D_MODEL) // COL_TILE

ROW_TILE = 1664
MIX_ROW_TILE = 832
NORM_TILE = 640
COMBINE_TILE = 320
SPAN = 2048
NEG_BIG = -1e30

EXPERT_BLOCK = 128
EXPERT_CAP_BLOCKS = 16
EXPERT_CAP = EXPERT_BLOCK * EXPERT_CAP_BLOCKS
FF_TILE = 256
N_SLOTS = T_ALL * TOP_K
N_SORTED_ROWS = N_SLOTS + N_EXPERTS * EXPERT_BLOCK
N_SORTED_BLOCKS = N_SORTED_ROWS // EXPERT_BLOCK
N_WORK_ITEMS = N_EXPERTS + -(-N_SORTED_BLOCKS // EXPERT_CAP_BLOCKS)


def _params(semantics, vmem=VMEM_LIMIT_BYTES):
    return pltpu.CompilerParams(dimension_semantics=semantics, vmem_limit_bytes=vmem)


def _rmsnorm_kernel(x_ref, g_ref, o_ref):
    x = x_ref[...]
    y = x * lax.rsqrt(jnp.mean(x * x, axis=-1, keepdims=True) + EPS)
    o_ref[...] = (y * g_ref[...]).astype(o_ref.dtype)


def _rmsnorm_bf16(x, g):
    t = x.shape[0]
    return pl.pallas_call(
        _rmsnorm_kernel,
        out_shape=jax.ShapeDtypeStruct((t, D_MODEL), BF16),
        grid=(t // NORM_TILE,),
        in_specs=[pl.BlockSpec((NORM_TILE, D_MODEL), lambda i: (i, 0)),
                  pl.BlockSpec((1, D_MODEL), lambda i: (0, 0))],
        out_specs=pl.BlockSpec((NORM_TILE, D_MODEL), lambda i: (i, 0)),
        compiler_params=_params(("parallel",)),
        name="rmsnorm_bf16",
    )(x, g)


def _in_proj_kernel(x_ref, w_ref, cos_ref, sin_ref, gain_ref, o_ref, wbf_ref):
    j = pl.program_id(0)

    @pl.when(pl.program_id(1) == 0)
    def _():
        wbf_ref[...] = w_ref[...].astype(BF16)

    o_ref[...] = jnp.dot(x_ref[...], wbf_ref[...], preferred_element_type=F32)

    @pl.when(j < 2 * ATTN_WIDTH // COL_TILE)
    def _():
        lane = lax.broadcasted_iota(I32, (ROW_TILE, HEAD_DIM), 1)
        first_half = lane < ROT_DIM // 2
        gain = gain_ref[...]
        cos = cos_ref[...]
        sin = sin_ref[...]
        for h in range(COL_TILE // HEAD_DIM):
            cols = slice(h * HEAD_DIM, (h + 1) * HEAD_DIM)
            xh = o_ref[:, cols]
            y = xh * lax.rsqrt(jnp.mean(xh * xh, axis=-1, keepdims=True) + EPS) * gain
            partner = jnp.where(first_half,
                                pltpu.roll(y, HEAD_DIM - ROT_DIM // 2, 1),
                                pltpu.roll(y, ROT_DIM // 2, 1))
            o_ref[:, cols] = y * cos + partner * sin


def _in_proj(xn, w_in, cos_t, sin_t, gains):
    n_tiles = IN_WIDTH // COL_TILE
    qk_tiles = ATTN_WIDTH // COL_TILE
    return pl.pallas_call(
        _in_proj_kernel,
        out_shape=jax.ShapeDtypeStruct((T_ALL, IN_WIDTH), F32),
        grid=(n_tiles, T_ALL // ROW_TILE),
        in_specs=[
            pl.BlockSpec((ROW_TILE, D_MODEL), lambda j, i: (i, 0)),
            pl.BlockSpec((None, D_MODEL, COL_TILE), lambda j, i: (0, 0, j)),
            pl.BlockSpec((ROW_TILE, HEAD_DIM), lambda j, i: (i, 0)),
            pl.BlockSpec((ROW_TILE, HEAD_DIM), lambda j, i: (i, 0)),
            pl.BlockSpec((None, 1, HEAD_DIM), lambda j, i: (jnp.minimum(j // qk_tiles, 1), 0, 0)),
        ],
        out_specs=pl.BlockSpec((ROW_TILE, COL_TILE), lambda j, i: (i, j)),
        scratch_shapes=[pltpu.VMEM((D_MODEL, COL_TILE), BF16)],
        compiler_params=_params(("arbitrary", "arbitrary")),
        name="in_proj",
    )(xn, w_in, cos_t, sin_t, gains)


def _dot_nt(a, b):
    return lax.dot_general(a, b, (((1,), (1,)), ((), ())), preferred_element_type=F32)


def _attn_prompt_kernel(*refs):
    ins = refs[:15]
    o_ref = refs[15]
    og_ref, lse_ref = refs[16], refs[17]
    span_idx = pl.program_id(0)
    qi = lax.broadcasted_iota(I32, (ATTN_BLOCK, ATTN_BLOCK), 0)
    kj = lax.broadcasted_iota(I32, (ATTN_BLOCK, ATTN_BLOCK), 1)
    mask_cur = qi >= kj
    mask_prev_band = kj >= qi
    scale = HEAD_DIM ** -0.5

    for g, dil in enumerate(DILATIONS):
        q_ref, kc_ref, vc_ref, kp_ref, vp_ref = ins[5 * g:5 * g + 5]
        blk = ATTN_BLOCK * dil
        for m in range(SPAN // blk):
            for r in range(dil):
                rows = pl.ds(m * blk + r, ATTN_BLOCK, stride=dil) if dil > 1 else pl.ds(m * blk, ATTN_BLOCK)
                q = (q_ref[rows, :] * scale).astype(BF16)
                k_cur = kc_ref[rows, :].astype(BF16)
                v_cur = vc_ref[rows, :].astype(BF16)
                if m == 0:
                    prow = pl.ds(r, ATTN_BLOCK, stride=dil) if dil > 1 else pl.ds(0, ATTN_BLOCK)
                    k_prev = kp_ref[prow, :].astype(BF16)
                    v_prev = vp_ref[prow, :].astype(BF16)
                    mask_prev = jnp.logical_and(mask_prev_band, span_idx > 0)
                else:
                    prow = (pl.ds((m - 1) * blk + r, ATTN_BLOCK, stride=dil) if dil > 1
                            else pl.ds((m - 1) * blk, ATTN_BLOCK))
                    k_prev = kc_ref[prow, :].astype(BF16)
                    v_prev = vc_ref[prow, :].astype(BF16)
                    mask_prev = mask_prev_band
                s_cur = jnp.where(mask_cur, _dot_nt(q, k_cur), NEG_BIG)
                s_prev = jnp.where(mask_prev, _dot_nt(q, k_prev), NEG_BIG)
                mx = jnp.maximum(jnp.max(s_cur, axis=-1, keepdims=True),
                                 jnp.max(s_prev, axis=-1, keepdims=True))
                p_cur = jnp.exp(s_cur - mx)
                p_prev = jnp.exp(s_prev - mx)
                den = jnp.sum(p_cur, axis=-1, keepdims=True) + jnp.sum(p_prev, axis=-1, keepdims=True)
                pv = (jnp.dot(p_cur.astype(BF16), v_cur, preferred_element_type=F32)
                      + jnp.dot(p_prev.astype(BF16), v_prev, preferred_element_type=F32))
                og_ref[g, rows, :] = pv / den
                lse_ref[g, rows, :] = jnp.broadcast_to(mx + jnp.log(den), (ATTN_BLOCK, HEAD_DIM))

    lse = lse_ref[...]
    top = jnp.max(lse, axis=0)
    w = jnp.exp(lse - top[None])
    o_ref[...] = jnp.sum(w * og_ref[...], axis=0) / jnp.sum(w, axis=0)


def _attn_prompt(p):
    q_cols, k_cols, v_cols = 0, N_HEADS, 2 * N_HEADS
    in_specs = []
    for g, dil in enumerate(DILATIONS):
        blk = ATTN_BLOCK * dil
        per_span = SPAN // blk

        def cur(off, g=g):
            return pl.BlockSpec((SPAN, HEAD_DIM), lambda s, h: (s, off + g * HEADS_PER_GROUP + h))

        def prev(off, g=g, blk=blk, per_span=per_span):
            return pl.BlockSpec((blk, HEAD_DIM),
                                lambda s, h: (jnp.maximum(s * per_span - 1, 0), off + g * HEADS_PER_GROUP + h))

        in_specs += [cur(q_cols), cur(k_cols), cur(v_cols), prev(k_cols), prev(v_cols)]
    return pl.pallas_call(
        _attn_prompt_kernel,
        out_shape=jax.ShapeDtypeStruct((SEQ, ATTN_OUT), F32),
        grid=(SEQ // SPAN, HEADS_PER_GROUP),
        in_specs=in_specs,
        out_specs=pl.BlockSpec((SPAN, HEAD_DIM), lambda s, h: (s, h)),
        scratch_shapes=[pltpu.VMEM((N_GROUPS, SPAN, HEAD_DIM), F32),
                        pltpu.VMEM((N_GROUPS, SPAN, HEAD_DIM), F32)],
        compiler_params=_params(("parallel", "parallel")),
        name="attn_prompt",
    )(*([p] * 15))


SAMPLE_BB = 8


def _attn_sample_kernel(q_ref, k_ref, v_ref, ck0, cv0, ck1, cv1, ck2, cv2, o_ref):
    caches = ((ck0, cv0), (ck1, cv1), (ck2, cv2))
    scale = HEAD_DIM ** -0.5
    for b in range(SAMPLE_BB):
        outs, lses = [], []
        for g in range(N_GROUPS):
            heads = slice(g * HEADS_PER_GROUP, (g + 1) * HEADS_PER_GROUP)
            q = q_ref[b, heads, :] * scale
            k_new = k_ref[b, heads, :]
            v_new = v_ref[b, heads, :]
            k_old = caches[g][0][b]
            v_old = caches[g][1][b]
            s_old = jnp.sum(k_old * q[None], axis=-1, keepdims=True)
            s_new = jnp.sum(k_new * q, axis=-1, keepdims=True)
            mx = jnp.maximum(jnp.max(s_old, axis=0), s_new)
            p_old = jnp.exp(s_old - mx[None])
            p_new = jnp.exp(s_new - mx)
            den = jnp.sum(p_old, axis=0) + p_new
            pv = jnp.sum(p_old * v_old, axis=0) + p_new * v_new
            outs.append(pv / den)
            lses.append(mx + jnp.log(den))
        top = jnp.maximum(jnp.maximum(lses[0], lses[1]), lses[2])
        ws = [jnp.exp(l - top) for l in lses]
        o_ref[b] = (ws[0] * outs[0] + ws[1] * outs[1] + ws[2] * outs[2]) / (ws[0] + ws[1] + ws[2])


def _attn_sample(q_s, k_s, v_s, cache_views):
    row = pl.BlockSpec((SAMPLE_BB, N_HEADS, HEAD_DIM), lambda b: (b, 0, 0))
    cache_spec = pl.BlockSpec((SAMPLE_BB, ATTN_BLOCK, None, HEADS_PER_GROUP, HEAD_DIM),
                              lambda b: (b, 0, 0, 0, 0))
    return pl.pallas_call(
        _attn_sample_kernel,
        out_shape=jax.ShapeDtypeStruct((DEC_BATCH, HEADS_PER_GROUP, HEAD_DIM), F32),
        grid=(DEC_BATCH // SAMPLE_BB,),
        in_specs=[row, row, row] + [cache_spec] * 6,
        out_specs=pl.BlockSpec((SAMPLE_BB, HEADS_PER_GROUP, HEAD_DIM), lambda b: (b, 0, 0)),
        compiler_params=_params(("parallel",)),
        name="attn_sample",
    )(q_s, k_s, v_s, *cache_views)


SHIFT_CHUNKS = 4


def _cache_shift_kernel(*refs):
    caches, news, outs = refs[0:6], refs[6:12], refs[12:18]
    sem = refs[18]
    bb = DEC_BATCH // SHIFT_CHUNKS
    copies = []
    for a in range(6):
        w = caches[a].shape[2]
        for c in range(SHIFT_CHUNKS):
            copies.append(pltpu.make_async_copy(
                caches[a].at[0, pl.ds(c * bb, bb), pl.ds(1, w - 1)],
                outs[a].at[0, pl.ds(c * bb, bb), pl.ds(0, w - 1)],
                sem.at[a, c]))
        copies.append(pltpu.make_async_copy(news[a], outs[a].at[0, :, w - 1], sem.at[a, SHIFT_CHUNKS]))
    for cp in copies:
        cp.start()
    for cp in copies:
        cp.wait()


def _cache_shift(caches, news):
    any_spec = pl.BlockSpec(memory_space=pl.ANY)
    vmem_spec = pl.BlockSpec(memory_space=pltpu.VMEM)
    return pl.pallas_call(
        _cache_shift_kernel,
        out_shape=[jax.ShapeDtypeStruct(c.shape, c.dtype) for c in caches],
        in_specs=[any_spec] * 6 + [vmem_spec] * 6,
        out_specs=[any_spec] * 6,
        scratch_shapes=[pltpu.SemaphoreType.DMA((6, SHIFT_CHUNKS + 1))],
        name="cache_shift",
    )(*caches, *news)


def _rglru_gates(xc, wa_ref, ba_ref, wx_ref, bx_ref, lam_ref):
    r_parts, i_parts = [], []
    for n in range(COL_TILE // RNN_BLOCK_W):
        xb = xc[:, n * RNN_BLOCK_W:(n + 1) * RNN_BLOCK_W].astype(BF16)
        r_parts.append(jnp.dot(xb, wa_ref[n].astype(BF16), preferred_element_type=F32))
        i_parts.append(jnp.dot(xb, wx_ref[n].astype(BF16), preferred_element_type=F32))
    r = jax.nn.sigmoid(jnp.concatenate(r_parts, axis=-1) + ba_ref[...])
    i = jax.nn.sigmoid(jnp.concatenate(i_parts, axis=-1) + bx_ref[...])
    neg_lam = -lam_ref[...]
    softplus = jnp.maximum(neg_lam, 0.0) + jnp.log1p(jnp.exp(-jnp.abs(neg_lam)))
    log_a = -LRU_C * r * softplus
    a = jnp.exp(log_a)
    b = jnp.sqrt(-jnp.tanh(log_a) * (jnp.exp(2.0 * log_a) + 1.0)) * i * xc
    return a, b


RNN_T_TILE = 512
CONV_PAD = SUBLANES


def _rglru_prompt_kernel(x_ref, gate_ref, cw_ref, cb_ref, wa_ref, ba_ref, wx_ref, bx_ref, lam_ref,
                         y_ref, hlast_ref, xbuf, a_scr, b_scr, h_scr, carry):
    t = pl.program_id(1)

    @pl.when(t == 0)
    def _():
        xbuf[0:CONV_PAD, :] = jnp.zeros((CONV_PAD, COL_TILE), F32)
        carry[...] = jnp.zeros((SUBLANES, COL_TILE), F32)

    @pl.when(t > 0)
    def _():
        xbuf[0:CONV_PAD, :] = xbuf[RNN_T_TILE:RNN_T_TILE + CONV_PAD, :]

    xbuf[CONV_PAD:CONV_PAD + RNN_T_TILE, :] = x_ref[...]
    xc = cb_ref[...] + sum(
        xbuf[pl.ds(CONV_PAD - (CONV_W - 1) + j, RNN_T_TILE), :] * cw_ref[j:j + 1, :] for j in range(CONV_W))
    a, b = _rglru_gates(xc, wa_ref, ba_ref, wx_ref, bx_ref, lam_ref)
    a_scr[...] = a
    b_scr[...] = b

    row = lax.broadcasted_iota(I32, (SUBLANES, COL_TILE), 0)

    def chunk(c, h):
        rows = pl.ds(pl.multiple_of(c * SUBLANES, SUBLANES), SUBLANES)
        ac = a_scr[rows, :]
        bc = b_scr[rows, :]
        for s in (1, 2, 4):
            a_sh = jnp.where(row >= s, pltpu.roll(ac, s, 0), 1.0)
            b_sh = jnp.where(row >= s, pltpu.roll(bc, s, 0), 0.0)
            bc = ac * b_sh + bc
            ac = ac * a_sh
        hh = ac * h + bc
        h_scr[rows, :] = hh
        return jnp.broadcast_to(hh[SUBLANES - 1:SUBLANES, :], (SUBLANES, COL_TILE))

    h_end = lax.fori_loop(0, RNN_T_TILE // SUBLANES, chunk, carry[...], unroll=4)
    carry[...] = h_end
    hlast_ref[...] = h_end[0:1, :]
    y_ref[...] = (h_scr[...] * jax.nn.gelu(gate_ref[...])).astype(y_ref.dtype)


def _rnn_param_specs(idx):
    return [
        pl.BlockSpec((None, CONV_W, COL_TILE), lambda *g: (0, 0, idx(*g))),
        pl.BlockSpec((1, COL_TILE), lambda *g: (0, idx(*g))),
        pl.BlockSpec((None, COL_TILE // RNN_BLOCK_W, RNN_BLOCK_W, RNN_BLOCK_W), lambda *g: (0, idx(*g), 0, 0)),
        pl.BlockSpec((1, COL_TILE), lambda *g: (0, idx(*g))),
        pl.BlockSpec((None, COL_TILE // RNN_BLOCK_W, RNN_BLOCK_W, RNN_BLOCK_W), lambda *g: (0, idx(*g), 0, 0)),
        pl.BlockSpec((1, COL_TILE), lambda *g: (0, idx(*g))),
        pl.BlockSpec((1, COL_TILE), lambda *g: (0, idx(*g))),
    ]


def _rglru_prompt(p, conv_w, conv_b, w_rg_a, b_rg_a, w_rg_x, b_rg_x, lru_lambda):
    n_c = D_RNN // COL_TILE
    return pl.pallas_call(
        _rglru_prompt_kernel,
        out_shape=[jax.ShapeDtypeStruct((SEQ, D_RNN), BF16), jax.ShapeDtypeStruct((1, D_RNN), F32)],
        grid=(n_c, SEQ // RNN_T_TILE),
        in_specs=[pl.BlockSpec((RNN_T_TILE, COL_TILE), lambda c, t: (t, COL_RNN_X + c)),
                  pl.BlockSpec((RNN_T_TILE, COL_TILE), lambda c, t: (t, COL_RNN_GATE + c))]
                 + _rnn_param_specs(lambda c, t: c),
        out_specs=[pl.BlockSpec((RNN_T_TILE, COL_TILE), lambda c, t: (t, c)),
                   pl.BlockSpec((1, COL_TILE), lambda c, t: (0, c))],
        scratch_shapes=[pltpu.VMEM((CONV_PAD + RNN_T_TILE, COL_TILE), F32),
                        pltpu.VMEM((RNN_T_TILE, COL_TILE), F32),
                        pltpu.VMEM((RNN_T_TILE, COL_TILE), F32),
                        pltpu.VMEM((RNN_T_TILE, COL_TILE), F32),
                        pltpu.VMEM((SUBLANES, COL_TILE), F32)],
        compiler_params=_params(("parallel", "arbitrary")),
        name="rglru_prompt",
    )(p, p, conv_w, conv_b, w_rg_a, b_rg_a, w_rg_x, b_rg_x, lru_lambda)


def _rglru_sample_kernel(x_ref, gate_ref, hist_ref, h0_ref, cw_ref, cb_ref, wa_ref, ba_ref, wx_ref, bx_ref,
                         lam_ref, y_ref, h_ref):
    xc = cb_ref[...] + x_ref[...] * cw_ref[CONV_W - 1:CONV_W, :]
    for j in range(CONV_W - 1):
        xc = xc + hist_ref[j] * cw_ref[j:j + 1, :]
    a, b = _rglru_gates(xc, wa_ref, ba_ref, wx_ref, bx_ref, lam_ref)
    h = a * h0_ref[...] + b
    h_ref[...] = h
    y_ref[...] = (h * jax.nn.gelu(gate_ref[...])).astype(y_ref.dtype)


def _rglru_sample(p, hist, h0, conv_w, conv_b, w_rg_a, b_rg_a, w_rg_x, b_rg_x, lru_lambda):
    n_c = D_RNN // COL_TILE
    row_blk = SEQ // DEC_BATCH
    return pl.pallas_call(
        _rglru_sample_kernel,
        out_shape=[jax.ShapeDtypeStruct((DEC_BATCH, D_RNN), BF16), jax.ShapeDtypeStruct((DEC_BATCH, D_RNN), F32)],
        grid=(n_c,),
        in_specs=[pl.BlockSpec((DEC_BATCH, COL_TILE), lambda c: (row_blk, COL_RNN_X + c)),
                  pl.BlockSpec((DEC_BATCH, COL_TILE), lambda c: (row_blk, COL_RNN_GATE + c)),
                  pl.BlockSpec((CONV_W - 1, DEC_BATCH, COL_TILE), lambda c: (0, 0, c)),
                  pl.BlockSpec((DEC_BATCH, COL_TILE), lambda c: (0, c))]
                 + _rnn_param_specs(lambda c: c),
        out_specs=[pl.BlockSpec((DEC_BATCH, COL_TILE), lambda c: (0, c)),
                   pl.BlockSpec((DEC_BATCH, COL_TILE), lambda c: (0, c))],
        compiler_params=_params(("parallel",)),
        name="rglru_sample",
    )(p, p, hist, h0, conv_w, conv_b, w_rg_a, b_rg_a, w_rg_x, b_rg_x, lru_lambda)


def _mix_kernel(attn_ref, y_ref, ga_ref, gr_ref, wap_ref, wrp_ref, o_ref, wap_bf, wrp_bf):
    @pl.when(pl.program_id(1) == 0)
    def _():
        wap_bf[...] = wap_ref[...].astype(BF16)
        wrp_bf[...] = wrp_ref[...].astype(BF16)

    attn_d = jnp.dot(attn_ref[...].astype(BF16), wap_bf[...], preferred_element_type=F32)
    rnn_d = jnp.dot(y_ref[...], wrp_bf[...], preferred_element_type=F32)
    mix = jax.nn.sigmoid(ga_ref[...]) * attn_d + jax.nn.sigmoid(gr_ref[...]) * rnn_d
    o_ref[...] = mix.astype(o_ref.dtype)


def _mix(attn, y, p, w_attn_proj, w_rnn_proj):
    return pl.pallas_call(
        _mix_kernel,
        out_shape=jax.ShapeDtypeStruct((T_ALL, D_MODEL), BF16),
        grid=(D_MODEL // COL_TILE, T_ALL // MIX_ROW_TILE),
        in_specs=[pl.BlockSpec((MIX_ROW_TILE, ATTN_OUT), lambda c, i: (i, 0)),
                  pl.BlockSpec((MIX_ROW_TILE, D_RNN), lambda c, i: (i, 0)),
                  pl.BlockSpec((MIX_ROW_TILE, COL_TILE), lambda c, i: (i, COL_GATE_ATTN + c)),
                  pl.BlockSpec((MIX_ROW_TILE, COL_TILE), lambda c, i: (i, COL_GATE_RNN + c)),
                  pl.BlockSpec((None, ATTN_OUT, COL_TILE), lambda c, i: (0, 0, c)),
                  pl.BlockSpec((None, D_RNN, COL_TILE), lambda c, i: (0, 0, c))],
        out_specs=pl.BlockSpec((MIX_ROW_TILE, COL_TILE), lambda c, i: (i, c)),
        scratch_shapes=[pltpu.VMEM((ATTN_OUT, COL_TILE), BF16), pltpu.VMEM((D_RNN, COL_TILE), BF16)],
        compiler_params=_params(("arbitrary", "arbitrary")),
        name="mix",
    )(attn, y, p, p, w_attn_proj, w_rnn_proj)


def _out_proj_kernel(mix_ref, w_ref, x_ref, o_ref, wbf):
    @pl.when(pl.program_id(1) == 0)
    def _():
        wbf[...] = w_ref[...].astype(BF16)

    o_ref[...] = x_ref[...] + jnp.dot(mix_ref[...], wbf[...], preferred_element_type=F32)


def _out_proj(mix, w_out, x):
    return pl.pallas_call(
        _out_proj_kernel,
        out_shape=jax.ShapeDtypeStruct((T_ALL, D_MODEL), F32),
        grid=(D_MODEL // COL_TILE, T_ALL // ROW_TILE),
        in_specs=[pl.BlockSpec((ROW_TILE, D_MODEL), lambda n, i: (i, 0)),
                  pl.BlockSpec((None, D_MODEL, COL_TILE), lambda n, i: (0, 0, n)),
                  pl.BlockSpec((ROW_TILE, COL_TILE), lambda n, i: (i, n))],
        out_specs=pl.BlockSpec((ROW_TILE, COL_TILE), lambda n, i: (i, n)),
        scratch_shapes=[pltpu.VMEM((D_MODEL, COL_TILE), BF16)],
        compiler_params=_params(("arbitrary", "arbitrary")),
        name="out_proj",
    )(mix, w_out, x)


def _split_bf16(x):
    hi = x.astype(BF16)
    lo = (x - hi.astype(F32)).astype(BF16)
    return hi, lo


def _router_kernel(x_ref, g_ref, w_ref, b_ref, xn_ref, sel_ref, gate_ref, rank_ref, cnt_ref, carry, tri):
    step = pl.program_id(0)

    @pl.when(step == 0)
    def _():
        carry[...] = jnp.zeros((1, LANES), F32)
        ri = lax.broadcasted_iota(I32, (NORM_TILE, NORM_TILE), 0)
        ci = lax.broadcasted_iota(I32, (NORM_TILE, NORM_TILE), 1)
        tri[...] = jnp.where(ci < ri, 1.0, 0.0).astype(BF16)

    x = x_ref[...]
    xn = x * lax.rsqrt(jnp.mean(x * x, axis=-1, keepdims=True) + EPS) * g_ref[...]
    xn_ref[...] = xn.astype(BF16)

    x_hi, x_lo = _split_bf16(xn)
    w_hi, w_lo = _split_bf16(w_ref[...])
    logits = (jnp.dot(x_hi, w_hi, preferred_element_type=F32)
              + jnp.dot(x_hi, w_lo, preferred_element_type=F32)
              + jnp.dot(x_lo, w_hi, preferred_element_type=F32)) + b_ref[...]

    lane = lax.broadcasted_iota(I32, (NORM_TILE, LANES), 1)
    work = logits
    vals, idxs = [], []
    for _ in range(TOP_K):
        mk = jnp.max(work, axis=-1, keepdims=True)
        ik = jnp.min(jnp.where(work == mk, lane, LANES), axis=-1, keepdims=True)
        vals.append(mk)
        idxs.append(ik)
        work = jnp.where(lane == ik, -jnp.inf, work)
    exps = [jnp.exp(v - vals[0]) for v in vals]
    den = exps[0] + exps[1] + exps[2] + exps[3]

    member = jnp.zeros((NORM_TILE, LANES), F32)
    for ik in idxs:
        member = member + jnp.where(lane == ik, 1.0, 0.0)
    before = jnp.dot(tri[...], member.astype(BF16), preferred_element_type=F32) + carry[...]
    sel = jnp.zeros((NORM_TILE, LANES), I32)
    gates = jnp.zeros((NORM_TILE, LANES), F32)
    ranks = jnp.zeros((NORM_TILE, LANES), F32)
    for k in range(TOP_K):
        rk = jnp.sum(jnp.where(lane == idxs[k], before, 0.0), axis=-1, keepdims=True)
        sel = jnp.where(lane == k, idxs[k], sel)
        gates = jnp.where(lane == k, exps[k] / den, gates)
        ranks = jnp.where(lane == k, rk, ranks)
    sel_ref[...] = sel
    gate_ref[...] = gates
    rank_ref[...] = ranks.astype(I32)
    carry[...] = carry[...] + jnp.sum(member, axis=0, keepdims=True)
    cnt_ref[...] = carry[...].astype(I32)


def _router(x2, g_ffn, w_router_pad, b_router_pad):
    tile_spec = pl.BlockSpec((NORM_TILE, LANES), lambda i: (i, 0))
    return pl.pallas_call(
        _router_kernel,
        out_shape=[jax.ShapeDtypeStruct((T_ALL, D_MODEL), BF16),
                   jax.ShapeDtypeStruct((T_ALL, LANES), I32),
                   jax.ShapeDtypeStruct((T_ALL, LANES), F32),
                   jax.ShapeDtypeStruct((T_ALL, LANES), I32),
                   jax.ShapeDtypeStruct((1, LANES), I32)],
        grid=(T_ALL // NORM_TILE,),
        in_specs=[pl.BlockSpec((NORM_TILE, D_MODEL), lambda i: (i, 0)),
                  pl.BlockSpec((1, D_MODEL), lambda i: (0, 0)),
                  pl.BlockSpec((D_MODEL, LANES), lambda i: (0, 0)),
                  pl.BlockSpec((1, LANES), lambda i: (0, 0))],
        out_specs=[pl.BlockSpec((NORM_TILE, D_MODEL), lambda i: (i, 0)),
                   tile_spec, tile_spec, tile_spec,
                   pl.BlockSpec((1, LANES), lambda i: (0, 0))],
        scratch_shapes=[pltpu.VMEM((1, LANES), F32), pltpu.VMEM((NORM_TILE, NORM_TILE), BF16)],
        compiler_params=_params(("arbitrary",)),
        name="router",
    )(x2, g_ffn, w_router_pad, b_router_pad)


def _experts_kernel(item_e, item_row0, item_nblk, used_blocks, xs_hbm, wg_ref, wu_ref, wd_ref, bg_ref, bu_ref,
                    bd_ref, out_hbm, xbuf, acc, wg_bf, wu_bf, wd_bf, sem_in, sem_out):
    i = pl.program_id(0)
    j = pl.program_id(1)
    n_j = pl.num_programs(1)
    nblk = item_nblk[i]
    row0 = item_row0[i]

    @pl.when(jnp.logical_and(i == 0, j == 0))
    def _():
        acc[0:EXPERT_BLOCK, :] = jnp.zeros((EXPERT_BLOCK, D_MODEL), F32)

        def zero_copy(c):
            dst = pl.ds(pl.multiple_of((used_blocks[0] + c) * EXPERT_BLOCK, EXPERT_BLOCK), EXPERT_BLOCK)
            return pltpu.make_async_copy(acc.at[0:EXPERT_BLOCK, :], out_hbm.at[dst, :], sem_out)

        n_slack = N_SORTED_BLOCKS - used_blocks[0]
        pl.loop(0, n_slack)(lambda c: zero_copy(c).start())
        pl.loop(0, n_slack)(lambda c: zero_copy(c).wait())

    def x_copy(c):
        rows = pl.ds(pl.multiple_of(c * EXPERT_BLOCK, EXPERT_BLOCK), EXPERT_BLOCK)
        src = pl.ds(pl.multiple_of(row0 + c * EXPERT_BLOCK, EXPERT_BLOCK), EXPERT_BLOCK)
        return pltpu.make_async_copy(xs_hbm.at[src, :], xbuf.at[rows, :], sem_in)

    def out_copy(c):
        rows = pl.ds(pl.multiple_of(c * EXPERT_BLOCK, EXPERT_BLOCK), EXPERT_BLOCK)
        dst = pl.ds(pl.multiple_of(row0 + c * EXPERT_BLOCK, EXPERT_BLOCK), EXPERT_BLOCK)
        return pltpu.make_async_copy(acc.at[rows, :], out_hbm.at[dst, :], sem_out)

    @pl.when(jnp.logical_and(j == 0, nblk > 0))
    def _():
        pl.loop(0, nblk)(lambda c: x_copy(c).start())
        pl.loop(0, nblk)(lambda c: x_copy(c).wait())

    @pl.when(nblk > 0)
    def _():
        wg_bf[...] = wg_ref[...].astype(BF16)
        wu_bf[...] = wu_ref[...].astype(BF16)
        wd_bf[...] = wd_ref[...].astype(BF16)

        def process(start, size):
            rows = pl.ds(start, size)
            xb = xbuf[rows, :]
            gt = jnp.dot(xb, wg_bf[...], preferred_element_type=F32) + bg_ref[...]
            up = jnp.dot(xb, wu_bf[...], preferred_element_type=F32) + bu_ref[...]
            gt = jnp.minimum(gt, SWIGLU_LIMIT)
            up = jnp.clip(up, -SWIGLU_LIMIT, SWIGLU_LIMIT)
            act = gt * jax.nn.sigmoid(SWIGLU_ALPHA * gt) * (up + 1.0)
            part = jnp.dot(act.astype(BF16), wd_bf[...], preferred_element_type=F32)

            @pl.when(j == 0)
            def _():
                acc[rows, :] = part + bd_ref[...]

            @pl.when(j > 0)
            def _():
                acc[rows, :] = acc[rows, :] + part

        @pl.loop(0, nblk // 2)
        def _(c):
            process(pl.multiple_of(c * (2 * EXPERT_BLOCK), 2 * EXPERT_BLOCK), 2 * EXPERT_BLOCK)

        @pl.when(nblk % 2 == 1)
        def _():
            process(pl.multiple_of((nblk - 1) * EXPERT_BLOCK, EXPERT_BLOCK), EXPERT_BLOCK)

    @pl.when(jnp.logical_and(j == n_j - 1, nblk > 0))
    def _():
        pl.loop(0, nblk)(lambda c: out_copy(c).start())
        pl.loop(0, nblk)(lambda c: out_copy(c).wait())


def _experts(item_e, item_row0, item_nblk, used_blocks, xs, w_gate, w_up, w_down, b_gate, b_up, b_down):
    n_j = D_FF // FF_TILE

    def jj(i, j, nblk):
        return jnp.where(nblk[i] > 0, j, n_j - 1)

    grid_spec = pltpu.PrefetchScalarGridSpec(
        num_scalar_prefetch=4,
        grid=(N_WORK_ITEMS, n_j),
        in_specs=[
            pl.BlockSpec(memory_space=pl.ANY),
            pl.BlockSpec((None, None, D_MODEL, FF_TILE), lambda i, j, e, r, n, u: (0, e[i], 0, jj(i, j, n))),
            pl.BlockSpec((None, None, D_MODEL, FF_TILE), lambda i, j, e, r, n, u: (0, e[i], 0, jj(i, j, n))),
            pl.BlockSpec((None, None, FF_TILE, D_MODEL), lambda i, j, e, r, n, u: (0, e[i], jj(i, j, n), 0)),
            pl.BlockSpec((None, 1, FF_TILE), lambda i, j, e, r, n, u: (e[i], 0, jj(i, j, n))),
            pl.BlockSpec((None, 1, FF_TILE), lambda i, j, e, r, n, u: (e[i], 0, jj(i, j, n))),
            pl.BlockSpec((None, 1, D_MODEL), lambda i, j, e, r, n, u: (e[i], 0, 0)),
        ],
        out_specs=pl.BlockSpec(memory_space=pl.ANY),
        scratch_shapes=[pltpu.VMEM((EXPERT_CAP, D_MODEL), BF16),
                        pltpu.VMEM((EXPERT_CAP, D_MODEL), F32),
                        pltpu.VMEM((D_MODEL, FF_TILE), BF16),
                        pltpu.VMEM((D_MODEL, FF_TILE), BF16),
                        pltpu.VMEM((FF_TILE, D_MODEL), BF16),
                        pltpu.SemaphoreType.DMA(()),
                        pltpu.SemaphoreType.DMA(())],
    )
    return pl.pallas_call(
        _experts_kernel,
        out_shape=jax.ShapeDtypeStruct((N_SORTED_ROWS, D_MODEL), F32),
        grid_spec=grid_spec,
        compiler_params=_params(("arbitrary", "arbitrary")),
        name="experts",
    )(item_e, item_row0, item_nblk, used_blocks, xs, w_gate, w_up, w_down, b_gate, b_up, b_down)


def _combine_kernel(x_ref, gate_ref, o0, o1, o2, o3, y_ref):
    gates = gate_ref[...]
    y = x_ref[...]
    for k, o_ref in enumerate((o0, o1, o2, o3)):
        y = y + gates[:, k:k + 1] * o_ref[...]
    y_ref[...] = y


def _combine(x2, gates, outs):
    row = pl.BlockSpec((COMBINE_TILE, D_MODEL), lambda i: (i, 0))
    return pl.pallas_call(
        _combine_kernel,
        out_shape=jax.ShapeDtypeStruct((T_ALL, D_MODEL), F32),
        grid=(T_ALL // COMBINE_TILE,),
        in_specs=[row, pl.BlockSpec((COMBINE_TILE, LANES), lambda i: (i, 0)), row, row, row, row],
        out_specs=row,
        compiler_params=_params(("parallel",)),
        name="combine",
    )(x2, gates, *outs)


def _rope_tables():
    half = ROT_DIM // 2
    inv = ROPE_THETA ** (-2.0 * jnp.arange(half, dtype=F32) / ROT_DIM)
    pos = jnp.concatenate([jnp.arange(SEQ), jnp.full((DEC_BATCH,), PAST_LEN)]).astype(F32)
    ang = pos[:, None] * inv[None, :]
    cos, sin = jnp.cos(ang), jnp.sin(ang)
    rest = HEAD_DIM - ROT_DIM
    cos_t = jnp.concatenate([cos, cos, jnp.ones((T_ALL, rest), F32)], axis=-1)
    sin_t = jnp.concatenate([-sin, sin, jnp.zeros((T_ALL, rest), F32)], axis=-1)
    return cos_t, sin_t


def _dispatch_plan(sel, ranks, counts):
    nb = (counts + EXPERT_BLOCK - 1) // EXPERT_BLOCK
    padded = nb * EXPERT_BLOCK
    pad_start = jnp.cumsum(padded) - padded
    dest = pad_start[sel] + ranks
    n_items = (nb + EXPERT_CAP_BLOCKS - 1) // EXPERT_CAP_BLOCKS
    item_end = jnp.cumsum(n_items)
    item_start = item_end - n_items
    w = jnp.arange(N_WORK_ITEMS, dtype=I32)
    total = item_end[-1]
    w_eff = jnp.minimum(w, total - 1)
    e_w = jnp.minimum(jnp.searchsorted(item_end, w_eff, side='right'), N_EXPERTS - 1).astype(I32)
    k_w = w_eff - item_start[e_w]
    row0 = pad_start[e_w] + k_w * EXPERT_CAP
    nblk = jnp.clip(nb[e_w] - k_w * EXPERT_CAP_BLOCKS, 0, EXPERT_CAP_BLOCKS)
    nblk = jnp.where(w < total, nblk, 0)
    used_blocks = jnp.sum(nb).astype(I32).reshape(1)
    return dest, e_w, row0.astype(I32), nblk.astype(I32), used_blocks


def kernel(x_prompt, x_sample, cache_k_w128, cache_v_w128, cache_k_w512, cache_v_w512, cache_k_w2048,
           cache_v_w2048, state_conv, state_rglru, g_mix, w_in, q_norm, k_norm, w_attn_proj, conv_w, conv_b,
           w_rg_a, b_rg_a, w_rg_x, b_rg_x, lru_lambda, w_rnn_proj, w_out, g_ffn, w_router, b_router,
           w_gate, b_gate, w_up, b_up, w_down, b_down):
    caches = (cache_k_w128, cache_v_w128, cache_k_w512, cache_v_w512, cache_k_w2048, cache_v_w2048)
    x = jnp.concatenate([x_prompt[0], x_sample[:, 0]], axis=0)

    xn = _rmsnorm_bf16(x, g_mix)
    cos_t, sin_t = _rope_tables()
    gains = jnp.stack([q_norm, k_norm])
    p = _in_proj(xn, w_in, cos_t, sin_t, gains)

    attn_p = _attn_prompt(p)
    qkv_s = p[SEQ:, :3 * ATTN_WIDTH].reshape(DEC_BATCH, 3, N_HEADS, HEAD_DIM)
    q_s, k_s, v_s = qkv_s[:, 0], qkv_s[:, 1], qkv_s[:, 2]
    views = [c.reshape(DEC_BATCH, ATTN_BLOCK, DILATIONS[n // 2], HEADS_PER_GROUP, HEAD_DIM)
             for n, c in enumerate(caches)]
    attn_s = _attn_sample(q_s, k_s, v_s, views).reshape(DEC_BATCH, ATTN_OUT)
    news = []
    for g in range(N_GROUPS):
        heads = slice(g * HEADS_PER_GROUP, (g + 1) * HEADS_PER_GROUP)
        news += [k_s[:, heads], v_s[:, heads]]
    new_caches = _cache_shift(caches, news)
    attn = jnp.concatenate([attn_p, attn_s], axis=0)

    y_p, h_p = _rglru_prompt(p, conv_w, conv_b, w_rg_a, b_rg_a, w_rg_x, b_rg_x, lru_lambda)
    hist = jnp.transpose(state_conv[0], (1, 0, 2))
    y_s, h_s = _rglru_sample(p, hist, state_rglru[0], conv_w, conv_b, w_rg_a, b_rg_a, w_rg_x, b_rg_x,
                             lru_lambda)
    y = jnp.concatenate([y_p, y_s], axis=0)

    mix = _mix(attn, y, p, w_attn_proj, w_rnn_proj)
    x2 = _out_proj(mix, w_out, x)

    w_router_pad = jnp.pad(w_router[0], ((0, 0), (0, LANES - N_EXPERTS)))
    b_router_pad = jnp.pad(b_router, ((0, 0), (0, LANES - N_EXPERTS)), constant_values=NEG_BIG)
    xn2, sel, gates, ranks, counts = _router(x2, g_ffn, w_router_pad, b_router_pad)
    dest, item_e, item_row0, item_nblk, used_blocks = _dispatch_plan(
        sel[:, :TOP_K], ranks[:, :TOP_K], counts[0, :N_EXPERTS])
    tok = jnp.repeat(jnp.arange(T_ALL, dtype=I32), TOP_K)
    src_tok = jnp.zeros((N_SORTED_ROWS,), I32).at[dest.reshape(-1)].set(tok)
    xs = xn2[src_tok]
    out_sorted = _experts(item_e, item_row0, item_nblk, used_blocks, xs, w_gate, w_up, w_down,
                          b_gate.reshape(N_EXPERTS, 1, D_FF), b_up.reshape(N_EXPERTS, 1, D_FF),
                          b_down.reshape(N_EXPERTS, 1, D_MODEL))
    outs = [out_sorted[dest[:, k]] for k in range(TOP_K)]
    y_all = _combine(x2, gates, outs)

    y_prompt = y_all[:SEQ].reshape(1, SEQ, D_MODEL)
    y_sample = y_all[SEQ:].reshape(DEC_BATCH, 1, D_MODEL)
    states_p = []
    for g, w in enumerate(WINDOWS):
        keep = min(w, SEQ)
        for off in (ATTN_WIDTH, 2 * ATTN_WIDTH):
            c0 = off + g * ATTN_OUT
            states_p.append(p[SEQ - keep:SEQ, c0:c0 + ATTN_OUT].reshape(1, 1, keep, HEADS_PER_GROUP, HEAD_DIM))
    conv_p = p[SEQ - (CONV_W - 1):SEQ, 3 * ATTN_WIDTH:3 * ATTN_WIDTH + D_RNN].reshape(1, 1, CONV_W - 1, D_RNN)
    rglru_p = h_p.reshape(1, 1, D_RNN)
    rnn_x_s = p[SEQ:, 3 * ATTN_WIDTH:3 * ATTN_WIDTH + D_RNN]
    conv_s = jnp.concatenate([state_conv[0][:, 1:], rnn_x_s[:, None, :]], axis=1)[None]
    rglru_s = h_s[None]
    return (y_prompt, y_sample, *states_p, conv_p, rglru_p, *new_caches, conv_s, rglru_s)
```

```python
import functools

import jax
import jax.numpy as jnp
from jax import lax
from jax.experimental import pallas as pl
from jax.experimental.pallas import tpu as pltpu

F32 = jnp.float32
BF16 = jnp.bfloat16
I32 = jnp.int32

D_MODEL = 2048
SEQ = 8192
DEC_BATCH = 128
PAST_LEN = 2048
T_ALL = SEQ + DEC_BATCH

HEAD_DIM = 128
HEADS_PER_GROUP = 4
WINDOWS = (128, 512, 2048)
DILATIONS = (1, 4, 16)
N_GROUPS = 3
N_HEADS = N_GROUPS * HEADS_PER_GROUP
ATTN_WIDTH = N_HEADS * HEAD_DIM
ATTN_OUT = HEADS_PER_GROUP * HEAD_DIM
ATTN_BLOCK = 128
ROT_DIM = HEAD_DIM // 4
ROPE_THETA = 500000.0
D_RNN = D_MODEL
RNN_BLOCK_W = 128
CONV_W = 4
LRU_C = 8.0
N_EXPERTS = 32
TOP_K = 4
D_FF = D_MODEL
SWIGLU_LIMIT = 7.0
SWIGLU_ALPHA = 1.702
EPS = 1e-6
IN_WIDTH = 3 * ATTN_WIDTH + 2 * D_RNN + 2 * D_MODEL

LANES = 128
SUBLANES = 8
VMEM_LIMIT_BYTES = 56 * 1024 * 1024

COL_TILE = 512
COL_RNN_X = (3 * ATTN_WIDTH) // COL_TILE
COL_RNN_GATE = (3 * ATTN_WIDTH + D_RNN) // COL_TILE
COL_GATE_ATTN = (3 * ATTN_WIDTH + 2 * D_RNN) // COL_TILE
COL_GATE_RNN = (3 * ATTN_WIDTH + 2 * D_RNN + D_MODEL) // COL_TILE

ROW_TILE = 1664
MIX_ROW_TILE = 832
NORM_TILE = 640
COMBINE_TILE = 320
SPAN = 2048
NEG_BIG = -1e30

EXPERT_BLOCK = 128
EXPERT_CAP_BLOCKS = 16
EXPERT_CAP = EXPERT_BLOCK * EXPERT_CAP_BLOCKS
FF_TILE = 256
N_SLOTS = T_ALL * TOP_K
N_SORTED_ROWS = N_SLOTS + N_EXPERTS * EXPERT_BLOCK
N_SORTED_BLOCKS = N_SORTED_ROWS // EXPERT_BLOCK
N_WORK_ITEMS = N_EXPERTS + -(-N_SORTED_BLOCKS // EXPERT_CAP_BLOCKS)


def _params(semantics, vmem=VMEM_LIMIT_BYTES):
    return pltpu.CompilerParams(dimension_semantics=semantics, vmem_limit_bytes=vmem)


def _rmsnorm_kernel(x_ref, g_ref, o_ref):
    x = x_ref[...]
    y = x * lax.rsqrt(jnp.mean(x * x, axis=-1, keepdims=True) + EPS)
    o_ref[...] = (y * g_ref[...]).astype(o_ref.dtype)


def _rmsnorm_bf16(x, g):
    t = x.shape[0]
    return pl.pallas_call(
        _rmsnorm_kernel,
        out_shape=jax.ShapeDtypeStruct((t, D_MODEL), BF16),
        grid=(t // NORM_TILE,),
        in_specs=[pl.BlockSpec((NORM_TILE, D_MODEL), lambda i: (i, 0)),
                  pl.BlockSpec((1, D_MODEL), lambda i: (0, 0))],
        out_specs=pl.BlockSpec((NORM_TILE, D_MODEL), lambda i: (i, 0)),
        compiler_params=_params(("parallel",)),
        name="rmsnorm_bf16",
    )(x, g)


def _in_proj_kernel(x_ref, w_ref, cos_ref, sin_ref, gain_ref, o_ref, wbf_ref):
    j = pl.program_id(0)

    @pl.when(pl.program_id(1) == 0)
    def _():
        wbf_ref[...] = w_ref[...].astype(BF16)

    o_ref[...] = jnp.dot(x_ref[...], wbf_ref[...], preferred_element_type=F32)

    @pl.when(j < 2 * ATTN_WIDTH // COL_TILE)
    def _():
        lane = lax.broadcasted_iota(I32, (ROW_TILE, HEAD_DIM), 1)
        first_half = lane < ROT_DIM // 2
        gain = gain_ref[...]
        cos = cos_ref[...]
        sin = sin_ref[...]
        for h in range(COL_TILE // HEAD_DIM):
            cols = slice(h * HEAD_DIM, (h + 1) * HEAD_DIM)
            xh = o_ref[:, cols]
            y = xh * lax.rsqrt(jnp.mean(xh * xh, axis=-1, keepdims=True) + EPS) * gain
            partner = jnp.where(first_half,
                                pltpu.roll(y, HEAD_DIM - ROT_DIM // 2, 1),
                                pltpu.roll(y, ROT_DIM // 2, 1))
            o_ref[:, cols] = y * cos + partner * sin


def _in_proj(xn, w_in, cos_t, sin_t, gains):
    n_tiles = IN_WIDTH // COL_TILE
    qk_tiles = ATTN_WIDTH // COL_TILE
    return pl.pallas_call(
        _in_proj_kernel,
        out_shape=jax.ShapeDtypeStruct((T_ALL, IN_WIDTH), F32),
        grid=(n_tiles, T_ALL // ROW_TILE),
        in_specs=[
            pl.BlockSpec((ROW_TILE, D_MODEL), lambda j, i: (i, 0)),
            pl.BlockSpec((None, D_MODEL, COL_TILE), lambda j, i: (0, 0, j)),
            pl.BlockSpec((ROW_TILE, HEAD_DIM), lambda j, i: (i, 0)),
            pl.BlockSpec((ROW_TILE, HEAD_DIM), lambda j, i: (i, 0)),
            pl.BlockSpec((None, 1, HEAD_DIM), lambda j, i: (jnp.minimum(j // qk_tiles, 1), 0, 0)),
        ],
        out_specs=pl.BlockSpec((ROW_TILE, COL_TILE), lambda j, i: (i, j)),
        scratch_shapes=[pltpu.VMEM((D_MODEL, COL_TILE), BF16)],
        compiler_params=_params(("arbitrary", "arbitrary")),
        name="in_proj",
    )(xn, w_in, cos_t, sin_t, gains)


def _dot_nt(a, b):
    return lax.dot_general(a, b, (((1,), (1,)), ((), ())), preferred_element_type=F32)


def _attn_prompt_kernel(*refs):
    ins = refs[:15]
    o_ref = refs[15]
    og_ref, lse_ref = refs[16], refs[17]
    span_idx = pl.program_id(0)
    qi = lax.broadcasted_iota(I32, (ATTN_BLOCK, ATTN_BLOCK), 0)
    kj = lax.broadcasted_iota(I32, (ATTN_BLOCK, ATTN_BLOCK), 1)
    mask_cur = qi >= kj
    mask_prev_band = kj >= qi
    scale = HEAD_DIM ** -0.5

    for g, dil in enumerate(DILATIONS):
        q_ref, kc_ref, vc_ref, kp_ref, vp_ref = ins[5 * g:5 * g + 5]
        blk = ATTN_BLOCK * dil
        for m in range(SPAN // blk):
            for r in range(dil):
                rows = pl.ds(m * blk + r, ATTN_BLOCK, stride=dil) if dil > 1 else pl.ds(m * blk, ATTN_BLOCK)
                q = (q_ref[rows, :] * scale).astype(BF16)
                k_cur = kc_ref[rows, :].astype(BF16)
                v_cur = vc_ref[rows, :].astype(BF16)
                if m == 0:
                    prow = pl.ds(r, ATTN_BLOCK, stride=dil) if dil > 1 else pl.ds(0, ATTN_BLOCK)
                    k_prev = kp_ref[prow, :].astype(BF16)
                    v_prev = vp_ref[prow, :].astype(BF16)
                    mask_prev = jnp.logical_and(mask_prev_band, span_idx > 0)
                else:
                    prow = (pl.ds((m - 1) * blk + r, ATTN_BLOCK, stride=dil) if dil > 1
                            else pl.ds((m - 1) * blk, ATTN_BLOCK))
                    k_prev = kc_ref[prow, :].astype(BF16)
                    v_prev = vc_ref[prow, :].astype(BF16)
                    mask_prev = mask_prev_band
                s_cur = jnp.where(mask_cur, _dot_nt(q, k_cur), NEG_BIG)
                s_prev = jnp.where(mask_prev, _dot_nt(q, k_prev), NEG_BIG)
                mx = jnp.maximum(jnp.max(s_cur, axis=-1, keepdims=True),
                                 jnp.max(s_prev, axis=-1, keepdims=True))
                p_cur = jnp.exp(s_cur - mx)
                p_prev = jnp.exp(s_prev - mx)
                den = jnp.sum(p_cur, axis=-1, keepdims=True) + jnp.sum(p_prev, axis=-1, keepdims=True)
                pv = (jnp.dot(p_cur.astype(BF16), v_cur, preferred_element_type=F32)
                      + jnp.dot(p_prev.astype(BF16), v_prev, preferred_element_type=F32))
                og_ref[g, rows, :] = pv / den
                lse_ref[g, rows, :] = jnp.broadcast_to(mx + jnp.log(den), (ATTN_BLOCK, HEAD_DIM))

    lse = lse_ref[...]
    top = jnp.max(lse, axis=0)
    w = jnp.exp(lse - top[None])
    o_ref[...] = jnp.sum(w * og_ref[...], axis=0) / jnp.sum(w, axis=0)


def _attn_prompt(p):
    q_cols, k_cols, v_cols = 0, N_HEADS, 2 * N_HEADS
    in_specs = []
    for g, dil in enumerate(DILATIONS):
        blk = ATTN_BLOCK * dil
        per_span = SPAN // blk

        def cur(off, g=g):
            return pl.BlockSpec((SPAN, HEAD_DIM), lambda s, h: (s, off + g * HEADS_PER_GROUP + h))

        def prev(off, g=g, blk=blk, per_span=per_span):
            return pl.BlockSpec((blk, HEAD_DIM),
                                lambda s, h: (jnp.maximum(s * per_span - 1, 0), off + g * HEADS_PER_GROUP + h))

        in_specs += [cur(q_cols), cur(k_cols), cur(v_cols), prev(k_cols), prev(v_cols)]
    return pl.pallas_call(
        _attn_prompt_kernel,
        out_shape=jax.ShapeDtypeStruct((SEQ, ATTN_OUT), F32),
        grid=(SEQ // SPAN, HEADS_PER_GROUP),
        in_specs=in_specs,
        out_specs=pl.BlockSpec((SPAN, HEAD_DIM), lambda s, h: (s, h)),
        scratch_shapes=[pltpu.VMEM((N_GROUPS, SPAN, HEAD_DIM), F32),
                        pltpu.VMEM((N_GROUPS, SPAN, HEAD_DIM), F32)],
        compiler_params=_params(("parallel", "parallel")),
        name="attn_prompt",
    )(*([p] * 15))


SAMPLE_BB = 8


def _attn_sample_kernel(q_ref, k_ref, v_ref, ck0, cv0, ck1, cv1, ck2, cv2, o_ref):
    caches = ((ck0, cv0), (ck1, cv1), (ck2, cv2))
    scale = HEAD_DIM ** -0.5
    for b in range(SAMPLE_BB):
        outs, lses = [], []
        for g in range(N_GROUPS):
            heads = slice(g * HEADS_PER_GROUP, (g + 1) * HEADS_PER_GROUP)
            q = q_ref[b, heads, :] * scale
            k_new = k_ref[b, heads, :]
            v_new = v_ref[b, heads, :]
            k_old = caches[g][0][b]
            v_old = caches[g][1][b]
            s_old = jnp.sum(k_old * q[None], axis=-1, keepdims=True)
            s_new = jnp.sum(k_new * q, axis=-1, keepdims=True)
            mx = jnp.maximum(jnp.max(s_old, axis=0), s_new)
            p_old = jnp.exp(s_old - mx[None])
            p_new = jnp.exp(s_new - mx)
            den = jnp.sum(p_old, axis=0) + p_new
            pv = jnp.sum(p_old * v_old, axis=0) + p_new * v_new
            outs.append(pv / den)
            lses.append(mx + jnp.log(den))
        top = jnp.maximum(jnp.maximum(lses[0], lses[1]), lses[2])
        ws = [jnp.exp(l - top) for l in lses]
        o_ref[b] = (ws[0] * outs[0] + ws[1] * outs[1] + ws[2] * outs[2]) / (ws[0] + ws[1] + ws[2])


def _attn_sample(q_s, k_s, v_s, cache_views):
    row = pl.BlockSpec((SAMPLE_BB, N_HEADS, HEAD_DIM), lambda b: (b, 0, 0))
    cache_spec = pl.BlockSpec((SAMPLE_BB, ATTN_BLOCK, None, HEADS_PER_GROUP, HEAD_DIM),
                              lambda b: (b, 0, 0, 0, 0))
    return pl.pallas_call(
        _attn_sample_kernel,
        out_shape=jax.ShapeDtypeStruct((DEC_BATCH, HEADS_PER_GROUP, HEAD_DIM), F32),
        grid=(DEC_BATCH // SAMPLE_BB,),
        in_specs=[row, row, row] + [cache_spec] * 6,
        out_specs=pl.BlockSpec((SAMPLE_BB, HEADS_PER_GROUP, HEAD_DIM), lambda b: (b, 0, 0)),
        compiler_params=_params(("parallel",)),
        name="attn_sample",
    )(q_s, k_s, v_s, *cache_views)


SHIFT_ROWS = 2048


def _shift_one_cache(cache, new, out):
    w = cache.shape[2]
    bb = SHIFT_ROWS // w
    n_chunks = DEC_BATCH // bb

    def run(buf, sem_in, sem_out):
        def in_copy(c, slot):
            return pltpu.make_async_copy(cache.at[0, pl.ds(c * bb, bb), pl.ds(1, w - 1)],
                                         buf.at[slot, :, pl.ds(0, w - 1)], sem_in.at[slot])

        def out_copy(c, slot):
            return pltpu.make_async_copy(buf.at[slot], out.at[0, pl.ds(c * bb, bb)], sem_out.at[slot])

        in_copy(0, 0).start()

        @pl.loop(0, n_chunks)
        def _(c):
            slot = c % 2
            other = 1 - slot

            @pl.when(c + 1 < n_chunks)
            def _():
                @pl.when(c >= 1)
                def _():
                    out_copy(c - 1, other).wait()

                in_copy(c + 1, other).start()

            in_copy(c, slot).wait()
            buf[slot, :, w - 1] = new[pl.ds(c * bb, bb)]
            out_copy(c, slot).start()

        out_copy(n_chunks - 2, n_chunks % 2).wait()
        out_copy(n_chunks - 1, (n_chunks - 1) % 2).wait()

    pl.run_scoped(run, pltpu.VMEM((2, bb, w, HEADS_PER_GROUP, HEAD_DIM), F32),
                  pltpu.SemaphoreType.DMA((2,)), pltpu.SemaphoreType.DMA((2,)))


def _cache_shift_kernel(*refs):
    caches, news, outs = refs[0:6], refs[6:12], refs[12:18]
    for a in range(6):
        _shift_one_cache(caches[a], news[a], outs[a])


def _cache_shift(caches, news):
    any_spec = pl.BlockSpec(memory_space=pl.ANY)
    vmem_spec = pl.BlockSpec(memory_space=pltpu.VMEM)
    return pl.pallas_call(
        _cache_shift_kernel,
        out_shape=[jax.ShapeDtypeStruct(c.shape, c.dtype) for c in caches],
        in_specs=[any_spec] * 6 + [vmem_spec] * 6,
        out_specs=[any_spec] * 6,
        compiler_params=pltpu.CompilerParams(vmem_limit_bytes=VMEM_LIMIT_BYTES),
        name="cache_shift",
    )(*caches, *news)


def _rglru_gates(xc, wa_ref, ba_ref, wx_ref, bx_ref, lam_ref):
    r_parts, i_parts = [], []
    for n in range(COL_TILE // RNN_BLOCK_W):
        xb = xc[:, n * RNN_BLOCK_W:(n + 1) * RNN_BLOCK_W].astype(BF16)
        r_parts.append(jnp.dot(xb, wa_ref[n].astype(BF16), preferred_element_type=F32))
        i_parts.append(jnp.dot(xb, wx_ref[n].astype(BF16), preferred_element_type=F32))
    r = jax.nn.sigmoid(jnp.concatenate(r_parts, axis=-1) + ba_ref[...])
    i = jax.nn.sigmoid(jnp.concatenate(i_parts, axis=-1) + bx_ref[...])
    neg_lam = -lam_ref[...]
    softplus = jnp.maximum(neg_lam, 0.0) + jnp.log1p(jnp.exp(-jnp.abs(neg_lam)))
    log_a = -LRU_C * r * softplus
    a = jnp.exp(log_a)
    b = jnp.sqrt(-jnp.tanh(log_a) * (jnp.exp(2.0 * log_a) + 1.0)) * i * xc
    return a, b


RNN_T_TILE = 512
CONV_PAD = SUBLANES


def _rglru_prompt_kernel(x_ref, gate_ref, cw_ref, cb_ref, wa_ref, ba_ref, wx_ref, bx_ref, lam_ref,
                         y_ref, hlast_ref, xbuf, a_scr, b_scr, h_scr, carry):
    t = pl.program_id(1)

    @pl.when(t == 0)
    def _():
        xbuf[0:CONV_PAD, :] = jnp.zeros((CONV_PAD, COL_TILE), F32)
        carry[...] = jnp.zeros((SUBLANES, COL_TILE), F32)

    @pl.when(t > 0)
    def _():
        xbuf[0:CONV_PAD, :] = xbuf[RNN_T_TILE:RNN_T_TILE + CONV_PAD, :]

    xbuf[CONV_PAD:CONV_PAD + RNN_T_TILE, :] = x_ref[...]
    xc = cb_ref[...] + sum(
        xbuf[pl.ds(CONV_PAD - (CONV_W - 1) + j, RNN_T_TILE), :] * cw_ref[j:j + 1, :] for j in range(CONV_W))
    a, b = _rglru_gates(xc, wa_ref, ba_ref, wx_ref, bx_ref, lam_ref)
    a_scr[...] = a
    b_scr[...] = b

    row = lax.broadcasted_iota(I32, (SUBLANES, COL_TILE), 0)

    def chunk(c, h):
        rows = pl.ds(pl.multiple_of(c * SUBLANES, SUBLANES), SUBLANES)
        ac = a_scr[rows, :]
        bc = b_scr[rows, :]
        for s in (1, 2, 4):
            a_sh = jnp.where(row >= s, pltpu.roll(ac, s, 0), 1.0)
            b_sh = jnp.where(row >= s, pltpu.roll(bc, s, 0), 0.0)
            bc = ac * b_sh + bc
            ac = ac * a_sh
        hh = ac * h + bc
        h_scr[rows, :] = hh
        return jnp.broadcast_to(hh[SUBLANES - 1:SUBLANES, :], (SUBLANES, COL_TILE))

    h_end = lax.fori_loop(0, RNN_T_TILE // SUBLANES, chunk, carry[...], unroll=4)
    carry[...] = h_end
    hlast_ref[...] = h_end[0:1, :]
    y_ref[...] = (h_scr[...] * jax.nn.gelu(gate_ref[...])).astype(y_ref.dtype)


def _rnn_param_specs(idx):
    return [
        pl.BlockSpec((None, CONV_W, COL_TILE), lambda *g: (0, 0, idx(*g))),
        pl.BlockSpec((1, COL_TILE), lambda *g: (0, idx(*g))),
        pl.BlockSpec((None, COL_TILE // RNN_BLOCK_W, RNN_BLOCK_W, RNN_BLOCK_W), lambda *g: (0, idx(*g), 0, 0)),
        pl.BlockSpec((1, COL_TILE), lambda *g: (0, idx(*g))),
        pl.BlockSpec((None, COL_TILE // RNN_BLOCK_W, RNN_BLOCK_W, RNN_BLOCK_W), lambda *g: (0, idx(*g), 0, 0)),
        pl.BlockSpec((1, COL_TILE), lambda *g: (0, idx(*g))),
        pl.BlockSpec((1, COL_TILE), lambda *g: (0, idx(*g))),
    ]


def _rglru_prompt(p, conv_w, conv_b, w_rg_a, b_rg_a, w_rg_x, b_rg_x, lru_lambda):
    n_c = D_RNN // COL_TILE
    return pl.pallas_call(
        _rglru_prompt_kernel,
        out_shape=[jax.ShapeDtypeStruct((SEQ, D_RNN), BF16), jax.ShapeDtypeStruct((1, D_RNN), F32)],
        grid=(n_c, SEQ // RNN_T_TILE),
        in_specs=[pl.BlockSpec((RNN_T_TILE, COL_TILE), lambda c, t: (t, COL_RNN_X + c)),
                  pl.BlockSpec((RNN_T_TILE, COL_TILE), lambda c, t: (t, COL_RNN_GATE + c))]
                 + _rnn_param_specs(lambda c, t: c),
        out_specs=[pl.BlockSpec((RNN_T_TILE, COL_TILE), lambda c, t: (t, c)),
                   pl.BlockSpec((1, COL_TILE), lambda c, t: (0, c))],
        scratch_shapes=[pltpu.VMEM((CONV_PAD + RNN_T_TILE, COL_TILE), F32),
                        pltpu.VMEM((RNN_T_TILE, COL_TILE), F32),
                        pltpu.VMEM((RNN_T_TILE, COL_TILE), F32),
                        pltpu.VMEM((RNN_T_TILE, COL_TILE), F32),
                        pltpu.VMEM((SUBLANES, COL_TILE), F32)],
        compiler_params=_params(("parallel", "arbitrary")),
        name="rglru_prompt",
    )(p, p, conv_w, conv_b, w_rg_a, b_rg_a, w_rg_x, b_rg_x, lru_lambda)


def _rglru_sample_kernel(x_ref, gate_ref, hist_ref, h0_ref, cw_ref, cb_ref, wa_ref, ba_ref, wx_ref, bx_ref,
                         lam_ref, y_ref, h_ref):
    xc = cb_ref[...] + x_ref[...] * cw_ref[CONV_W - 1:CONV_W, :]
    for j in range(CONV_W - 1):
        xc = xc + hist_ref[j] * cw_ref[j:j + 1, :]
    a, b = _rglru_gates(xc, wa_ref, ba_ref, wx_ref, bx_ref, lam_ref)
    h = a * h0_ref[...] + b
    h_ref[...] = h
    y_ref[...] = (h * jax.nn.gelu(gate_ref[...])).astype(y_ref.dtype)


def _rglru_sample(p, hist, h0, conv_w, conv_b, w_rg_a, b_rg_a, w_rg_x, b_rg_x, lru_lambda):
    n_c = D_RNN // COL_TILE
    row_blk = SEQ // DEC_BATCH
    return pl.pallas_call(
        _rglru_sample_kernel,
        out_shape=[jax.ShapeDtypeStruct((DEC_BATCH, D_RNN), BF16), jax.ShapeDtypeStruct((DEC_BATCH, D_RNN), F32)],
        grid=(n_c,),
        in_specs=[pl.BlockSpec((DEC_BATCH, COL_TILE), lambda c: (row_blk, COL_RNN_X + c)),
                  pl.BlockSpec((DEC_BATCH, COL_TILE), lambda c: (row_blk, COL_RNN_GATE + c)),
                  pl.BlockSpec((CONV_W - 1, DEC_BATCH, COL_TILE), lambda c: (0, 0, c)),
                  pl.BlockSpec((DEC_BATCH, COL_TILE), lambda c: (0, c))]
                 + _rnn_param_specs(lambda c: c),
        out_specs=[pl.BlockSpec((DEC_BATCH, COL_TILE), lambda c: (0, c)),
                   pl.BlockSpec((DEC_BATCH, COL_TILE), lambda c: (0, c))],
        compiler_params=_params(("parallel",)),
        name="rglru_sample",
    )(p, p, hist, h0, conv_w, conv_b, w_rg_a, b_rg_a, w_rg_x, b_rg_x, lru_lambda)


def _mix_kernel(attn_ref, y_ref, ga_ref, gr_ref, wap_ref, wrp_ref, o_ref, wap_bf, wrp_bf):
    @pl.when(pl.program_id(1) == 0)
    def _():
        wap_bf[...] = wap_ref[...].astype(BF16)
        wrp_bf[...] = wrp_ref[...].astype(BF16)

    attn_d = jnp.dot(attn_ref[...].astype(BF16), wap_bf[...], preferred_element_type=F32)
    rnn_d = jnp.dot(y_ref[...], wrp_bf[...], preferred_element_type=F32)
    mix = jax.nn.sigmoid(ga_ref[...]) * attn_d + jax.nn.sigmoid(gr_ref[...]) * rnn_d
    o_ref[...] = mix.astype(o_ref.dtype)


def _mix(attn, y, p, w_attn_proj, w_rnn_proj):
    return pl.pallas_call(
        _mix_kernel,
        out_shape=jax.ShapeDtypeStruct((T_ALL, D_MODEL), BF16),
        grid=(D_MODEL // COL_TILE, T_ALL // MIX_ROW_TILE),
        in_specs=[pl.BlockSpec((MIX_ROW_TILE, ATTN_OUT), lambda c, i: (i, 0)),
                  pl.BlockSpec((MIX_ROW_TILE, D_RNN), lambda c, i: (i, 0)),
                  pl.BlockSpec((MIX_ROW_TILE, COL_TILE), lambda c, i: (i, COL_GATE_ATTN + c)),
                  pl.BlockSpec((MIX_ROW_TILE, COL_TILE), lambda c, i: (i, COL_GATE_RNN + c)),
                  pl.BlockSpec((None, ATTN_OUT, COL_TILE), lambda c, i: (0, 0, c)),
                  pl.BlockSpec((None, D_RNN, COL_TILE), lambda c, i: (0, 0, c))],
        out_specs=pl.BlockSpec((MIX_ROW_TILE, COL_TILE), lambda c, i: (i, c)),
        scratch_shapes=[pltpu.VMEM((ATTN_OUT, COL_TILE), BF16), pltpu.VMEM((D_RNN, COL_TILE), BF16)],
        compiler_params=_params(("arbitrary", "arbitrary")),
        name="mix",
    )(attn, y, p, p, w_attn_proj, w_rnn_proj)


def _out_proj_kernel(mix_ref, w_ref, x_ref, o_ref, wbf):
    @pl.when(pl.program_id(1) == 0)
    def _():
        wbf[...] = w_ref[...].astype(BF16)

    o_ref[...] = x_ref[...] + jnp.dot(mix_ref[...], wbf[...], preferred_element_type=F32)


def _out_proj(mix, w_out, x):
    return pl.pallas_call(
        _out_proj_kernel,
        out_shape=jax.ShapeDtypeStruct((T_ALL, D_MODEL), F32),
        grid=(D_MODEL // COL_TILE, T_ALL // ROW_TILE),
        in_specs=[pl.BlockSpec((ROW_TILE, D_MODEL), lambda n, i: (i, 0)),
                  pl.BlockSpec((None, D_MODEL, COL_TILE), lambda n, i: (0, 0, n)),
                  pl.BlockSpec((ROW_TILE, COL_TILE), lambda n, i: (i, n))],
        out_specs=pl.BlockSpec((ROW_TILE, COL_TILE), lambda n, i: (i, n)),
        scratch_shapes=[pltpu.VMEM((D_MODEL, COL_TILE), BF16)],
        compiler_params=_params(("arbitrary", "arbitrary")),
        name="out_proj",
    )(mix, w_out, x)


def _split_bf16(x):
    hi = x.astype(BF16)
    lo = (x - hi.astype(F32)).astype(BF16)
    return hi, lo


def _router_kernel(x_ref, g_ref, w_ref, b_ref, xn_ref, sel_ref, gate_ref, rank_ref, cnt_ref, carry, tri):
    step = pl.program_id(0)

    @pl.when(step == 0)
    def _():
        carry[...] = jnp.zeros((1, LANES), F32)
        ri = lax.broadcasted_iota(I32, (NORM_TILE, NORM_TILE), 0)
        ci = lax.broadcasted_iota(I32, (NORM_TILE, NORM_TILE), 1)
        tri[...] = jnp.where(ci < ri, 1.0, 0.0).astype(BF16)

    x = x_ref[...]
    xn = x * lax.rsqrt(jnp.mean(x * x, axis=-1, keepdims=True) + EPS) * g_ref[...]
    xn_ref[...] = xn.astype(BF16)

    x_hi, x_lo = _split_bf16(xn)
    w_hi, w_lo = _split_bf16(w_ref[...])
    logits = (jnp.dot(x_hi, w_hi, preferred_element_type=F32)
              + jnp.dot(x_hi, w_lo, preferred_element_type=F32)
              + jnp.dot(x_lo, w_hi, preferred_element_type=F32)) + b_ref[...]

    lane = lax.broadcasted_iota(I32, (NORM_TILE, LANES), 1)
    work = logits
    vals, idxs = [], []
    for _ in range(TOP_K):
        mk = jnp.max(work, axis=-1, keepdims=True)
        ik = jnp.min(jnp.where(work == mk, lane, LANES), axis=-1, keepdims=True)
        vals.append(mk)
        idxs.append(ik)
        work = jnp.where(lane == ik, -jnp.inf, work)
    exps = [jnp.exp(v - vals[0]) for v in vals]
    den = exps[0] + exps[1] + exps[2] + exps[3]

    member = jnp.zeros((NORM_TILE, LANES), F32)
    for ik in idxs:
        member = member + jnp.where(lane == ik, 1.0, 0.0)
    before = jnp.dot(tri[...], member.astype(BF16), preferred_element_type=F32) + carry[...]
    sel = jnp.zeros((NORM_TILE, LANES), I32)
    gates = jnp.zeros((NORM_TILE, LANES), F32)
    ranks = jnp.zeros((NORM_TILE, LANES), F32)
    for k in range(TOP_K):
        rk = jnp.sum(jnp.where(lane == idxs[k], before, 0.0), axis=-1, keepdims=True)
        sel = jnp.where(lane == k, idxs[k], sel)
        gates = jnp.where(lane == k, exps[k] / den, gates)
        ranks = jnp.where(lane == k, rk, ranks)
    sel_ref[...] = sel
    gate_ref[...] = gates
    rank_ref[...] = ranks.astype(I32)
    carry[...] = carry[...] + jnp.sum(member, axis=0, keepdims=True)
    cnt_ref[...] = carry[...].astype(I32)


def _router(x2, g_ffn, w_router_pad, b_router_pad):
    tile_spec = pl.BlockSpec((NORM_TILE, LANES), lambda i: (i, 0))
    return pl.pallas_call(
        _router_kernel,
        out_shape=[jax.ShapeDtypeStruct((T_ALL, D_MODEL), BF16),
                   jax.ShapeDtypeStruct((T_ALL, LANES), I32),
                   jax.ShapeDtypeStruct((T_ALL, LANES), F32),
                   jax.ShapeDtypeStruct((T_ALL, LANES), I32),
                   jax.ShapeDtypeStruct((1, LANES), I32)],
        grid=(T_ALL // NORM_TILE,),
        in_specs=[pl.BlockSpec((NORM_TILE, D_MODEL), lambda i: (i, 0)),
                  pl.BlockSpec((1, D_MODEL), lambda i: (0, 0)),
                  pl.BlockSpec((D_MODEL, LANES), lambda i: (0, 0)),
                  pl.BlockSpec((1, LANES), lambda i: (0, 0))],
        out_specs=[pl.BlockSpec((NORM_TILE, D_MODEL), lambda i: (i, 0)),
                   tile_spec, tile_spec, tile_spec,
                   pl.BlockSpec((1, LANES), lambda i: (0, 0))],
        scratch_shapes=[pltpu.VMEM((1, LANES), F32), pltpu.VMEM((NORM_TILE, NORM_TILE), BF16)],
        compiler_params=_params(("arbitrary",)),
        name="router",
    )(x2, g_ffn, w_router_pad, b_router_pad)


def _experts_kernel(item_e, item_row0, item_nblk, used_blocks, xs_hbm, wg_ref, wu_ref, wd_ref, bg_ref, bu_ref,
                    bd_ref, out_hbm, xbuf, acc, wg_bf, wu_bf, wd_bf, sem_in, sem_out):
    i = pl.program_id(0)
    j = pl.program_id(1)
    n_j = pl.num_programs(1)
    nblk = item_nblk[i]
    row0 = item_row0[i]

    @pl.when(jnp.logical_and(i == 0, j == 0))
    def _():
        acc[0:EXPERT_BLOCK, :] = jnp.zeros((EXPERT_BLOCK, D_MODEL), F32)

        def zero_copy(c):
            dst = pl.ds(pl.multiple_of((used_blocks[0] + c) * EXPERT_BLOCK, EXPERT_BLOCK), EXPERT_BLOCK)
            return pltpu.make_async_copy(acc.at[0:EXPERT_BLOCK, :], out_hbm.at[dst, :], sem_out)

        n_slack = N_SORTED_BLOCKS - used_blocks[0]
        pl.loop(0, n_slack)(lambda c: zero_copy(c).start())
        pl.loop(0, n_slack)(lambda c: zero_copy(c).wait())

    def x_copy(c):
        rows = pl.ds(pl.multiple_of(c * EXPERT_BLOCK, EXPERT_BLOCK), EXPERT_BLOCK)
        src = pl.ds(pl.multiple_of(row0 + c * EXPERT_BLOCK, EXPERT_BLOCK), EXPERT_BLOCK)
        return pltpu.make_async_copy(xs_hbm.at[src, :], xbuf.at[rows, :], sem_in)

    def out_copy(c):
        rows = pl.ds(pl.multiple_of(c * EXPERT_BLOCK, EXPERT_BLOCK), EXPERT_BLOCK)
        dst = pl.ds(pl.multiple_of(row0 + c * EXPERT_BLOCK, EXPERT_BLOCK), EXPERT_BLOCK)
        return pltpu.make_async_copy(acc.at[rows, :], out_hbm.at[dst, :], sem_out)

    @pl.when(jnp.logical_and(j == 0, nblk > 0))
    def _():
        pl.loop(0, nblk)(lambda c: x_copy(c).start())
        pl.loop(0, nblk)(lambda c: x_copy(c).wait())

    @pl.when(nblk > 0)
    def _():
        wg_bf[...] = wg_ref[...].astype(BF16)
        wu_bf[...] = wu_ref[...].astype(BF16)
        wd_bf[...] = wd_ref[...].astype(BF16)

        @pl.when(j == 0)
        def _():
            @pl.loop(0, nblk)
            def _(c):
                rows = pl.ds(pl.multiple_of(c * EXPERT_BLOCK, EXPERT_BLOCK), EXPERT_BLOCK)
                acc[rows, :] = jnp.broadcast_to(bd_ref[...], (EXPERT_BLOCK, D_MODEL))

        def process(start, size):
            rows = pl.ds(pl.multiple_of(start, EXPERT_BLOCK), size)
            xb = xbuf[rows, :]
            gt = jnp.dot(xb, wg_bf[...], preferred_element_type=F32) + bg_ref[...]
            up = jnp.dot(xb, wu_bf[...], preferred_element_type=F32) + bu_ref[...]
            gt = jnp.minimum(gt, SWIGLU_LIMIT)
            up = jnp.clip(up, -SWIGLU_LIMIT, SWIGLU_LIMIT)
            act = gt * jax.nn.sigmoid(SWIGLU_ALPHA * gt) * (up + 1.0)
            acc[rows, :] += jnp.dot(act.astype(BF16), wd_bf[...], preferred_element_type=F32)

        quad = 4 * EXPERT_BLOCK

        @pl.loop(0, nblk // 4)
        def _(c):
            process(c * quad, 2 * EXPERT_BLOCK)
            process(c * quad + 2 * EXPERT_BLOCK, 2 * EXPERT_BLOCK)

        tail = (nblk // 4) * quad

        @pl.when(nblk % 4 >= 2)
        def _():
            process(tail, 2 * EXPERT_BLOCK)

        @pl.when(nblk % 2 == 1)
        def _():
            process((nblk - 1) * EXPERT_BLOCK, EXPERT_BLOCK)

    @pl.when(jnp.logical_and(j == n_j - 1, nblk > 0))
    def _():
        pl.loop(0, nblk)(lambda c: out_copy(c).start())
        pl.loop(0, nblk)(lambda c: out_copy(c).wait())


def _experts(item_e, item_row0, item_nblk, used_blocks, xs, w_gate, w_up, w_down, b_gate, b_up, b_down):
    n_j = D_FF // FF_TILE

    def jj(i, j, nblk):
        return jnp.where(nblk[i] > 0, j, n_j - 1)

    grid_spec = pltpu.PrefetchScalarGridSpec(
        num_scalar_prefetch=4,
        grid=(N_WORK_ITEMS, n_j),
        in_specs=[
            pl.BlockSpec(memory_space=pl.ANY),
            pl.BlockSpec((None, None, D_MODEL, FF_TILE), lambda i, j, e, r, n, u: (0, e[i], 0, jj(i, j, n))),
            pl.BlockSpec((None, None, D_MODEL, FF_TILE), lambda i, j, e, r, n, u: (0, e[i], 0, jj(i, j, n))),
            pl.BlockSpec((None, None, FF_TILE, D_MODEL), lambda i, j, e, r, n, u: (0, e[i], jj(i, j, n), 0)),
            pl.BlockSpec((None, 1, FF_TILE), lambda i, j, e, r, n, u: (e[i], 0, jj(i, j, n))),
            pl.BlockSpec((None, 1, FF_TILE), lambda i, j, e, r, n, u: (e[i], 0, jj(i, j, n))),
            pl.BlockSpec((None, 1, D_MODEL), lambda i, j, e, r, n, u: (e[i], 0, 0)),
        ],
        out_specs=pl.BlockSpec(memory_space=pl.ANY),
        scratch_shapes=[pltpu.VMEM((EXPERT_CAP, D_MODEL), BF16),
                        pltpu.VMEM((EXPERT_CAP, D_MODEL), F32),
                        pltpu.VMEM((D_MODEL, FF_TILE), BF16),
                        pltpu.VMEM((D_MODEL, FF_TILE), BF16),
                        pltpu.VMEM((FF_TILE, D_MODEL), BF16),
                        pltpu.SemaphoreType.DMA(()),
                        pltpu.SemaphoreType.DMA(())],
    )
    return pl.pallas_call(
        _experts_kernel,
        out_shape=jax.ShapeDtypeStruct((N_SORTED_ROWS, D_MODEL), F32),
        grid_spec=grid_spec,
        compiler_params=_params(("arbitrary", "arbitrary")),
        name="experts",
    )(item_e, item_row0, item_nblk, used_blocks, xs, w_gate, w_up, w_down, b_gate, b_up, b_down)


def _combine_kernel(x_ref, gate_ref, o0, o1, o2, o3, y_ref):
    gates = gate_ref[...]
    y = x_ref[...]
    for k, o_ref in enumerate((o0, o1, o2, o3)):
        y = y + gates[:, k:k + 1] * o_ref[...]
    y_ref[...] = y


def _combine(x2, gates, outs):
    row = pl.BlockSpec((COMBINE_TILE, D_MODEL), lambda i: (i, 0))
    return pl.pallas_call(
        _combine_kernel,
        out_shape=jax.ShapeDtypeStruct((T_ALL, D_MODEL), F32),
        grid=(T_ALL // COMBINE_TILE,),
        in_specs=[row, pl.BlockSpec((COMBINE_TILE, LANES), lambda i: (i, 0)), row, row, row, row],
        out_specs=row,
        compiler_params=_params(("parallel",)),
        name="combine",
    )(x2, gates, *outs)


def _rope_tables():
    half = ROT_DIM // 2
    inv = ROPE_THETA ** (-2.0 * jnp.arange(half, dtype=F32) / ROT_DIM)
    pos = jnp.concatenate([jnp.arange(SEQ), jnp.full((DEC_BATCH,), PAST_LEN)]).astype(F32)
    ang = pos[:, None] * inv[None, :]
    cos, sin = jnp.cos(ang), jnp.sin(ang)
    rest = HEAD_DIM - ROT_DIM
    cos_t = jnp.concatenate([cos, cos, jnp.ones((T_ALL, rest), F32)], axis=-1)
    sin_t = jnp.concatenate([-sin, sin, jnp.zeros((T_ALL, rest), F32)], axis=-1)
    return cos_t, sin_t


def _dispatch_plan(sel, ranks, counts):
    nb = (counts + EXPERT_BLOCK - 1) // EXPERT_BLOCK
    padded = nb * EXPERT_BLOCK
    pad_start = jnp.cumsum(padded) - padded
    dest = pad_start[sel] + ranks
    n_items = (nb + EXPERT_CAP_BLOCKS - 1) // EXPERT_CAP_BLOCKS
    item_end = jnp.cumsum(n_items)
    item_start = item_end - n_items
    w = jnp.arange(N_WORK_ITEMS, dtype=I32)
    total = item_end[-1]
    w_eff = jnp.minimum(w, total - 1)
    e_w = jnp.minimum(jnp.searchsorted(item_end, w_eff, side='right'), N_EXPERTS - 1).astype(I32)
    k_w = w_eff - item_start[e_w]
    row0 = pad_start[e_w] + k_w * EXPERT_CAP
    nblk = jnp.clip(nb[e_w] - k_w * EXPERT_CAP_BLOCKS, 0, EXPERT_CAP_BLOCKS)
    nblk = jnp.where(w < total, nblk, 0)
    used_blocks = jnp.sum(nb).astype(I32).reshape(1)
    return dest, e_w, row0.astype(I32), nblk.astype(I32), used_blocks


def kernel(x_prompt, x_sample, cache_k_w128, cache_v_w128, cache_k_w512, cache_v_w512, cache_k_w2048,
           cache_v_w2048, state_conv, state_rglru, g_mix, w_in, q_norm, k_norm, w_attn_proj, conv_w, conv_b,
           w_rg_a, b_rg_a, w_rg_x, b_rg_x, lru_lambda, w_rnn_proj, w_out, g_ffn, w_router, b_router,
           w_gate, b_gate, w_up, b_up, w_down, b_down):
    caches = (cache_k_w128, cache_v_w128, cache_k_w512, cache_v_w512, cache_k_w2048, cache_v_w2048)
    x = jnp.concatenate([x_prompt[0], x_sample[:, 0]], axis=0)

    xn = _rmsnorm_bf16(x, g_mix)
    cos_t, sin_t = _rope_tables()
    gains = jnp.stack([q_norm, k_norm])
    p = _in_proj(xn, w_in, cos_t, sin_t, gains)

    attn_p = _attn_prompt(p)
    qkv_s = p[SEQ:, :3 * ATTN_WIDTH].reshape(DEC_BATCH, 3, N_HEADS, HEAD_DIM)
    q_s, k_s, v_s = qkv_s[:, 0], qkv_s[:, 1], qkv_s[:, 2]
    views = [c.reshape(DEC_BATCH, ATTN_BLOCK, DILATIONS[n // 2], HEADS_PER_GROUP, HEAD_DIM)
             for n, c in enumerate(caches)]
    attn_s = _attn_sample(q_s, k_s, v_s, views).reshape(DEC_BATCH, ATTN_OUT)
    news = []
    for g in range(N_GROUPS):
        heads = slice(g * HEADS_PER_GROUP, (g + 1) * HEADS_PER_GROUP)
        news += [k_s[:, heads], v_s[:, heads]]
    new_caches = _cache_shift(caches, news)
    attn = jnp.concatenate([attn_p, attn_s], axis=0)

    y_p, h_p = _rglru_prompt(p, conv_w, conv_b, w_rg_a, b_rg_a, w_rg_x, b_rg_x, lru_lambda)
    hist = jnp.transpose(state_conv[0], (1, 0, 2))
    y_s, h_s = _rglru_sample(p, hist, state_rglru[0], conv_w, conv_b, w_rg_a, b_rg_a, w_rg_x, b_rg_x,
                             lru_lambda)
    y = jnp.concatenate([y_p, y_s], axis=0)

    mix = _mix(attn, y, p, w_attn_proj, w_rnn_proj)
    x2 = _out_proj(mix, w_out, x)

    w_router_pad = jnp.pad(w_router[0], ((0, 0), (0, LANES - N_EXPERTS)))
    b_router_pad = jnp.pad(b_router, ((0, 0), (0, LANES - N_EXPERTS)), constant_values=NEG_BIG)
    xn2, sel, gates, ranks, counts = _router(x2, g_ffn, w_router_pad, b_router_pad)
    dest, item_e, item_row0, item_nblk, used_blocks = _dispatch_plan(
        sel[:, :TOP_K], ranks[:, :TOP_K], counts[0, :N_EXPERTS])
    tok = jnp.repeat(jnp.arange(T_ALL, dtype=I32), TOP_K)
    src_tok = jnp.zeros((N_SORTED_ROWS,), I32).at[dest.reshape(-1)].set(tok)
    xs = xn2[src_tok]
    out_sorted = _experts(item_e, item_row0, item_nblk, used_blocks, xs, w_gate, w_up, w_down,
                          b_gate.reshape(N_EXPERTS, 1, D_FF), b_up.reshape(N_EXPERTS, 1, D_FF),
                          b_down.reshape(N_EXPERTS, 1, D_MODEL))
    outs = [out_sorted[dest[:, k]] for k in range(TOP_K)]
    y_all = _combine(x2, gates, outs)

    y_prompt = y_all[:SEQ].reshape(1, SEQ, D_MODEL)
    y_sample = y_all[SEQ:].reshape(DEC_BATCH, 1, D_MODEL)
    states_p = []
    for g, w in enumerate(WINDOWS):
        keep = min(w, SEQ)
        for off in (ATTN_WIDTH, 2 * ATTN_WIDTH):
            c0 = off + g * ATTN_OUT
            states_p.append(p[SEQ - keep:SEQ, c0:c0 + ATTN_OUT].reshape(1, 1, keep, HEADS_PER_GROUP, HEAD_DIM))
    conv_p = p[SEQ - (CONV_W - 1):SEQ, 3 * ATTN_WIDTH:3 * ATTN_WIDTH + D_RNN].reshape(1, 1, CONV_W - 1, D_RNN)
    rglru_p = h_p.reshape(1, 1, D_RNN)
    rnn_x_s = p[SEQ:, 3 * ATTN_WIDTH:3 * ATTN_WIDTH + D_RNN]
    conv_s = jnp.concatenate([state_conv[0][:, 1:], rnn_x_s[:, None, :]], axis=1)[None]
    rglru_s = h_s[None]
    return (y_prompt, y_sample, *states_p, conv_p, rglru_p, *new_caches, conv_s, rglru_s)
```

```python
import functools

import jax
import jax.numpy as jnp
from jax import lax
from jax.experimental import pallas as pl
from jax.experimental.pallas import tpu as pltpu

F32 = jnp.float32
BF16 = jnp.bfloat16
I32 = jnp.int32

D_MODEL = 2048
SEQ = 8192
DEC_BATCH = 128
PAST_LEN = 2048
T_ALL = SEQ + DEC_BATCH

HEAD_DIM = 128
HEADS_PER_GROUP = 4
WINDOWS = (128, 512, 2048)
DILATIONS = (1, 4, 16)
N_GROUPS = 3
N_HEADS = N_GROUPS * HEADS_PER_GROUP
ATTN_WIDTH = N_HEADS * HEAD_DIM
ATTN_OUT = HEADS_PER_GROUP * HEAD_DIM
ATTN_BLOCK = 128
ROT_DIM = HEAD_DIM // 4
ROPE_THETA = 500000.0
D_RNN = D_MODEL
RNN_BLOCK_W = 128
CONV_W = 4
LRU_C = 8.0
N_EXPERTS = 32
TOP_K = 4
D_FF = D_MODEL
SWIGLU_LIMIT = 7.0
SWIGLU_ALPHA = 1.702
EPS = 1e-6
IN_WIDTH = 3 * ATTN_WIDTH + 2 * D_RNN + 2 * D_MODEL

LANES = 128
SUBLANES = 8
VMEM_LIMIT_BYTES = 56 * 1024 * 1024

COL_TILE = 512
COL_RNN_X = (3 * ATTN_WIDTH) // COL_TILE
COL_RNN_GATE = (3 * ATTN_WIDTH + D_RNN) // COL_TILE
COL_GATE_ATTN = (3 * ATTN_WIDTH + 2 * D_RNN) // COL_TILE
COL_GATE_RNN = (3 * ATTN_WIDTH + 2 * D_RNN + D_MODEL) // COL_TILE

ROW_TILE = 1664
MIX_ROW_TILE = 832
NORM_TILE = 640
COMBINE_TILE = 320
SPAN = 2048
NEG_BIG = -1e30

EXPERT_BLOCK = 128
EXPERT_CAP_BLOCKS = 12
EXPERT_CAP = EXPERT_BLOCK * EXPERT_CAP_BLOCKS
FF_TILE = 256
N_SLOTS = T_ALL * TOP_K
N_SORTED_ROWS = N_SLOTS + N_EXPERTS * EXPERT_BLOCK
N_SORTED_BLOCKS = N_SORTED_ROWS // EXPERT_BLOCK
N_WORK_ITEMS = N_EXPERTS + -(-N_SORTED_BLOCKS // EXPERT_CAP_BLOCKS)


def _params(semantics, vmem=VMEM_LIMIT_BYTES):
    return pltpu.CompilerParams(dimension_semantics=semantics, vmem_limit_bytes=vmem)


def _rmsnorm_kernel(x_ref, g_ref, o_ref):
    x = x_ref[...]
    y = x * lax.rsqrt(jnp.mean(x * x, axis=-1, keepdims=True) + EPS)
    o_ref[...] = (y * g_ref[...]).astype(o_ref.dtype)


def _rmsnorm_bf16(x, g):
    t = x.shape[0]
    return pl.pallas_call(
        _rmsnorm_kernel,
        out_shape=jax.ShapeDtypeStruct((t, D_MODEL), BF16),
        grid=(t // NORM_TILE,),
        in_specs=[pl.BlockSpec((NORM_TILE, D_MODEL), lambda i: (i, 0)),
                  pl.BlockSpec((1, D_MODEL), lambda i: (0, 0))],
        out_specs=pl.BlockSpec((NORM_TILE, D_MODEL), lambda i: (i, 0)),
        compiler_params=_params(("parallel",)),
        name="rmsnorm_bf16",
    )(x, g)


def _in_proj_kernel(x_ref, w_ref, cos_ref, sin_ref, gain_ref, o_ref, wbf_ref):
    j = pl.program_id(0)

    @pl.when(pl.program_id(1) == 0)
    def _():
        wbf_ref[...] = w_ref[...].astype(BF16)

    o_ref[...] = jnp.dot(x_ref[...], wbf_ref[...], preferred_element_type=F32)

    @pl.when(j < 2 * ATTN_WIDTH // COL_TILE)
    def _():
        lane = lax.broadcasted_iota(I32, (ROW_TILE, HEAD_DIM), 1)
        first_half = lane < ROT_DIM // 2
        gain = gain_ref[...]
        cos = cos_ref[...]
        sin = sin_ref[...]
        for h in range(COL_TILE // HEAD_DIM):
            cols = slice(h * HEAD_DIM, (h + 1) * HEAD_DIM)
            xh = o_ref[:, cols]
            y = xh * lax.rsqrt(jnp.mean(xh * xh, axis=-1, keepdims=True) + EPS) * gain
            partner = jnp.where(first_half,
                                pltpu.roll(y, HEAD_DIM - ROT_DIM // 2, 1),
                                pltpu.roll(y, ROT_DIM // 2, 1))
            o_ref[:, cols] = y * cos + partner * sin


def _in_proj(xn, w_in, cos_t, sin_t, gains):
    n_tiles = IN_WIDTH // COL_TILE
    qk_tiles = ATTN_WIDTH // COL_TILE
    return pl.pallas_call(
        _in_proj_kernel,
        out_shape=jax.ShapeDtypeStruct((T_ALL, IN_WIDTH), F32),
        grid=(n_tiles, T_ALL // ROW_TILE),
        in_specs=[
            pl.BlockSpec((ROW_TILE, D_MODEL), lambda j, i: (i, 0)),
            pl.BlockSpec((None, D_MODEL, COL_TILE), lambda j, i: (0, 0, j)),
            pl.BlockSpec((ROW_TILE, HEAD_DIM), lambda j, i: (i, 0)),
            pl.BlockSpec((ROW_TILE, HEAD_DIM), lambda j, i: (i, 0)),
            pl.BlockSpec((None, 1, HEAD_DIM), lambda j, i: (jnp.minimum(j // qk_tiles, 1), 0, 0)),
        ],
        out_specs=pl.BlockSpec((ROW_TILE, COL_TILE), lambda j, i: (i, j)),
        scratch_shapes=[pltpu.VMEM((D_MODEL, COL_TILE), BF16)],
        compiler_params=_params(("arbitrary", "arbitrary")),
        name="in_proj",
    )(xn, w_in, cos_t, sin_t, gains)


def _dot_nt(a, b):
    return lax.dot_general(a, b, (((1,), (1,)), ((), ())), preferred_element_type=F32)


def _attn_prompt_kernel(*refs):
    ins = refs[:15]
    o_ref = refs[15]
    og_ref, lse_ref = refs[16], refs[17]
    span_idx = pl.program_id(0)
    qi = lax.broadcasted_iota(I32, (ATTN_BLOCK, ATTN_BLOCK), 0)
    kj = lax.broadcasted_iota(I32, (ATTN_BLOCK, ATTN_BLOCK), 1)
    mask_cur = qi >= kj
    mask_prev_band = kj >= qi
    scale = HEAD_DIM ** -0.5

    for g, dil in enumerate(DILATIONS):
        q_ref, kc_ref, vc_ref, kp_ref, vp_ref = ins[5 * g:5 * g + 5]
        blk = ATTN_BLOCK * dil
        for m in range(SPAN // blk):
            for r in range(dil):
                rows = pl.ds(m * blk + r, ATTN_BLOCK, stride=dil) if dil > 1 else pl.ds(m * blk, ATTN_BLOCK)
                q = (q_ref[rows, :] * scale).astype(BF16)
                k_cur = kc_ref[rows, :].astype(BF16)
                v_cur = vc_ref[rows, :].astype(BF16)
                if m == 0:
                    prow = pl.ds(r, ATTN_BLOCK, stride=dil) if dil > 1 else pl.ds(0, ATTN_BLOCK)
                    k_prev = kp_ref[prow, :].astype(BF16)
                    v_prev = vp_ref[prow, :].astype(BF16)
                    mask_prev = jnp.logical_and(mask_prev_band, span_idx > 0)
                else:
                    prow = (pl.ds((m - 1) * blk + r, ATTN_BLOCK, stride=dil) if dil > 1
                            else pl.ds((m - 1) * blk, ATTN_BLOCK))
                    k_prev = kc_ref[prow, :].astype(BF16)
                    v_prev = vc_ref[prow, :].astype(BF16)
                    mask_prev = mask_prev_band
                s_cur = jnp.where(mask_cur, _dot_nt(q, k_cur), NEG_BIG)
                s_prev = jnp.where(mask_prev, _dot_nt(q, k_prev), NEG_BIG)
                mx = jnp.maximum(jnp.max(s_cur, axis=-1, keepdims=True),
                                 jnp.max(s_prev, axis=-1, keepdims=True))
                p_cur = jnp.exp(s_cur - mx)
                p_prev = jnp.exp(s_prev - mx)
                den = jnp.sum(p_cur, axis=-1, keepdims=True) + jnp.sum(p_prev, axis=-1, keepdims=True)
                pv = (jnp.dot(p_cur.astype(BF16), v_cur, preferred_element_type=F32)
                      + jnp.dot(p_prev.astype(BF16), v_prev, preferred_element_type=F32))
                og_ref[g, rows, :] = pv / den
                lse_ref[g, rows, :] = jnp.broadcast_to(mx + jnp.log(den), (ATTN_BLOCK, HEAD_DIM))

    lse = lse_ref[...]
    top = jnp.max(lse, axis=0)
    w = jnp.exp(lse - top[None])
    o_ref[...] = jnp.sum(w * og_ref[...], axis=0) / jnp.sum(w, axis=0)


def _attn_prompt(p):
    q_cols, k_cols, v_cols = 0, N_HEADS, 2 * N_HEADS
    in_specs = []
    for g, dil in enumerate(DILATIONS):
        blk = ATTN_BLOCK * dil
        per_span = SPAN // blk

        def cur(off, g=g):
            return pl.BlockSpec((SPAN, HEAD_DIM), lambda s, h: (s, off + g * HEADS_PER_GROUP + h))

        def prev(off, g=g, blk=blk, per_span=per_span):
            return pl.BlockSpec((blk, HEAD_DIM),
                                lambda s, h: (jnp.maximum(s * per_span - 1, 0), off + g * HEADS_PER_GROUP + h))

        in_specs += [cur(q_cols), cur(k_cols), cur(v_cols), prev(k_cols), prev(v_cols)]
    return pl.pallas_call(
        _attn_prompt_kernel,
        out_shape=jax.ShapeDtypeStruct((SEQ, ATTN_OUT), F32),
        grid=(SEQ // SPAN, HEADS_PER_GROUP),
        in_specs=in_specs,
        out_specs=pl.BlockSpec((SPAN, HEAD_DIM), lambda s, h: (s, h)),
        scratch_shapes=[pltpu.VMEM((N_GROUPS, SPAN, HEAD_DIM), F32),
                        pltpu.VMEM((N_GROUPS, SPAN, HEAD_DIM), F32)],
        compiler_params=_params(("parallel", "parallel")),
        name="attn_prompt",
    )(*([p] * 15))


SAMPLE_BB = 8


def _attn_sample_kernel(q_ref, k_ref, v_ref, ck0, cv0, ck1, cv1, ck2, cv2, o_ref):
    caches = ((ck0, cv0), (ck1, cv1), (ck2, cv2))
    scale = HEAD_DIM ** -0.5
    for b in range(SAMPLE_BB):
        outs, lses = [], []
        for g in range(N_GROUPS):
            heads = slice(g * HEADS_PER_GROUP, (g + 1) * HEADS_PER_GROUP)
            q = q_ref[b, heads, :] * scale
            k_new = k_ref[b, heads, :]
            v_new = v_ref[b, heads, :]
            k_old = caches[g][0][b]
            v_old = caches[g][1][b]
            s_old = jnp.sum(k_old * q[None], axis=-1, keepdims=True)
            s_new = jnp.sum(k_new * q, axis=-1, keepdims=True)
            mx = jnp.maximum(jnp.max(s_old, axis=0), s_new)
            p_old = jnp.exp(s_old - mx[None])
            p_new = jnp.exp(s_new - mx)
            den = jnp.sum(p_old, axis=0) + p_new
            pv = jnp.sum(p_old * v_old, axis=0) + p_new * v_new
            outs.append(pv / den)
            lses.append(mx + jnp.log(den))
        top = jnp.maximum(jnp.maximum(lses[0], lses[1]), lses[2])
        ws = [jnp.exp(l - top) for l in lses]
        o_ref[b] = (ws[0] * outs[0] + ws[1] * outs[1] + ws[2] * outs[2]) / (ws[0] + ws[1] + ws[2])


def _attn_sample(q_s, k_s, v_s, cache_views):
    row = pl.BlockSpec((SAMPLE_BB, N_HEADS, HEAD_DIM), lambda b: (b, 0, 0))
    cache_spec = pl.BlockSpec((SAMPLE_BB, ATTN_BLOCK, None, HEADS_PER_GROUP, HEAD_DIM),
                              lambda b: (b, 0, 0, 0, 0))
    return pl.pallas_call(
        _attn_sample_kernel,
        out_shape=jax.ShapeDtypeStruct((DEC_BATCH, HEADS_PER_GROUP, HEAD_DIM), F32),
        grid=(DEC_BATCH // SAMPLE_BB,),
        in_specs=[row, row, row] + [cache_spec] * 6,
        out_specs=pl.BlockSpec((SAMPLE_BB, HEADS_PER_GROUP, HEAD_DIM), lambda b: (b, 0, 0)),
        compiler_params=_params(("parallel",)),
        name="attn_sample",
    )(q_s, k_s, v_s, *cache_views)


SHIFT_CHUNK_ROWS = 1024
SHIFT_CHUNKS_PER_STEP = 2
SHIFT_SLOTS = 2 * SHIFT_CHUNKS_PER_STEP


def _shift_steps(window):
    return DEC_BATCH * window // (SHIFT_CHUNK_ROWS * SHIFT_CHUNKS_PER_STEP)


def _shift_schedule():
    spans, lo = [], 0
    for w in WINDOWS:
        for _ in range(2):
            spans.append((lo, lo + _shift_steps(w)))
            lo += _shift_steps(w)
    return spans


SHIFT_TOTAL_STEPS = _shift_schedule()[-1][1]


def _shift_chunk_copies(cache, out, sbuf, sem_in, sem_out, local_step, p, slot):
    w = cache.shape[2]
    ins, outs, new_rows = [], [], []
    copy = functools.partial(functools.partial, pltpu.make_async_copy)
    if w <= SHIFT_CHUNK_ROWS:
        per_chunk = SHIFT_CHUNK_ROWS // w
        b0 = (local_step * SHIFT_CHUNKS_PER_STEP + p) * per_chunk
        for bl in range(per_chunk):
            ins.append(copy(cache.at[0, b0 + bl, pl.ds(1, w - 1)], sbuf.at[slot, pl.ds(bl * w, w - 1)],
                            sem_in.at[slot]))
            outs.append(copy(sbuf.at[slot, pl.ds(bl * w, w)], out.at[0, b0 + bl], sem_out.at[slot]))
            new_rows.append((bl * w + w - 1, b0 + bl))
    else:
        assert w == SHIFT_CHUNK_ROWS * SHIFT_CHUNKS_PER_STEP
        last = p == SHIFT_CHUNKS_PER_STEP - 1
        n_in = SHIFT_CHUNK_ROWS - 1 if last else SHIFT_CHUNK_ROWS
        ins.append(copy(cache.at[0, local_step, pl.ds(1 + p * SHIFT_CHUNK_ROWS, n_in)],
                        sbuf.at[slot, pl.ds(0, n_in)], sem_in.at[slot]))
        outs.append(copy(sbuf.at[slot], out.at[0, local_step, pl.ds(p * SHIFT_CHUNK_ROWS, SHIFT_CHUNK_ROWS)],
                         sem_out.at[slot]))
        if last:
            new_rows.append((SHIFT_CHUNK_ROWS - 1, local_step))
    return ins, outs, new_rows


def _shift_step(step, caches, news, outs, sbuf, sem_in, sem_out):
    spans = _shift_schedule()

    def for_step(t, fn):
        for a, (lo, hi) in enumerate(spans):
            @pl.when(jnp.logical_and(t >= lo, t < hi))
            def _(a=a, lo=lo):
                for p in range(SHIFT_CHUNKS_PER_STEP):
                    slot = (t % 2) * SHIFT_CHUNKS_PER_STEP + p
                    ins, outs_, new_rows = _shift_chunk_copies(caches[a], outs[a], sbuf, sem_in, sem_out,
                                                               t - lo, p, slot)
                    fn(a, slot, ins, outs_, new_rows)

    def drain(a, slot, ins, outs_, new_rows):
        for make in outs_:
            make().wait()

    def prefetch(a, slot, ins, outs_, new_rows):
        for make in ins:
            make().start()

    def forward(a, slot, ins, outs_, new_rows):
        for make in ins:
            make().wait()
        for row, b in new_rows:
            sbuf[slot, row] = news[a][b]
        for make in outs_:
            make().start()

    @pl.when(step == 0)
    def _():
        for_step(step, prefetch)

    for_step(step - 1, drain)
    for_step(step + 1, prefetch)
    for_step(step, forward)


def _rglru_gates(xc, wa_ref, ba_ref, wx_ref, bx_ref, lam_ref):
    r_parts, i_parts = [], []
    for n in range(COL_TILE // RNN_BLOCK_W):
        xb = xc[:, n * RNN_BLOCK_W:(n + 1) * RNN_BLOCK_W].astype(BF16)
        r_parts.append(jnp.dot(xb, wa_ref[n].astype(BF16), preferred_element_type=F32))
        i_parts.append(jnp.dot(xb, wx_ref[n].astype(BF16), preferred_element_type=F32))
    r = jax.nn.sigmoid(jnp.concatenate(r_parts, axis=-1) + ba_ref[...])
    i = jax.nn.sigmoid(jnp.concatenate(i_parts, axis=-1) + bx_ref[...])
    neg_lam = -lam_ref[...]
    softplus = jnp.maximum(neg_lam, 0.0) + jnp.log1p(jnp.exp(-jnp.abs(neg_lam)))
    log_a = -LRU_C * r * softplus
    a = jnp.exp(log_a)
    b = jnp.sqrt(-jnp.tanh(log_a) * (jnp.exp(2.0 * log_a) + 1.0)) * i * xc
    return a, b


RNN_T_TILE = 512
CONV_PAD = SUBLANES


def _rglru_prompt_kernel(x_ref, gate_ref, cw_ref, cb_ref, wa_ref, ba_ref, wx_ref, bx_ref, lam_ref,
                         y_ref, hlast_ref, xbuf, a_scr, b_scr, h_scr, carry):
    t = pl.program_id(1)

    @pl.when(t == 0)
    def _():
        xbuf[0:CONV_PAD, :] = jnp.zeros((CONV_PAD, COL_TILE), F32)
        carry[...] = jnp.zeros((SUBLANES, COL_TILE), F32)

    @pl.when(t > 0)
    def _():
        xbuf[0:CONV_PAD, :] = xbuf[RNN_T_TILE:RNN_T_TILE + CONV_PAD, :]

    xbuf[CONV_PAD:CONV_PAD + RNN_T_TILE, :] = x_ref[...]
    xc = cb_ref[...] + sum(
        xbuf[pl.ds(CONV_PAD - (CONV_W - 1) + j, RNN_T_TILE), :] * cw_ref[j:j + 1, :] for j in range(CONV_W))
    a, b = _rglru_gates(xc, wa_ref, ba_ref, wx_ref, bx_ref, lam_ref)
    a_scr[...] = a
    b_scr[...] = b

    row = lax.broadcasted_iota(I32, (SUBLANES, COL_TILE), 0)

    def chunk(c, h):
        rows = pl.ds(pl.multiple_of(c * SUBLANES, SUBLANES), SUBLANES)
        ac = a_scr[rows, :]
        bc = b_scr[rows, :]
        for s in (1, 2, 4):
            a_sh = jnp.where(row >= s, pltpu.roll(ac, s, 0), 1.0)
            b_sh = jnp.where(row >= s, pltpu.roll(bc, s, 0), 0.0)
            bc = ac * b_sh + bc
            ac = ac * a_sh
        hh = ac * h + bc
        h_scr[rows, :] = hh
        return jnp.broadcast_to(hh[SUBLANES - 1:SUBLANES, :], (SUBLANES, COL_TILE))

    h_end = lax.fori_loop(0, RNN_T_TILE // SUBLANES, chunk, carry[...], unroll=4)
    carry[...] = h_end
    hlast_ref[...] = h_end[0:1, :]
    y_ref[...] = (h_scr[...] * jax.nn.gelu(gate_ref[...])).astype(y_ref.dtype)


def _rnn_param_specs(idx):
    return [
        pl.BlockSpec((None, CONV_W, COL_TILE), lambda *g: (0, 0, idx(*g))),
        pl.BlockSpec((1, COL_TILE), lambda *g: (0, idx(*g))),
        pl.BlockSpec((None, COL_TILE // RNN_BLOCK_W, RNN_BLOCK_W, RNN_BLOCK_W), lambda *g: (0, idx(*g), 0, 0)),
        pl.BlockSpec((1, COL_TILE), lambda *g: (0, idx(*g))),
        pl.BlockSpec((None, COL_TILE // RNN_BLOCK_W, RNN_BLOCK_W, RNN_BLOCK_W), lambda *g: (0, idx(*g), 0, 0)),
        pl.BlockSpec((1, COL_TILE), lambda *g: (0, idx(*g))),
        pl.BlockSpec((1, COL_TILE), lambda *g: (0, idx(*g))),
    ]


def _rglru_prompt(p, conv_w, conv_b, w_rg_a, b_rg_a, w_rg_x, b_rg_x, lru_lambda):
    n_c = D_RNN // COL_TILE
    return pl.pallas_call(
        _rglru_prompt_kernel,
        out_shape=[jax.ShapeDtypeStruct((SEQ, D_RNN), BF16), jax.ShapeDtypeStruct((1, D_RNN), F32)],
        grid=(n_c, SEQ // RNN_T_TILE),
        in_specs=[pl.BlockSpec((RNN_T_TILE, COL_TILE), lambda c, t: (t, COL_RNN_X + c)),
                  pl.BlockSpec((RNN_T_TILE, COL_TILE), lambda c, t: (t, COL_RNN_GATE + c))]
                 + _rnn_param_specs(lambda c, t: c),
        out_specs=[pl.BlockSpec((RNN_T_TILE, COL_TILE), lambda c, t: (t, c)),
                   pl.BlockSpec((1, COL_TILE), lambda c, t: (0, c))],
        scratch_shapes=[pltpu.VMEM((CONV_PAD + RNN_T_TILE, COL_TILE), F32),
                        pltpu.VMEM((RNN_T_TILE, COL_TILE), F32),
                        pltpu.VMEM((RNN_T_TILE, COL_TILE), F32),
                        pltpu.VMEM((RNN_T_TILE, COL_TILE), F32),
                        pltpu.VMEM((SUBLANES, COL_TILE), F32)],
        compiler_params=_params(("parallel", "arbitrary")),
        name="rglru_prompt",
    )(p, p, conv_w, conv_b, w_rg_a, b_rg_a, w_rg_x, b_rg_x, lru_lambda)


def _rglru_sample_kernel(x_ref, gate_ref, hist_ref, h0_ref, cw_ref, cb_ref, wa_ref, ba_ref, wx_ref, bx_ref,
                         lam_ref, y_ref, h_ref):
    xc = cb_ref[...] + x_ref[...] * cw_ref[CONV_W - 1:CONV_W, :]
    for j in range(CONV_W - 1):
        xc = xc + hist_ref[j] * cw_ref[j:j + 1, :]
    a, b = _rglru_gates(xc, wa_ref, ba_ref, wx_ref, bx_ref, lam_ref)
    h = a * h0_ref[...] + b
    h_ref[...] = h
    y_ref[...] = (h * jax.nn.gelu(gate_ref[...])).astype(y_ref.dtype)


def _rglru_sample(p, hist, h0, conv_w, conv_b, w_rg_a, b_rg_a, w_rg_x, b_rg_x, lru_lambda):
    n_c = D_RNN // COL_TILE
    row_blk = SEQ // DEC_BATCH
    return pl.pallas_call(
        _rglru_sample_kernel,
        out_shape=[jax.ShapeDtypeStruct((DEC_BATCH, D_RNN), BF16), jax.ShapeDtypeStruct((DEC_BATCH, D_RNN), F32)],
        grid=(n_c,),
        in_specs=[pl.BlockSpec((DEC_BATCH, COL_TILE), lambda c: (row_blk, COL_RNN_X + c)),
                  pl.BlockSpec((DEC_BATCH, COL_TILE), lambda c: (row_blk, COL_RNN_GATE + c)),
                  pl.BlockSpec((CONV_W - 1, DEC_BATCH, COL_TILE), lambda c: (0, 0, c)),
                  pl.BlockSpec((DEC_BATCH, COL_TILE), lambda c: (0, c))]
                 + _rnn_param_specs(lambda c: c),
        out_specs=[pl.BlockSpec((DEC_BATCH, COL_TILE), lambda c: (0, c)),
                   pl.BlockSpec((DEC_BATCH, COL_TILE), lambda c: (0, c))],
        compiler_params=_params(("parallel",)),
        name="rglru_sample",
    )(p, p, hist, h0, conv_w, conv_b, w_rg_a, b_rg_a, w_rg_x, b_rg_x, lru_lambda)


def _mix_kernel(attn_ref, y_ref, ga_ref, gr_ref, wap_ref, wrp_ref, o_ref, wap_bf, wrp_bf):
    @pl.when(pl.program_id(1) == 0)
    def _():
        wap_bf[...] = wap_ref[...].astype(BF16)
        wrp_bf[...] = wrp_ref[...].astype(BF16)

    attn_d = jnp.dot(attn_ref[...].astype(BF16), wap_bf[...], preferred_element_type=F32)
    rnn_d = jnp.dot(y_ref[...], wrp_bf[...], preferred_element_type=F32)
    mix = jax.nn.sigmoid(ga_ref[...]) * attn_d + jax.nn.sigmoid(gr_ref[...]) * rnn_d
    o_ref[...] = mix.astype(o_ref.dtype)


def _mix(attn, y, p, w_attn_proj, w_rnn_proj):
    return pl.pallas_call(
        _mix_kernel,
        out_shape=jax.ShapeDtypeStruct((T_ALL, D_MODEL), BF16),
        grid=(D_MODEL // COL_TILE, T_ALL // MIX_ROW_TILE),
        in_specs=[pl.BlockSpec((MIX_ROW_TILE, ATTN_OUT), lambda c, i: (i, 0)),
                  pl.BlockSpec((MIX_ROW_TILE, D_RNN), lambda c, i: (i, 0)),
                  pl.BlockSpec((MIX_ROW_TILE, COL_TILE), lambda c, i: (i, COL_GATE_ATTN + c)),
                  pl.BlockSpec((MIX_ROW_TILE, COL_TILE), lambda c, i: (i, COL_GATE_RNN + c)),
                  pl.BlockSpec((None, ATTN_OUT, COL_TILE), lambda c, i: (0, 0, c)),
                  pl.BlockSpec((None, D_RNN, COL_TILE), lambda c, i: (0, 0, c))],
        out_specs=pl.BlockSpec((MIX_ROW_TILE, COL_TILE), lambda c, i: (i, c)),
        scratch_shapes=[pltpu.VMEM((ATTN_OUT, COL_TILE), BF16), pltpu.VMEM((D_RNN, COL_TILE), BF16)],
        compiler_params=_params(("arbitrary", "arbitrary")),
        name="mix",
    )(attn, y, p, p, w_attn_proj, w_rnn_proj)


def _out_proj_kernel(mix_ref, w_ref, x_ref, o_ref, wbf):
    @pl.when(pl.program_id(1) == 0)
    def _():
        wbf[...] = w_ref[...].astype(BF16)

    o_ref[...] = x_ref[...] + jnp.dot(mix_ref[...], wbf[...], preferred_element_type=F32)


def _out_proj(mix, w_out, x):
    return pl.pallas_call(
        _out_proj_kernel,
        out_shape=jax.ShapeDtypeStruct((T_ALL, D_MODEL), F32),
        grid=(D_MODEL // COL_TILE, T_ALL // ROW_TILE),
        in_specs=[pl.BlockSpec((ROW_TILE, D_MODEL), lambda n, i: (i, 0)),
                  pl.BlockSpec((None, D_MODEL, COL_TILE), lambda n, i: (0, 0, n)),
                  pl.BlockSpec((ROW_TILE, COL_TILE), lambda n, i: (i, n))],
        out_specs=pl.BlockSpec((ROW_TILE, COL_TILE), lambda n, i: (i, n)),
        scratch_shapes=[pltpu.VMEM((D_MODEL, COL_TILE), BF16)],
        compiler_params=_params(("arbitrary", "arbitrary")),
        name="out_proj",
    )(mix, w_out, x)


def _split_bf16(x):
    hi = x.astype(BF16)
    lo = (x - hi.astype(F32)).astype(BF16)
    return hi, lo


def _router_kernel(x_ref, g_ref, w_ref, b_ref, xn_ref, sel_ref, gate_ref, rank_ref, cnt_ref, carry, tri):
    step = pl.program_id(0)

    @pl.when(step == 0)
    def _():
        carry[...] = jnp.zeros((1, LANES), F32)
        ri = lax.broadcasted_iota(I32, (NORM_TILE, NORM_TILE), 0)
        ci = lax.broadcasted_iota(I32, (NORM_TILE, NORM_TILE), 1)
        tri[...] = jnp.where(ci < ri, 1.0, 0.0).astype(BF16)

    x = x_ref[...]
    xn = x * lax.rsqrt(jnp.mean(x * x, axis=-1, keepdims=True) + EPS) * g_ref[...]
    xn_ref[...] = xn.astype(BF16)

    x_hi, x_lo = _split_bf16(xn)
    w_hi, w_lo = _split_bf16(w_ref[...])
    logits = (jnp.dot(x_hi, w_hi, preferred_element_type=F32)
              + jnp.dot(x_hi, w_lo, preferred_element_type=F32)
              + jnp.dot(x_lo, w_hi, preferred_element_type=F32)) + b_ref[...]

    lane = lax.broadcasted_iota(I32, (NORM_TILE, LANES), 1)
    work = logits
    vals, idxs = [], []
    for _ in range(TOP_K):
        mk = jnp.max(work, axis=-1, keepdims=True)
        ik = jnp.min(jnp.where(work == mk, lane, LANES), axis=-1, keepdims=True)
        vals.append(mk)
        idxs.append(ik)
        work = jnp.where(lane == ik, -jnp.inf, work)
    exps = [jnp.exp(v - vals[0]) for v in vals]
    den = exps[0] + exps[1] + exps[2] + exps[3]

    member = jnp.zeros((NORM_TILE, LANES), F32)
    for ik in idxs:
        member = member + jnp.where(lane == ik, 1.0, 0.0)
    before = jnp.dot(tri[...], member.astype(BF16), preferred_element_type=F32) + carry[...]
    sel = jnp.zeros((NORM_TILE, LANES), I32)
    gates = jnp.zeros((NORM_TILE, LANES), F32)
    ranks = jnp.zeros((NORM_TILE, LANES), F32)
    for k in range(TOP_K):
        rk = jnp.sum(jnp.where(lane == idxs[k], before, 0.0), axis=-1, keepdims=True)
        sel = jnp.where(lane == k, idxs[k], sel)
        gates = jnp.where(lane == k, exps[k] / den, gates)
        ranks = jnp.where(lane == k, rk, ranks)
    sel_ref[...] = sel
    gate_ref[...] = gates
    rank_ref[...] = ranks.astype(I32)
    carry[...] = carry[...] + jnp.sum(member, axis=0, keepdims=True)
    cnt_ref[...] = carry[...].astype(I32)


def _router(x2, g_ffn, w_router_pad, b_router_pad):
    tile_spec = pl.BlockSpec((NORM_TILE, LANES), lambda i: (i, 0))
    return pl.pallas_call(
        _router_kernel,
        out_shape=[jax.ShapeDtypeStruct((T_ALL, D_MODEL), BF16),
                   jax.ShapeDtypeStruct((T_ALL, LANES), I32),
                   jax.ShapeDtypeStruct((T_ALL, LANES), F32),
                   jax.ShapeDtypeStruct((T_ALL, LANES), I32),
                   jax.ShapeDtypeStruct((1, LANES), I32)],
        grid=(T_ALL // NORM_TILE,),
        in_specs=[pl.BlockSpec((NORM_TILE, D_MODEL), lambda i: (i, 0)),
                  pl.BlockSpec((1, D_MODEL), lambda i: (0, 0)),
                  pl.BlockSpec((D_MODEL, LANES), lambda i: (0, 0)),
                  pl.BlockSpec((1, LANES), lambda i: (0, 0))],
        out_specs=[pl.BlockSpec((NORM_TILE, D_MODEL), lambda i: (i, 0)),
                   tile_spec, tile_spec, tile_spec,
                   pl.BlockSpec((1, LANES), lambda i: (0, 0))],
        scratch_shapes=[pltpu.VMEM((1, LANES), F32), pltpu.VMEM((NORM_TILE, NORM_TILE), BF16)],
        compiler_params=_params(("arbitrary",)),
        name="router",
    )(x2, g_ffn, w_router_pad, b_router_pad)


def _experts_kernel(item_e, item_row0, item_nblk, used_blocks, xs_hbm, wg_ref, wu_ref, wd_ref, bg_ref, bu_ref,
                    bd_ref, *rest):
    caches, news = rest[0:6], rest[6:12]
    out_hbm, cache_outs = rest[12], rest[13:19]
    xbuf, acc, wg_bf, wu_bf, wd_bf, sem_in, sem_out, sbuf, sem_shift_in, sem_shift_out = rest[19:]
    i = pl.program_id(0)
    j = pl.program_id(1)
    n_j = pl.num_programs(1)
    nblk = item_nblk[i]

    _shift_step(i * n_j + j, caches, news, cache_outs, sbuf, sem_shift_in, sem_shift_out)

    @pl.when(jnp.logical_and(i == 0, j == 0))
    def _():
        acc[0:EXPERT_BLOCK, :] = jnp.zeros((EXPERT_BLOCK, D_MODEL), F32)

        def zero_copy(c):
            dst = pl.ds(pl.multiple_of((used_blocks[0] + c) * EXPERT_BLOCK, EXPERT_BLOCK), EXPERT_BLOCK)
            return pltpu.make_async_copy(acc.at[0:EXPERT_BLOCK, :], out_hbm.at[dst, :], sem_out)

        n_slack = N_SORTED_BLOCKS - used_blocks[0]
        pl.loop(0, n_slack)(lambda c: zero_copy(c).start())
        pl.loop(0, n_slack)(lambda c: zero_copy(c).wait())

    n_items = pl.num_programs(0)
    x_slot = i % 2

    def x_copy(item, c):
        rows = pl.ds(pl.multiple_of(c * EXPERT_BLOCK, EXPERT_BLOCK), EXPERT_BLOCK)
        src = pl.ds(pl.multiple_of(item_row0[item] + c * EXPERT_BLOCK, EXPERT_BLOCK), EXPERT_BLOCK)
        return pltpu.make_async_copy(xs_hbm.at[src, :], xbuf.at[item % 2, rows, :], sem_in.at[item % 2])

    def out_copy(item, c):
        rows = pl.ds(pl.multiple_of(c * EXPERT_BLOCK, EXPERT_BLOCK), EXPERT_BLOCK)
        dst = pl.ds(pl.multiple_of(item_row0[item] + c * EXPERT_BLOCK, EXPERT_BLOCK), EXPERT_BLOCK)
        return pltpu.make_async_copy(acc.at[rows, :], out_hbm.at[dst, :], sem_out)

    @pl.when(j == 0)
    def _():
        @pl.when(i == 0)
        def _():
            pl.loop(0, nblk)(lambda c: x_copy(i, c).start())

        pl.loop(0, nblk)(lambda c: x_copy(i, c).wait())
        nxt = jnp.minimum(i + 1, n_items - 1)

        @pl.when(i + 1 < n_items)
        def _():
            pl.loop(0, item_nblk[nxt])(lambda c: x_copy(nxt, c).start())

    @pl.when(nblk > 0)
    def _():
        wg_bf[...] = wg_ref[...].astype(BF16)
        wu_bf[...] = wu_ref[...].astype(BF16)
        wd_bf[...] = wd_ref[...].astype(BF16)

    prev = jnp.maximum(i - 1, 0)

    @pl.when(jnp.logical_and(j == 0, i > 0))
    def _():
        pl.loop(0, item_nblk[prev])(lambda c: out_copy(prev, c).wait())

    @pl.when(nblk > 0)
    def _():
        @pl.when(j == 0)
        def _():
            @pl.loop(0, nblk)
            def _(c):
                rows = pl.ds(pl.multiple_of(c * EXPERT_BLOCK, EXPERT_BLOCK), EXPERT_BLOCK)
                acc[rows, :] = jnp.broadcast_to(bd_ref[...], (EXPERT_BLOCK, D_MODEL))

        def process(start, size):
            rows = pl.ds(pl.multiple_of(start, EXPERT_BLOCK), size)
            xb = xbuf[x_slot, rows, :]
            gt = jnp.dot(xb, wg_bf[...], preferred_element_type=F32) + bg_ref[...]
            up = jnp.dot(xb, wu_bf[...], preferred_element_type=F32) + bu_ref[...]
            gt = jnp.minimum(gt, SWIGLU_LIMIT)
            up = jnp.clip(up, -SWIGLU_LIMIT, SWIGLU_LIMIT)
            act = gt * jax.nn.sigmoid(SWIGLU_ALPHA * gt) * (up + 1.0)
            acc[rows, :] += jnp.dot(act.astype(BF16), wd_bf[...], preferred_element_type=F32)

        quad = 4 * EXPERT_BLOCK

        @pl.loop(0, nblk // 4)
        def _(c):
            process(c * quad, 2 * EXPERT_BLOCK)
            process(c * quad + 2 * EXPERT_BLOCK, 2 * EXPERT_BLOCK)

        tail = (nblk // 4) * quad

        @pl.when(nblk % 4 >= 2)
        def _():
            process(tail, 2 * EXPERT_BLOCK)

        @pl.when(nblk % 2 == 1)
        def _():
            process((nblk - 1) * EXPERT_BLOCK, EXPERT_BLOCK)

    @pl.when(j == n_j - 1)
    def _():
        pl.loop(0, nblk)(lambda c: out_copy(i, c).start())

        @pl.when(i == n_items - 1)
        def _():
            pl.loop(0, nblk)(lambda c: out_copy(i, c).wait())


def _experts(item_e, item_row0, item_nblk, used_blocks, xs, w_gate, w_up, w_down, b_gate, b_up, b_down,
             caches, news):
    n_j = D_FF // FF_TILE
    assert N_WORK_ITEMS * n_j > SHIFT_TOTAL_STEPS

    def jj(i, j, nblk):
        return jnp.where(nblk[i] > 0, j, n_j - 1)

    any_spec = pl.BlockSpec(memory_space=pl.ANY)
    grid_spec = pltpu.PrefetchScalarGridSpec(
        num_scalar_prefetch=4,
        grid=(N_WORK_ITEMS, n_j),
        in_specs=[
            pl.BlockSpec(memory_space=pl.ANY),
            pl.BlockSpec((None, None, D_MODEL, FF_TILE), lambda i, j, e, r, n, u: (0, e[i], 0, jj(i, j, n))),
            pl.BlockSpec((None, None, D_MODEL, FF_TILE), lambda i, j, e, r, n, u: (0, e[i], 0, jj(i, j, n))),
            pl.BlockSpec((None, None, FF_TILE, D_MODEL), lambda i, j, e, r, n, u: (0, e[i], jj(i, j, n), 0)),
            pl.BlockSpec((None, 1, FF_TILE), lambda i, j, e, r, n, u: (e[i], 0, jj(i, j, n))),
            pl.BlockSpec((None, 1, FF_TILE), lambda i, j, e, r, n, u: (e[i], 0, jj(i, j, n))),
            pl.BlockSpec((None, 1, D_MODEL), lambda i, j, e, r, n, u: (e[i], 0, 0)),
        ] + [any_spec] * 6 + [pl.BlockSpec(memory_space=pltpu.VMEM)] * 6,
        out_specs=[any_spec] * 7,
        scratch_shapes=[pltpu.VMEM((2, EXPERT_CAP, D_MODEL), BF16),
                        pltpu.VMEM((EXPERT_CAP, D_MODEL), F32),
                        pltpu.VMEM((D_MODEL, FF_TILE), BF16),
                        pltpu.VMEM((D_MODEL, FF_TILE), BF16),
                        pltpu.VMEM((FF_TILE, D_MODEL), BF16),
                        pltpu.SemaphoreType.DMA((2,)),
                        pltpu.SemaphoreType.DMA(()),
                        pltpu.VMEM((SHIFT_SLOTS, SHIFT_CHUNK_ROWS, HEADS_PER_GROUP, HEAD_DIM), F32),
                        pltpu.SemaphoreType.DMA((SHIFT_SLOTS,)),
                        pltpu.SemaphoreType.DMA((SHIFT_SLOTS,))],
    )
    results = pl.pallas_call(
        _experts_kernel,
        out_shape=[jax.ShapeDtypeStruct((N_SORTED_ROWS, D_MODEL), F32)]
                  + [jax.ShapeDtypeStruct(c.shape, c.dtype) for c in caches],
        grid_spec=grid_spec,
        compiler_params=_params(("arbitrary", "arbitrary")),
        name="experts",
    )(item_e, item_row0, item_nblk, used_blocks, xs, w_gate, w_up, w_down, b_gate, b_up, b_down,
      *caches, *news)
    return results[0], results[1:]


def _combine_kernel(x_ref, gate_ref, o0, o1, o2, o3, y_ref):
    gates = gate_ref[...]
    y = x_ref[...]
    for k, o_ref in enumerate((o0, o1, o2, o3)):
        y = y + gates[:, k:k + 1] * o_ref[...]
    y_ref[...] = y


def _combine(x2, gates, outs):
    row = pl.BlockSpec((COMBINE_TILE, D_MODEL), lambda i: (i, 0))
    return pl.pallas_call(
        _combine_kernel,
        out_shape=jax.ShapeDtypeStruct((T_ALL, D_MODEL), F32),
        grid=(T_ALL // COMBINE_TILE,),
        in_specs=[row, pl.BlockSpec((COMBINE_TILE, LANES), lambda i: (i, 0)), row, row, row, row],
        out_specs=row,
        compiler_params=_params(("parallel",)),
        name="combine",
    )(x2, gates, *outs)


def _rope_tables():
    half = ROT_DIM // 2
    inv = ROPE_THETA ** (-2.0 * jnp.arange(half, dtype=F32) / ROT_DIM)
    pos = jnp.concatenate([jnp.arange(SEQ), jnp.full((DEC_BATCH,), PAST_LEN)]).astype(F32)
    ang = pos[:, None] * inv[None, :]
    cos, sin = jnp.cos(ang), jnp.sin(ang)
    rest = HEAD_DIM - ROT_DIM
    cos_t = jnp.concatenate([cos, cos, jnp.ones((T_ALL, rest), F32)], axis=-1)
    sin_t = jnp.concatenate([-sin, sin, jnp.zeros((T_ALL, rest), F32)], axis=-1)
    return cos_t, sin_t


def _dispatch_plan(sel, ranks, counts):
    nb = (counts + EXPERT_BLOCK - 1) // EXPERT_BLOCK
    padded = nb * EXPERT_BLOCK
    pad_start = jnp.cumsum(padded) - padded
    dest = pad_start[sel] + ranks
    n_items = (nb + EXPERT_CAP_BLOCKS - 1) // EXPERT_CAP_BLOCKS
    item_end = jnp.cumsum(n_items)
    item_start = item_end - n_items
    w = jnp.arange(N_WORK_ITEMS, dtype=I32)
    total = item_end[-1]
    w_eff = jnp.minimum(w, total - 1)
    e_w = jnp.minimum(jnp.searchsorted(item_end, w_eff, side='right'), N_EXPERTS - 1).astype(I32)
    k_w = w_eff - item_start[e_w]
    row0 = pad_start[e_w] + k_w * EXPERT_CAP
    nblk = jnp.clip(nb[e_w] - k_w * EXPERT_CAP_BLOCKS, 0, EXPERT_CAP_BLOCKS)
    nblk = jnp.where(w < total, nblk, 0)
    used_blocks = jnp.sum(nb).astype(I32).reshape(1)
    return dest, e_w, row0.astype(I32), nblk.astype(I32), used_blocks


def kernel(x_prompt, x_sample, cache_k_w128, cache_v_w128, cache_k_w512, cache_v_w512, cache_k_w2048,
           cache_v_w2048, state_conv, state_rglru, g_mix, w_in, q_norm, k_norm, w_attn_proj, conv_w, conv_b,
           w_rg_a, b_rg_a, w_rg_x, b_rg_x, lru_lambda, w_rnn_proj, w_out, g_ffn, w_router, b_router,
           w_gate, b_gate, w_up, b_up, w_down, b_down):
    caches = (cache_k_w128, cache_v_w128, cache_k_w512, cache_v_w512, cache_k_w2048, cache_v_w2048)
    x = jnp.concatenate([x_prompt[0], x_sample[:, 0]], axis=0)

    xn = _rmsnorm_bf16(x, g_mix)
    cos_t, sin_t = _rope_tables()
    gains = jnp.stack([q_norm, k_norm])
    p = _in_proj(xn, w_in, cos_t, sin_t, gains)

    attn_p = _attn_prompt(p)
    qkv_s = p[SEQ:, :3 * ATTN_WIDTH].reshape(DEC_BATCH, 3, N_HEADS, HEAD_DIM)
    q_s, k_s, v_s = qkv_s[:, 0], qkv_s[:, 1], qkv_s[:, 2]
    views = [c.reshape(DEC_BATCH, ATTN_BLOCK, DILATIONS[n // 2], HEADS_PER_GROUP, HEAD_DIM)
             for n, c in enumerate(caches)]
    attn_s = _attn_sample(q_s, k_s, v_s, views).reshape(DEC_BATCH, ATTN_OUT)
    news = []
    for g in range(N_GROUPS):
        heads = slice(g * HEADS_PER_GROUP, (g + 1) * HEADS_PER_GROUP)
        news += [k_s[:, heads], v_s[:, heads]]
    attn = jnp.concatenate([attn_p, attn_s], axis=0)

    y_p, h_p = _rglru_prompt(p, conv_w, conv_b, w_rg_a, b_rg_a, w_rg_x, b_rg_x, lru_lambda)
    hist = jnp.transpose(state_conv[0], (1, 0, 2))
    y_s, h_s = _rglru_sample(p, hist, state_rglru[0], conv_w, conv_b, w_rg_a, b_rg_a, w_rg_x, b_rg_x,
                             lru_lambda)
    y = jnp.concatenate([y_p, y_s], axis=0)

    mix = _mix(attn, y, p, w_attn_proj, w_rnn_proj)
    x2 = _out_proj(mix, w_out, x)

    w_router_pad = jnp.pad(w_router[0], ((0, 0), (0, LANES - N_EXPERTS)))
    b_router_pad = jnp.pad(b_router, ((0, 0), (0, LANES - N_EXPERTS)), constant_values=NEG_BIG)
    xn2, sel, gates, ranks, counts = _router(x2, g_ffn, w_router_pad, b_router_pad)
    dest, item_e, item_row0, item_nblk, used_blocks = _dispatch_plan(
        sel[:, :TOP_K], ranks[:, :TOP_K], counts[0, :N_EXPERTS])
    tok = jnp.repeat(jnp.arange(T_ALL, dtype=I32), TOP_K)
    src_tok = jnp.zeros((N_SORTED_ROWS,), I32).at[dest.reshape(-1)].set(tok)
    xs = xn2[src_tok]
    out_sorted, new_caches = _experts(item_e, item_row0, item_nblk, used_blocks, xs, w_gate, w_up, w_down,
                                      b_gate.reshape(N_EXPERTS, 1, D_FF), b_up.reshape(N_EXPERTS, 1, D_FF),
                                      b_down.reshape(N_EXPERTS, 1, D_MODEL), caches, news)
    outs = [out_sorted[dest[:, k]] for k in range(TOP_K)]
    y_all = _combine(x2, gates, outs)

    y_prompt = y_all[:SEQ].reshape(1, SEQ, D_MODEL)
    y_sample = y_all[SEQ:].reshape(DEC_BATCH, 1, D_MODEL)
    states_p = []
    for g, w in enumerate(WINDOWS):
        keep = min(w, SEQ)
        for off in (ATTN_WIDTH, 2 * ATTN_WIDTH):
            c0 = off + g * ATTN_OUT
            states_p.append(p[SEQ - keep:SEQ, c0:c0 + ATTN_OUT].reshape(1, 1, keep, HEADS_PER_GROUP, HEAD_DIM))
    conv_p = p[SEQ - (CONV_W - 1):SEQ, 3 * ATTN_WIDTH:3 * ATTN_WIDTH + D_RNN].reshape(1, 1, CONV_W - 1, D_RNN)
    rglru_p = h_p.reshape(1, 1, D_RNN)
    rnn_x_s = p[SEQ:, 3 * ATTN_WIDTH:3 * ATTN_WIDTH + D_RNN]
    conv_s = jnp.concatenate([state_conv[0][:, 1:], rnn_x_s[:, None, :]], axis=1)[None]
    rglru_s = h_s[None]
    return (y_prompt, y_sample, *states_p, conv_p, rglru_p, *new_caches, conv_s, rglru_s)
```

```python
import functools

import jax
import jax.numpy as jnp
from jax import lax
from jax.experimental import pallas as pl
from jax.experimental.pallas import tpu as pltpu

F32 = jnp.float32
BF16 = jnp.bfloat16
I32 = jnp.int32

D_MODEL = 2048
SEQ = 8192
DEC_BATCH = 128
PAST_LEN = 2048
T_ALL = SEQ + DEC_BATCH

HEAD_DIM = 128
HEADS_PER_GROUP = 4
WINDOWS = (128, 512, 2048)
DILATIONS = (1, 4, 16)
N_GROUPS = 3
N_HEADS = N_GROUPS * HEADS_PER_GROUP
ATTN_WIDTH = N_HEADS * HEAD_DIM
ATTN_OUT = HEADS_PER_GROUP * HEAD_DIM
ATTN_BLOCK = 128
ROT_DIM = HEAD_DIM // 4
ROPE_THETA = 500000.0
D_RNN = D_MODEL
RNN_BLOCK_W = 128
CONV_W = 4
LRU_C = 8.0
N_EXPERTS = 32
TOP_K = 4
D_FF = D_MODEL
SWIGLU_LIMIT = 7.0
SWIGLU_ALPHA = 1.702
EPS = 1e-6
IN_WIDTH = 3 * ATTN_WIDTH + 2 * D_RNN + 2 * D_MODEL

LANES = 128
SUBLANES = 8
VMEM_LIMIT_BYTES = 56 * 1024 * 1024

COL_TILE = 512
COL_RNN_X = (3 * ATTN_WIDTH) // COL_TILE
COL_RNN_GATE = (3 * ATTN_WIDTH + D_RNN) // COL_TILE
COL_GATE_ATTN = (3 * ATTN_WIDTH + 2 * D_RNN) // COL_TILE
COL_GATE_RNN = (3 * ATTN_WIDTH + 2 * D_RNN + D_MODEL) // COL_TILE

ROW_TILE = 1664
MIX_ROW_TILE = 832
NORM_TILE = 640
COMBINE_TILE = 320
SPAN = 2048
NEG_BIG = -1e30

EXPERT_BLOCK = 128
EXPERT_CAP_BLOCKS = 12
EXPERT_CAP = EXPERT_BLOCK * EXPERT_CAP_BLOCKS
FF_TILE = 256
N_SLOTS = T_ALL * TOP_K
N_SORTED_ROWS = N_SLOTS + N_EXPERTS * EXPERT_BLOCK
N_SORTED_BLOCKS = N_SORTED_ROWS // EXPERT_BLOCK
N_WORK_ITEMS = N_EXPERTS + -(-N_SORTED_BLOCKS // EXPERT_CAP_BLOCKS)
XS_PARTS = 4
XS_PART_BLOCKS = N_SORTED_BLOCKS // XS_PARTS
assert XS_PART_BLOCKS * XS_PARTS == N_SORTED_BLOCKS


def _params(semantics, vmem=VMEM_LIMIT_BYTES):
    return pltpu.CompilerParams(dimension_semantics=semantics, vmem_limit_bytes=vmem)


def _rmsnorm_kernel(x_ref, g_ref, o_ref):
    x = x_ref[...]
    y = x * lax.rsqrt(jnp.mean(x * x, axis=-1, keepdims=True) + EPS)
    o_ref[...] = (y * g_ref[...]).astype(o_ref.dtype)


def _rmsnorm_bf16(x, g):
    t = x.shape[0]
    return pl.pallas_call(
        _rmsnorm_kernel,
        out_shape=jax.ShapeDtypeStruct((t, D_MODEL), BF16),
        grid=(t // NORM_TILE,),
        in_specs=[pl.BlockSpec((NORM_TILE, D_MODEL), lambda i: (i, 0)),
                  pl.BlockSpec((1, D_MODEL), lambda i: (0, 0))],
        out_specs=pl.BlockSpec((NORM_TILE, D_MODEL), lambda i: (i, 0)),
        compiler_params=_params(("parallel",)),
        name="rmsnorm_bf16",
    )(x, g)


def _in_proj_kernel(x_ref, w_ref, cos_ref, sin_ref, gain_ref, o_ref, wbf_ref):
    j = pl.program_id(0)

    @pl.when(pl.program_id(1) == 0)
    def _():
        wbf_ref[...] = w_ref[...].astype(BF16)

    o_ref[...] = jnp.dot(x_ref[...], wbf_ref[...], preferred_element_type=F32)

    @pl.when(j < 2 * ATTN_WIDTH // COL_TILE)
    def _():
        lane = lax.broadcasted_iota(I32, (ROW_TILE, HEAD_DIM), 1)
        first_half = lane < ROT_DIM // 2
        gain = gain_ref[...]
        cos = cos_ref[...]
        sin = sin_ref[...]
        for h in range(COL_TILE // HEAD_DIM):
            cols = slice(h * HEAD_DIM, (h + 1) * HEAD_DIM)
            xh = o_ref[:, cols]
            y = xh * lax.rsqrt(jnp.mean(xh * xh, axis=-1, keepdims=True) + EPS) * gain
            partner = jnp.where(first_half,
                                pltpu.roll(y, HEAD_DIM - ROT_DIM // 2, 1),
                                pltpu.roll(y, ROT_DIM // 2, 1))
            o_ref[:, cols] = y * cos + partner * sin


def _in_proj(xn, w_in, cos_t, sin_t, gains):
    n_tiles = IN_WIDTH // COL_TILE
    qk_tiles = ATTN_WIDTH // COL_TILE
    return pl.pallas_call(
        _in_proj_kernel,
        out_shape=jax.ShapeDtypeStruct((T_ALL, IN_WIDTH), F32),
        grid=(n_tiles, T_ALL // ROW_TILE),
        in_specs=[
            pl.BlockSpec((ROW_TILE, D_MODEL), lambda j, i: (i, 0)),
            pl.BlockSpec((None, D_MODEL, COL_TILE), lambda j, i: (0, 0, j)),
            pl.BlockSpec((ROW_TILE, HEAD_DIM), lambda j, i: (i, 0)),
            pl.BlockSpec((ROW_TILE, HEAD_DIM), lambda j, i: (i, 0)),
            pl.BlockSpec((None, 1, HEAD_DIM), lambda j, i: (jnp.minimum(j // qk_tiles, 1), 0, 0)),
        ],
        out_specs=pl.BlockSpec((ROW_TILE, COL_TILE), lambda j, i: (i, j)),
        scratch_shapes=[pltpu.VMEM((D_MODEL, COL_TILE), BF16)],
        compiler_params=_params(("arbitrary", "arbitrary")),
        name="in_proj",
    )(xn, w_in, cos_t, sin_t, gains)


def _dot_nt(a, b):
    return lax.dot_general(a, b, (((1,), (1,)), ((), ())), preferred_element_type=F32)


def _attn_prompt_kernel(*refs):
    ins = refs[:15]
    o_ref = refs[15]
    og_ref, lse_ref = refs[16], refs[17]
    span_idx = pl.program_id(0)
    qi = lax.broadcasted_iota(I32, (ATTN_BLOCK, ATTN_BLOCK), 0)
    kj = lax.broadcasted_iota(I32, (ATTN_BLOCK, ATTN_BLOCK), 1)
    mask_cur = qi >= kj
    mask_prev_band = kj >= qi
    scale = HEAD_DIM ** -0.5

    for g, dil in enumerate(DILATIONS):
        q_ref, kc_ref, vc_ref, kp_ref, vp_ref = ins[5 * g:5 * g + 5]
        blk = ATTN_BLOCK * dil
        for m in range(SPAN // blk):
            for r in range(dil):
                rows = pl.ds(m * blk + r, ATTN_BLOCK, stride=dil) if dil > 1 else pl.ds(m * blk, ATTN_BLOCK)
                q = (q_ref[rows, :] * scale).astype(BF16)
                k_cur = kc_ref[rows, :].astype(BF16)
                v_cur = vc_ref[rows, :].astype(BF16)
                if m == 0:
                    prow = pl.ds(r, ATTN_BLOCK, stride=dil) if dil > 1 else pl.ds(0, ATTN_BLOCK)
                    k_prev = kp_ref[prow, :].astype(BF16)
                    v_prev = vp_ref[prow, :].astype(BF16)
                    mask_prev = jnp.logical_and(mask_prev_band, span_idx > 0)
                else:
                    prow = (pl.ds((m - 1) * blk + r, ATTN_BLOCK, stride=dil) if dil > 1
                            else pl.ds((m - 1) * blk, ATTN_BLOCK))
                    k_prev = kc_ref[prow, :].astype(BF16)
                    v_prev = vc_ref[prow, :].astype(BF16)
                    mask_prev = mask_prev_band
                s_cur = jnp.where(mask_cur, _dot_nt(q, k_cur), NEG_BIG)
                s_prev = jnp.where(mask_prev, _dot_nt(q, k_prev), NEG_BIG)
                mx = jnp.maximum(jnp.max(s_cur, axis=-1, keepdims=True),
                                 jnp.max(s_prev, axis=-1, keepdims=True))
                p_cur = jnp.exp(s_cur - mx)
                p_prev = jnp.exp(s_prev - mx)
                den = jnp.sum(p_cur, axis=-1, keepdims=True) + jnp.sum(p_prev, axis=-1, keepdims=True)
                pv = (jnp.dot(p_cur.astype(BF16), v_cur, preferred_element_type=F32)
                      + jnp.dot(p_prev.astype(BF16), v_prev, preferred_element_type=F32))
                og_ref[g, rows, :] = pv / den
                lse_ref[g, rows, :] = jnp.broadcast_to(mx + jnp.log(den), (ATTN_BLOCK, HEAD_DIM))

    lse = lse_ref[...]
    top = jnp.max(lse, axis=0)
    w = jnp.exp(lse - top[None])
    o_ref[...] = jnp.sum(w * og_ref[...], axis=0) / jnp.sum(w, axis=0)


def _attn_prompt(p):
    q_cols, k_cols, v_cols = 0, N_HEADS, 2 * N_HEADS
    in_specs = []
    for g, dil in enumerate(DILATIONS):
        blk = ATTN_BLOCK * dil
        per_span = SPAN // blk

        def cur(off, g=g):
            return pl.BlockSpec((SPAN, HEAD_DIM), lambda s, h: (s, off + g * HEADS_PER_GROUP + h))

        def prev(off, g=g, blk=blk, per_span=per_span):
            return pl.BlockSpec((blk, HEAD_DIM),
                                lambda s, h: (jnp.maximum(s * per_span - 1, 0), off + g * HEADS_PER_GROUP + h))

        in_specs += [cur(q_cols), cur(k_cols), cur(v_cols), prev(k_cols), prev(v_cols)]
    return pl.pallas_call(
        _attn_prompt_kernel,
        out_shape=jax.ShapeDtypeStruct((SEQ, ATTN_OUT), F32),
        grid=(SEQ // SPAN, HEADS_PER_GROUP),
        in_specs=in_specs,
        out_specs=pl.BlockSpec((SPAN, HEAD_DIM), lambda s, h: (s, h)),
        scratch_shapes=[pltpu.VMEM((N_GROUPS, SPAN, HEAD_DIM), F32),
                        pltpu.VMEM((N_GROUPS, SPAN, HEAD_DIM), F32)],
        compiler_params=_params(("parallel", "parallel")),
        name="attn_prompt",
    )(*([p] * 15))


SAMPLE_BB = 8


def _attn_sample_kernel(q_ref, k_ref, v_ref, ck0, cv0, ck1, cv1, ck2, cv2, o_ref):
    caches = ((ck0, cv0), (ck1, cv1), (ck2, cv2))
    scale = HEAD_DIM ** -0.5
    for b in range(SAMPLE_BB):
        outs, lses = [], []
        for g in range(N_GROUPS):
            heads = slice(g * HEADS_PER_GROUP, (g + 1) * HEADS_PER_GROUP)
            q = q_ref[b, heads, :] * scale
            k_new = k_ref[b, heads, :]
            v_new = v_ref[b, heads, :]
            k_old = caches[g][0][b]
            v_old = caches[g][1][b]
            s_old = jnp.sum(k_old * q[None], axis=-1, keepdims=True)
            s_new = jnp.sum(k_new * q, axis=-1, keepdims=True)
            mx = jnp.maximum(jnp.max(s_old, axis=0), s_new)
            p_old = jnp.exp(s_old - mx[None])
            p_new = jnp.exp(s_new - mx)
            den = jnp.sum(p_old, axis=0) + p_new
            pv = jnp.sum(p_old * v_old, axis=0) + p_new * v_new
            outs.append(pv / den)
            lses.append(mx + jnp.log(den))
        top = jnp.maximum(jnp.maximum(lses[0], lses[1]), lses[2])
        ws = [jnp.exp(l - top) for l in lses]
        o_ref[b] = (ws[0] * outs[0] + ws[1] * outs[1] + ws[2] * outs[2]) / (ws[0] + ws[1] + ws[2])


def _attn_sample(q_s, k_s, v_s, cache_views):
    row = pl.BlockSpec((SAMPLE_BB, N_HEADS, HEAD_DIM), lambda b: (b, 0, 0))
    cache_spec = pl.BlockSpec((SAMPLE_BB, ATTN_BLOCK, None, HEADS_PER_GROUP, HEAD_DIM),
                              lambda b: (b, 0, 0, 0, 0))
    return pl.pallas_call(
        _attn_sample_kernel,
        out_shape=jax.ShapeDtypeStruct((DEC_BATCH, HEADS_PER_GROUP, HEAD_DIM), F32),
        grid=(DEC_BATCH // SAMPLE_BB,),
        in_specs=[row, row, row] + [cache_spec] * 6,
        out_specs=pl.BlockSpec((SAMPLE_BB, HEADS_PER_GROUP, HEAD_DIM), lambda b: (b, 0, 0)),
        compiler_params=_params(("parallel",)),
        name="attn_sample",
    )(q_s, k_s, v_s, *cache_views)


SHIFT_CHUNK_ROWS = 1024
SHIFT_CHUNKS_PER_STEP = 2
SHIFT_SLOTS = 2 * SHIFT_CHUNKS_PER_STEP
SHIFT_DMA_QUEUE = 1


def _shift_steps(window):
    return DEC_BATCH * window // (SHIFT_CHUNK_ROWS * SHIFT_CHUNKS_PER_STEP)


def _shift_schedule():
    spans, lo = [], 0
    for w in WINDOWS:
        for _ in range(2):
            spans.append((lo, lo + _shift_steps(w)))
            lo += _shift_steps(w)
    return spans


SHIFT_TOTAL_STEPS = _shift_schedule()[-1][1]


def _shift_chunk_copies(cache, out, sbuf, sem_in, sem_out, local_step, p, slot):
    w = cache.shape[2]
    ins, outs, new_rows = [], [], []
    copy = functools.partial(functools.partial, pltpu.make_async_copy)
    if w <= SHIFT_CHUNK_ROWS:
        per_chunk = SHIFT_CHUNK_ROWS // w
        b0 = (local_step * SHIFT_CHUNKS_PER_STEP + p) * per_chunk
        for bl in range(per_chunk):
            ins.append(copy(cache.at[0, b0 + bl, pl.ds(1, w - 1)], sbuf.at[slot, pl.ds(bl * w, w - 1)],
                            sem_in.at[slot]))
            outs.append(copy(sbuf.at[slot, pl.ds(bl * w, w)], out.at[0, b0 + bl], sem_out.at[slot]))
            new_rows.append((bl * w + w - 1, b0 + bl))
    else:
        assert w == SHIFT_CHUNK_ROWS * SHIFT_CHUNKS_PER_STEP
        last = p == SHIFT_CHUNKS_PER_STEP - 1
        n_in = SHIFT_CHUNK_ROWS - 1 if last else SHIFT_CHUNK_ROWS
        ins.append(copy(cache.at[0, local_step, pl.ds(1 + p * SHIFT_CHUNK_ROWS, n_in)],
                        sbuf.at[slot, pl.ds(0, n_in)], sem_in.at[slot]))
        outs.append(copy(sbuf.at[slot], out.at[0, local_step, pl.ds(p * SHIFT_CHUNK_ROWS, SHIFT_CHUNK_ROWS)],
                         sem_out.at[slot]))
        if last:
            new_rows.append((SHIFT_CHUNK_ROWS - 1, local_step))
    return ins, outs, new_rows


def _shift_step(step, caches, news, outs, sbuf, sem_in, sem_out):
    spans = _shift_schedule()

    def for_step(t, fn):
        for a, (lo, hi) in enumerate(spans):
            @pl.when(jnp.logical_and(t >= lo, t < hi))
            def _(a=a, lo=lo):
                for p in range(SHIFT_CHUNKS_PER_STEP):
                    slot = (t % 2) * SHIFT_CHUNKS_PER_STEP + p
                    ins, outs_, new_rows = _shift_chunk_copies(caches[a], outs[a], sbuf, sem_in, sem_out,
                                                               t - lo, p, slot)
                    fn(a, slot, ins, outs_, new_rows)

    def drain(a, slot, ins, outs_, new_rows):
        for make in outs_:
            make().wait()

    def prefetch(a, slot, ins, outs_, new_rows):
        for make in ins:
            make().start(priority=SHIFT_DMA_QUEUE)

    def forward(a, slot, ins, outs_, new_rows):
        for make in ins:
            make().wait()
        for row, b in new_rows:
            sbuf[slot, row] = news[a][b]
        for make in outs_:
            make().start(priority=SHIFT_DMA_QUEUE)

    @pl.when(step == 0)
    def _():
        for_step(step, prefetch)

    for_step(step - 1, drain)
    for_step(step + 1, prefetch)
    for_step(step, forward)


def _rglru_gates(xc, wa_ref, ba_ref, wx_ref, bx_ref, lam_ref):
    r_parts, i_parts = [], []
    for n in range(COL_TILE // RNN_BLOCK_W):
        xb = xc[:, n * RNN_BLOCK_W:(n + 1) * RNN_BLOCK_W].astype(BF16)
        r_parts.append(jnp.dot(xb, wa_ref[n].astype(BF16), preferred_element_type=F32))
        i_parts.append(jnp.dot(xb, wx_ref[n].astype(BF16), preferred_element_type=F32))
    r = jax.nn.sigmoid(jnp.concatenate(r_parts, axis=-1) + ba_ref[...])
    i = jax.nn.sigmoid(jnp.concatenate(i_parts, axis=-1) + bx_ref[...])
    neg_lam = -lam_ref[...]
    softplus = jnp.maximum(neg_lam, 0.0) + jnp.log1p(jnp.exp(-jnp.abs(neg_lam)))
    log_a = -LRU_C * r * softplus
    a = jnp.exp(log_a)
    b = jnp.sqrt(-jnp.tanh(log_a) * (jnp.exp(2.0 * log_a) + 1.0)) * i * xc
    return a, b


RNN_T_TILE = 512
CONV_PAD = SUBLANES


def _rglru_prompt_kernel(x_ref, gate_ref, cw_ref, cb_ref, wa_ref, ba_ref, wx_ref, bx_ref, lam_ref,
                         y_ref, hlast_ref, xbuf, a_scr, b_scr, h_scr, carry):
    t = pl.program_id(1)

    @pl.when(t == 0)
    def _():
        xbuf[0:CONV_PAD, :] = jnp.zeros((CONV_PAD, COL_TILE), F32)
        carry[...] = jnp.zeros((SUBLANES, COL_TILE), F32)

    @pl.when(t > 0)
    def _():
        xbuf[0:CONV_PAD, :] = xbuf[RNN_T_TILE:RNN_T_TILE + CONV_PAD, :]

    xbuf[CONV_PAD:CONV_PAD + RNN_T_TILE, :] = x_ref[...]
    xc = cb_ref[...] + sum(
        xbuf[pl.ds(CONV_PAD - (CONV_W - 1) + j, RNN_T_TILE), :] * cw_ref[j:j + 1, :] for j in range(CONV_W))
    a, b = _rglru_gates(xc, wa_ref, ba_ref, wx_ref, bx_ref, lam_ref)
    a_scr[...] = a
    b_scr[...] = b

    row = lax.broadcasted_iota(I32, (SUBLANES, COL_TILE), 0)

    def chunk(c, h):
        rows = pl.ds(pl.multiple_of(c * SUBLANES, SUBLANES), SUBLANES)
        ac = a_scr[rows, :]
        bc = b_scr[rows, :]
        for s in (1, 2, 4):
            a_sh = jnp.where(row >= s, pltpu.roll(ac, s, 0), 1.0)
            b_sh = jnp.where(row >= s, pltpu.roll(bc, s, 0), 0.0)
            bc = ac * b_sh + bc
            ac = ac * a_sh
        hh = ac * h + bc
        h_scr[rows, :] = hh
        return jnp.broadcast_to(hh[SUBLANES - 1:SUBLANES, :], (SUBLANES, COL_TILE))

    h_end = lax.fori_loop(0, RNN_T_TILE // SUBLANES, chunk, carry[...], unroll=4)
    carry[...] = h_end
    hlast_ref[...] = h_end[0:1, :]
    y_ref[...] = (h_scr[...] * jax.nn.gelu(gate_ref[...])).astype(y_ref.dtype)


def _rnn_param_specs(idx):
    return [
        pl.BlockSpec((None, CONV_W, COL_TILE), lambda *g: (0, 0, idx(*g))),
        pl.BlockSpec((1, COL_TILE), lambda *g: (0, idx(*g))),
        pl.BlockSpec((None, COL_TILE // RNN_BLOCK_W, RNN_BLOCK_W, RNN_BLOCK_W), lambda *g: (0, idx(*g), 0, 0)),
        pl.BlockSpec((1, COL_TILE), lambda *g: (0, idx(*g))),
        pl.BlockSpec((None, COL_TILE // RNN_BLOCK_W, RNN_BLOCK_W, RNN_BLOCK_W), lambda *g: (0, idx(*g), 0, 0)),
        pl.BlockSpec((1, COL_TILE), lambda *g: (0, idx(*g))),
        pl.BlockSpec((1, COL_TILE), lambda *g: (0, idx(*g))),
    ]


def _rglru_prompt(p, conv_w, conv_b, w_rg_a, b_rg_a, w_rg_x, b_rg_x, lru_lambda):
    n_c = D_RNN // COL_TILE
    return pl.pallas_call(
        _rglru_prompt_kernel,
        out_shape=[jax.ShapeDtypeStruct((SEQ, D_RNN), BF16), jax.ShapeDtypeStruct((1, D_RNN), F32)],
        grid=(n_c, SEQ // RNN_T_TILE),
        in_specs=[pl.BlockSpec((RNN_T_TILE, COL_TILE), lambda c, t: (t, COL_RNN_X + c)),
                  pl.BlockSpec((RNN_T_TILE, COL_TILE), lambda c, t: (t, COL_RNN_GATE + c))]
                 + _rnn_param_specs(lambda c, t: c),
        out_specs=[pl.BlockSpec((RNN_T_TILE, COL_TILE), lambda c, t: (t, c)),
                   pl.BlockSpec((1, COL_TILE), lambda c, t: (0, c))],
        scratch_shapes=[pltpu.VMEM((CONV_PAD + RNN_T_TILE, COL_TILE), F32),
                        pltpu.VMEM((RNN_T_TILE, COL_TILE), F32),
                        pltpu.VMEM((RNN_T_TILE, COL_TILE), F32),
                        pltpu.VMEM((RNN_T_TILE, COL_TILE), F32),
                        pltpu.VMEM((SUBLANES, COL_TILE), F32)],
        compiler_params=_params(("parallel", "arbitrary")),
        name="rglru_prompt",
    )(p, p, conv_w, conv_b, w_rg_a, b_rg_a, w_rg_x, b_rg_x, lru_lambda)


def _rglru_sample_kernel(x_ref, gate_ref, hist_ref, h0_ref, cw_ref, cb_ref, wa_ref, ba_ref, wx_ref, bx_ref,
                         lam_ref, y_ref, h_ref):
    xc = cb_ref[...] + x_ref[...] * cw_ref[CONV_W - 1:CONV_W, :]
    for j in range(CONV_W - 1):
        xc = xc + hist_ref[j] * cw_ref[j:j + 1, :]
    a, b = _rglru_gates(xc, wa_ref, ba_ref, wx_ref, bx_ref, lam_ref)
    h = a * h0_ref[...] + b
    h_ref[...] = h
    y_ref[...] = (h * jax.nn.gelu(gate_ref[...])).astype(y_ref.dtype)


def _rglru_sample(p, hist, h0, conv_w, conv_b, w_rg_a, b_rg_a, w_rg_x, b_rg_x, lru_lambda):
    n_c = D_RNN // COL_TILE
    row_blk = SEQ // DEC_BATCH
    return pl.pallas_call(
        _rglru_sample_kernel,
        out_shape=[jax.ShapeDtypeStruct((DEC_BATCH, D_RNN), BF16), jax.ShapeDtypeStruct((DEC_BATCH, D_RNN), F32)],
        grid=(n_c,),
        in_specs=[pl.BlockSpec((DEC_BATCH, COL_TILE), lambda c: (row_blk, COL_RNN_X + c)),
                  pl.BlockSpec((DEC_BATCH, COL_TILE), lambda c: (row_blk, COL_RNN_GATE + c)),
                  pl.BlockSpec((CONV_W - 1, DEC_BATCH, COL_TILE), lambda c: (0, 0, c)),
                  pl.BlockSpec((DEC_BATCH, COL_TILE), lambda c: (0, c))]
                 + _rnn_param_specs(lambda c: c),
        out_specs=[pl.BlockSpec((DEC_BATCH, COL_TILE), lambda c: (0, c)),
                   pl.BlockSpec((DEC_BATCH, COL_TILE), lambda c: (0, c))],
        compiler_params=_params(("parallel",)),
        name="rglru_sample",
    )(p, p, hist, h0, conv_w, conv_b, w_rg_a, b_rg_a, w_rg_x, b_rg_x, lru_lambda)


def _mix_kernel(attn_ref, y_ref, ga_ref, gr_ref, wap_ref, wrp_ref, o_ref, wap_bf, wrp_bf):
    @pl.when(pl.program_id(1) == 0)
    def _():
        wap_bf[...] = wap_ref[...].astype(BF16)
        wrp_bf[...] = wrp_ref[...].astype(BF16)

    attn_d = jnp.dot(attn_ref[...].astype(BF16), wap_bf[...], preferred_element_type=F32)
    rnn_d = jnp.dot(y_ref[...], wrp_bf[...], preferred_element_type=F32)
    mix = jax.nn.sigmoid(ga_ref[...]) * attn_d + jax.nn.sigmoid(gr_ref[...]) * rnn_d
    o_ref[...] = mix.astype(o_ref.dtype)


def _mix(attn, y, p, w_attn_proj, w_rnn_proj):
    return pl.pallas_call(
        _mix_kernel,
        out_shape=jax.ShapeDtypeStruct((T_ALL, D_MODEL), BF16),
        grid=(D_MODEL // COL_TILE, T_ALL // MIX_ROW_TILE),
        in_specs=[pl.BlockSpec((MIX_ROW_TILE, ATTN_OUT), lambda c, i: (i, 0)),
                  pl.BlockSpec((MIX_ROW_TILE, D_RNN), lambda c, i: (i, 0)),
                  pl.BlockSpec((MIX_ROW_TILE, COL_TILE), lambda c, i: (i, COL_GATE_ATTN + c)),
                  pl.BlockSpec((MIX_ROW_TILE, COL_TILE), lambda c, i: (i, COL_GATE_RNN + c)),
                  pl.BlockSpec((None, ATTN_OUT, COL_TILE), lambda c, i: (0, 0, c)),
                  pl.BlockSpec((None, D_RNN, COL_TILE), lambda c, i: (0, 0, c))],
        out_specs=pl.BlockSpec((MIX_ROW_TILE, COL_TILE), lambda c, i: (i, c)),
        scratch_shapes=[pltpu.VMEM((ATTN_OUT, COL_TILE), BF16), pltpu.VMEM((D_RNN, COL_TILE), BF16)],
        compiler_params=_params(("arbitrary", "arbitrary")),
        name="mix",
    )(attn, y, p, p, w_attn_proj, w_rnn_proj)


def _out_proj_kernel(mix_ref, w_ref, x_ref, o_ref, wbf):
    @pl.when(pl.program_id(1) == 0)
    def _():
        wbf[...] = w_ref[...].astype(BF16)

    o_ref[...] = x_ref[...] + jnp.dot(mix_ref[...], wbf[...], preferred_element_type=F32)


def _out_proj(mix, w_out, x):
    return pl.pallas_call(
        _out_proj_kernel,
        out_shape=jax.ShapeDtypeStruct((T_ALL, D_MODEL), F32),
        grid=(D_MODEL // COL_TILE, T_ALL // ROW_TILE),
        in_specs=[pl.BlockSpec((ROW_TILE, D_MODEL), lambda n, i: (i, 0)),
                  pl.BlockSpec((None, D_MODEL, COL_TILE), lambda n, i: (0, 0, n)),
                  pl.BlockSpec((ROW_TILE, COL_TILE), lambda n, i: (i, n))],
        out_specs=pl.BlockSpec((ROW_TILE, COL_TILE), lambda n, i: (i, n)),
        scratch_shapes=[pltpu.VMEM((D_MODEL, COL_TILE), BF16)],
        compiler_params=_params(("arbitrary", "arbitrary")),
        name="out_proj",
    )(mix, w_out, x)


HALF_MODEL = D_MODEL // 2
HIGH_HALF_MASK = -65536


def _pack_bf16_pairs(x):
    hi = pltpu.bitcast(x[:, :HALF_MODEL].astype(BF16).astype(F32), I32)
    lo = pltpu.bitcast(x[:, HALF_MODEL:].astype(BF16).astype(F32), I32)
    return pltpu.bitcast(hi | lax.shift_right_logical(lo, 16), F32)


def _unpack_bf16_pairs(words):
    bits = pltpu.bitcast(words, I32)
    first = pltpu.bitcast(bits & HIGH_HALF_MASK, F32).astype(BF16)
    second = pltpu.bitcast(lax.shift_left(bits, 16), F32).astype(BF16)
    return first, second


def _split_bf16(x):
    hi = x.astype(BF16)
    lo = (x - hi.astype(F32)).astype(BF16)
    return hi, lo


def _router_kernel(x_ref, g_ref, w_ref, b_ref, xn_ref, sel_ref, gate_ref, rank_ref, cnt_ref, carry, tri):
    step = pl.program_id(0)

    @pl.when(step == 0)
    def _():
        carry[...] = jnp.zeros((1, LANES), F32)
        ri = lax.broadcasted_iota(I32, (NORM_TILE, NORM_TILE), 0)
        ci = lax.broadcasted_iota(I32, (NORM_TILE, NORM_TILE), 1)
        tri[...] = jnp.where(ci < ri, 1.0, 0.0).astype(BF16)

    x = x_ref[...]
    xn = x * lax.rsqrt(jnp.mean(x * x, axis=-1, keepdims=True) + EPS) * g_ref[...]
    xn_ref[...] = _pack_bf16_pairs(xn)

    x_hi, x_lo = _split_bf16(xn)
    w_hi, w_lo = _split_bf16(w_ref[...])
    logits = (jnp.dot(x_hi, w_hi, preferred_element_type=F32)
              + jnp.dot(x_hi, w_lo, preferred_element_type=F32)
              + jnp.dot(x_lo, w_hi, preferred_element_type=F32)) + b_ref[...]

    lane = lax.broadcasted_iota(I32, (NORM_TILE, LANES), 1)
    work = logits
    vals, idxs = [], []
    for _ in range(TOP_K):
        mk = jnp.max(work, axis=-1, keepdims=True)
        ik = jnp.min(jnp.where(work == mk, lane, LANES), axis=-1, keepdims=True)
        vals.append(mk)
        idxs.append(ik)
        work = jnp.where(lane == ik, -jnp.inf, work)
    exps = [jnp.exp(v - vals[0]) for v in vals]
    den = exps[0] + exps[1] + exps[2] + exps[3]

    member = jnp.zeros((NORM_TILE, LANES), F32)
    for ik in idxs:
        member = member + jnp.where(lane == ik, 1.0, 0.0)
    before = jnp.dot(tri[...], member.astype(BF16), preferred_element_type=F32) + carry[...]
    sel = jnp.zeros((NORM_TILE, LANES), I32)
    gates = jnp.zeros((NORM_TILE, LANES), F32)
    ranks = jnp.zeros((NORM_TILE, LANES), F32)
    for k in range(TOP_K):
        rk = jnp.sum(jnp.where(lane == idxs[k], before, 0.0), axis=-1, keepdims=True)
        sel = jnp.where(lane == k, idxs[k], sel)
        gates = jnp.where(lane == k, exps[k] / den, gates)
        ranks = jnp.where(lane == k, rk, ranks)
    sel_ref[...] = sel
    gate_ref[...] = gates
    rank_ref[...] = ranks.astype(I32)
    carry[...] = carry[...] + jnp.sum(member, axis=0, keepdims=True)
    cnt_ref[...] = carry[...].astype(I32)


def _router(x2, g_ffn, w_router_pad, b_router_pad):
    tile_spec = pl.BlockSpec((NORM_TILE, LANES), lambda i: (i, 0))
    return pl.pallas_call(
        _router_kernel,
        out_shape=[jax.ShapeDtypeStruct((T_ALL, HALF_MODEL), F32),
                   jax.ShapeDtypeStruct((T_ALL, LANES), I32),
                   jax.ShapeDtypeStruct((T_ALL, LANES), F32),
                   jax.ShapeDtypeStruct((T_ALL, LANES), I32),
                   jax.ShapeDtypeStruct((1, LANES), I32)],
        grid=(T_ALL // NORM_TILE,),
        in_specs=[pl.BlockSpec((NORM_TILE, D_MODEL), lambda i: (i, 0)),
                  pl.BlockSpec((1, D_MODEL), lambda i: (0, 0)),
                  pl.BlockSpec((D_MODEL, LANES), lambda i: (0, 0)),
                  pl.BlockSpec((1, LANES), lambda i: (0, 0))],
        out_specs=[pl.BlockSpec((NORM_TILE, HALF_MODEL), lambda i: (i, 0)),
                   tile_spec, tile_spec, tile_spec,
                   pl.BlockSpec((1, LANES), lambda i: (0, 0))],
        scratch_shapes=[pltpu.VMEM((1, LANES), F32), pltpu.VMEM((NORM_TILE, NORM_TILE), BF16)],
        compiler_params=_params(("arbitrary",)),
        name="router",
    )(x2, g_ffn, w_router_pad, b_router_pad)


def _experts_kernel(item_e, item_row0, item_nblk, used_blocks, *rest):
    xs_parts, rest = rest[:XS_PARTS], rest[XS_PARTS:]
    wg_ref, wu_ref, wd_ref, bg_ref, bu_ref, bd_ref = rest[:6]
    caches, news = rest[6:12], rest[12:18]
    out_hbm, cache_outs = rest[18], rest[19:25]
    xbuf, acc, wg_bf, wu_bf, wd_bf, sem_in, sem_out, sbuf, sem_shift_in, sem_shift_out = rest[25:]
    i = pl.program_id(0)
    j = pl.program_id(1)
    n_j = pl.num_programs(1)
    nblk = item_nblk[i]

    _shift_step(i * n_j + j, caches, news, cache_outs, sbuf, sem_shift_in, sem_shift_out)

    @pl.when(jnp.logical_and(i == 0, j == 0))
    def _():
        acc[0:EXPERT_BLOCK, :] = jnp.zeros((EXPERT_BLOCK, D_MODEL), F32)

        def zero_copy(c):
            dst = pl.ds(pl.multiple_of((used_blocks[0] + c) * EXPERT_BLOCK, EXPERT_BLOCK), EXPERT_BLOCK)
            return pltpu.make_async_copy(acc.at[0:EXPERT_BLOCK, :], out_hbm.at[dst, :], sem_out)

        n_slack = N_SORTED_BLOCKS - used_blocks[0]
        pl.loop(0, n_slack)(lambda c: zero_copy(c).start())
        pl.loop(0, n_slack)(lambda c: zero_copy(c).wait())

    n_items = pl.num_programs(0)
    x_slot = i % 2

    def x_copy(item, c, action):
        rows = pl.ds(pl.multiple_of(c * EXPERT_BLOCK, EXPERT_BLOCK), EXPERT_BLOCK)
        block = item_row0[item] // EXPERT_BLOCK + c
        for part, xs_hbm in enumerate(xs_parts):
            @pl.when(block // XS_PART_BLOCKS == part)
            def _(part=part, xs_hbm=xs_hbm):
                src = pl.ds(pl.multiple_of((block - part * XS_PART_BLOCKS) * EXPERT_BLOCK, EXPERT_BLOCK),
                            EXPERT_BLOCK)
                cp = pltpu.make_async_copy(xs_hbm.at[src, :], xbuf.at[item % 2, rows, :], sem_in.at[item % 2])
                getattr(cp, action)()

    def out_copy(item, c):
        rows = pl.ds(pl.multiple_of(c * EXPERT_BLOCK, EXPERT_BLOCK), EXPERT_BLOCK)
        dst = pl.ds(pl.multiple_of(item_row0[item] + c * EXPERT_BLOCK, EXPERT_BLOCK), EXPERT_BLOCK)
        return pltpu.make_async_copy(acc.at[rows, :], out_hbm.at[dst, :], sem_out)

    @pl.when(j == 0)
    def _():
        @pl.when(i == 0)
        def _():
            pl.loop(0, nblk)(lambda c: x_copy(i, c, "start"))

        pl.loop(0, nblk)(lambda c: x_copy(i, c, "wait"))
        nxt = jnp.minimum(i + 1, n_items - 1)

        @pl.when(i + 1 < n_items)
        def _():
            pl.loop(0, item_nblk[nxt])(lambda c: x_copy(nxt, c, "start"))

    @pl.when(nblk > 0)
    def _():
        wg_bf[...] = wg_ref[...].astype(BF16)
        wu_bf[...] = wu_ref[...].astype(BF16)
        wd_bf[...] = wd_ref[...].astype(BF16)

    prev = jnp.maximum(i - 1, 0)

    @pl.when(jnp.logical_and(j == 0, i > 0))
    def _():
        pl.loop(0, item_nblk[prev])(lambda c: out_copy(prev, c).wait())

    @pl.when(nblk > 0)
    def _():
        @pl.when(j == 0)
        def _():
            @pl.loop(0, nblk)
            def _(c):
                rows = pl.ds(pl.multiple_of(c * EXPERT_BLOCK, EXPERT_BLOCK), EXPERT_BLOCK)
                acc[rows, :] = jnp.broadcast_to(bd_ref[...], (EXPERT_BLOCK, D_MODEL))

        def process(start, size):
            rows = pl.ds(pl.multiple_of(start, EXPERT_BLOCK), size)
            xa, xb = _unpack_bf16_pairs(xbuf[x_slot, rows, :])
            gt = (jnp.dot(xa, wg_bf[0:HALF_MODEL, :], preferred_element_type=F32)
                  + jnp.dot(xb, wg_bf[HALF_MODEL:D_MODEL, :], preferred_element_type=F32)) + bg_ref[...]
            up = (jnp.dot(xa, wu_bf[0:HALF_MODEL, :], preferred_element_type=F32)
                  + jnp.dot(xb, wu_bf[HALF_MODEL:D_MODEL, :], preferred_element_type=F32)) + bu_ref[...]
            gt = jnp.minimum(gt, SWIGLU_LIMIT)
            up = jnp.clip(up, -SWIGLU_LIMIT, SWIGLU_LIMIT)
            act = gt * jax.nn.sigmoid(SWIGLU_ALPHA * gt) * (up + 1.0)
            acc[rows, :] += jnp.dot(act.astype(BF16), wd_bf[...], preferred_element_type=F32)

        quad = 4 * EXPERT_BLOCK

        @pl.loop(0, nblk // 4)
        def _(c):
            process(c * quad, 2 * EXPERT_BLOCK)
            process(c * quad + 2 * EXPERT_BLOCK, 2 * EXPERT_BLOCK)

        tail = (nblk // 4) * quad

        @pl.when(nblk % 4 >= 2)
        def _():
            process(tail, 2 * EXPERT_BLOCK)

        @pl.when(nblk % 2 == 1)
        def _():
            process((nblk - 1) * EXPERT_BLOCK, EXPERT_BLOCK)

    @pl.when(j == n_j - 1)
    def _():
        pl.loop(0, nblk)(lambda c: out_copy(i, c).start())

        @pl.when(i == n_items - 1)
        def _():
            pl.loop(0, nblk)(lambda c: out_copy(i, c).wait())


def _experts(item_e, item_row0, item_nblk, used_blocks, xs_parts, w_gate, w_up, w_down, b_gate, b_up, b_down,
             caches, news):
    n_j = D_FF // FF_TILE
    assert N_WORK_ITEMS * n_j > SHIFT_TOTAL_STEPS

    def jj(i, j, nblk):
        return jnp.where(nblk[i] > 0, j, n_j - 1)

    any_spec = pl.BlockSpec(memory_space=pl.ANY)
    grid_spec = pltpu.PrefetchScalarGridSpec(
        num_scalar_prefetch=4,
        grid=(N_WORK_ITEMS, n_j),
        in_specs=[any_spec] * XS_PARTS + [
            pl.BlockSpec((None, None, D_MODEL, FF_TILE), lambda i, j, e, r, n, u: (0, e[i], 0, jj(i, j, n))),
            pl.BlockSpec((None, None, D_MODEL, FF_TILE), lambda i, j, e, r, n, u: (0, e[i], 0, jj(i, j, n))),
            pl.BlockSpec((None, None, FF_TILE, D_MODEL), lambda i, j, e, r, n, u: (0, e[i], jj(i, j, n), 0)),
            pl.BlockSpec((None, 1, FF_TILE), lambda i, j, e, r, n, u: (e[i], 0, jj(i, j, n))),
            pl.BlockSpec((None, 1, FF_TILE), lambda i, j, e, r, n, u: (e[i], 0, jj(i, j, n))),
            pl.BlockSpec((None, 1, D_MODEL), lambda i, j, e, r, n, u: (e[i], 0, 0)),
        ] + [any_spec] * 6 + [pl.BlockSpec(memory_space=pltpu.VMEM)] * 6,
        out_specs=[any_spec] * 7,
        scratch_shapes=[pltpu.VMEM((2, EXPERT_CAP, HALF_MODEL), F32),
                        pltpu.VMEM((EXPERT_CAP, D_MODEL), F32),
                        pltpu.VMEM((D_MODEL, FF_TILE), BF16),
                        pltpu.VMEM((D_MODEL, FF_TILE), BF16),
                        pltpu.VMEM((FF_TILE, D_MODEL), BF16),
                        pltpu.SemaphoreType.DMA((2,)),
                        pltpu.SemaphoreType.DMA(()),
                        pltpu.VMEM((SHIFT_SLOTS, SHIFT_CHUNK_ROWS, HEADS_PER_GROUP, HEAD_DIM), F32),
                        pltpu.SemaphoreType.DMA((SHIFT_SLOTS,)),
                        pltpu.SemaphoreType.DMA((SHIFT_SLOTS,))],
    )
    results = pl.pallas_call(
        _experts_kernel,
        out_shape=[jax.ShapeDtypeStruct((N_SORTED_ROWS, D_MODEL), F32)]
                  + [jax.ShapeDtypeStruct(c.shape, c.dtype) for c in caches],
        grid_spec=grid_spec,
        compiler_params=_params(("arbitrary", "arbitrary")),
        name="experts",
    )(item_e, item_row0, item_nblk, used_blocks, *xs_parts, w_gate, w_up, w_down, b_gate, b_up, b_down,
      *caches, *news)
    return results[0], results[1:]


def _combine_kernel(x_ref, gate_ref, o0, o1, o2, o3, y_ref):
    gates = gate_ref[...]
    y = x_ref[...]
    for k, o_ref in enumerate((o0, o1, o2, o3)):
        y = y + gates[:, k:k + 1] * o_ref[...]
    y_ref[...] = y


def _combine(x2, gates, outs):
    row = pl.BlockSpec((COMBINE_TILE, D_MODEL), lambda i: (i, 0))
    return pl.pallas_call(
        _combine_kernel,
        out_shape=jax.ShapeDtypeStruct((T_ALL, D_MODEL), F32),
        grid=(T_ALL // COMBINE_TILE,),
        in_specs=[row, pl.BlockSpec((COMBINE_TILE, LANES), lambda i: (i, 0)), row, row, row, row],
        out_specs=row,
        compiler_params=_params(("parallel",)),
        name="combine",
    )(x2, gates, *outs)


def _rope_tables():
    half = ROT_DIM // 2
    inv = ROPE_THETA ** (-2.0 * jnp.arange(half, dtype=F32) / ROT_DIM)
    pos = jnp.concatenate([jnp.arange(SEQ), jnp.full((DEC_BATCH,), PAST_LEN)]).astype(F32)
    ang = pos[:, None] * inv[None, :]
    cos, sin = jnp.cos(ang), jnp.sin(ang)
    rest = HEAD_DIM - ROT_DIM
    cos_t = jnp.concatenate([cos, cos, jnp.ones((T_ALL, rest), F32)], axis=-1)
    sin_t = jnp.concatenate([-sin, sin, jnp.zeros((T_ALL, rest), F32)], axis=-1)
    return cos_t, sin_t


def _dispatch_plan(sel, ranks, counts):
    nb = (counts + EXPERT_BLOCK - 1) // EXPERT_BLOCK
    padded = nb * EXPERT_BLOCK
    pad_start = jnp.cumsum(padded) - padded
    dest = pad_start[sel] + ranks
    n_items = (nb + EXPERT_CAP_BLOCKS - 1) // EXPERT_CAP_BLOCKS
    item_end = jnp.cumsum(n_items)
    item_start = item_end - n_items
    w = jnp.arange(N_WORK_ITEMS, dtype=I32)
    total = item_end[-1]
    w_eff = jnp.minimum(w, total - 1)
    e_w = jnp.minimum(jnp.searchsorted(item_end, w_eff, side='right'), N_EXPERTS - 1).astype(I32)
    k_w = w_eff - item_start[e_w]
    row0 = pad_start[e_w] + k_w * EXPERT_CAP
    nblk = jnp.clip(nb[e_w] - k_w * EXPERT_CAP_BLOCKS, 0, EXPERT_CAP_BLOCKS)
    nblk = jnp.where(w < total, nblk, 0)
    used_blocks = jnp.sum(nb).astype(I32).reshape(1)
    return dest, e_w, row0.astype(I32), nblk.astype(I32), used_blocks


def kernel(x_prompt, x_sample, cache_k_w128, cache_v_w128, cache_k_w512, cache_v_w512, cache_k_w2048,
           cache_v_w2048, state_conv, state_rglru, g_mix, w_in, q_norm, k_norm, w_attn_proj, conv_w, conv_b,
           w_rg_a, b_rg_a, w_rg_x, b_rg_x, lru_lambda, w_rnn_proj, w_out, g_ffn, w_router, b_router,
           w_gate, b_gate, w_up, b_up, w_down, b_down):
    caches = (cache_k_w128, cache_v_w128, cache_k_w512, cache_v_w512, cache_k_w2048, cache_v_w2048)
    x = jnp.concatenate([x_prompt[0], x_sample[:, 0]], axis=0)

    xn = _rmsnorm_bf16(x, g_mix)
    cos_t, sin_t = _rope_tables()
    gains = jnp.stack([q_norm, k_norm])
    p = _in_proj(xn, w_in, cos_t, sin_t, gains)

    attn_p = _attn_prompt(p)
    qkv_s = p[SEQ:, :3 * ATTN_WIDTH].reshape(DEC_BATCH, 3, N_HEADS, HEAD_DIM)
    q_s, k_s, v_s = qkv_s[:, 0], qkv_s[:, 1], qkv_s[:, 2]
    views = [c.reshape(DEC_BATCH, ATTN_BLOCK, DILATIONS[n // 2], HEADS_PER_GROUP, HEAD_DIM)
             for n, c in enumerate(caches)]
    attn_s = _attn_sample(q_s, k_s, v_s, views).reshape(DEC_BATCH, ATTN_OUT)
    news = []
    for g in range(N_GROUPS):
        heads = slice(g * HEADS_PER_GROUP, (g + 1) * HEADS_PER_GROUP)
        news += [k_s[:, heads], v_s[:, heads]]
    attn = jnp.concatenate([attn_p, attn_s], axis=0)

    y_p, h_p = _rglru_prompt(p, conv_w, conv_b, w_rg_a, b_rg_a, w_rg_x, b_rg_x, lru_lambda)
    hist = jnp.transpose(state_conv[0], (1, 0, 2))
    y_s, h_s = _rglru_sample(p, hist, state_rglru[0], conv_w, conv_b, w_rg_a, b_rg_a, w_rg_x, b_rg_x,
                             lru_lambda)
    y = jnp.concatenate([y_p, y_s], axis=0)

    mix = _mix(attn, y, p, w_attn_proj, w_rnn_proj)
    x2 = _out_proj(mix, w_out, x)

    w_router_pad = jnp.pad(w_router[0], ((0, 0), (0, LANES - N_EXPERTS)))
    b_router_pad = jnp.pad(b_router, ((0, 0), (0, LANES - N_EXPERTS)), constant_values=NEG_BIG)
    xn2, sel, gates, ranks, counts = _router(x2, g_ffn, w_router_pad, b_router_pad)
    dest, item_e, item_row0, item_nblk, used_blocks = _dispatch_plan(
        sel[:, :TOP_K], ranks[:, :TOP_K], counts[0, :N_EXPERTS])
    tok = jnp.repeat(jnp.arange(T_ALL, dtype=I32), TOP_K)
    src_tok = jnp.zeros((N_SORTED_ROWS,), I32).at[dest.reshape(-1)].set(tok)
    part_rows = XS_PART_BLOCKS * EXPERT_BLOCK
    xs_parts = [xn2[src_tok[k * part_rows:(k + 1) * part_rows]] for k in range(XS_PARTS)]
    out_sorted, new_caches = _experts(item_e, item_row0, item_nblk, used_blocks, xs_parts, w_gate, w_up, w_down,
                                      b_gate.reshape(N_EXPERTS, 1, D_FF), b_up.reshape(N_EXPERTS, 1, D_FF),
                                      b_down.reshape(N_EXPERTS, 1, D_MODEL), caches, news)
    outs = [out_sorted[dest[:, k]] for k in range(TOP_K)]
    y_all = _combine(x2, gates, outs)

    y_prompt = y_all[:SEQ].reshape(1, SEQ, D_MODEL)
    y_sample = y_all[SEQ:].reshape(DEC_BATCH, 1, D_MODEL)
    states_p = []
    for g, w in enumerate(WINDOWS):
        keep = min(w, SEQ)
        for off in (ATTN_WIDTH, 2 * ATTN_WIDTH):
            c0 = off + g * ATTN_OUT
            states_p.append(p[SEQ - keep:SEQ, c0:c0 + ATTN_OUT].reshape(1, 1, keep, HEADS_PER_GROUP, HEAD_DIM))
    conv_p = p[SEQ - (CONV_W - 1):SEQ, 3 * ATTN_WIDTH:3 * ATTN_WIDTH + D_RNN].reshape(1, 1, CONV_W - 1, D_RNN)
    rglru_p = h_p.reshape(1, 1, D_RNN)
    rnn_x_s = p[SEQ:, 3 * ATTN_WIDTH:3 * ATTN_WIDTH + D_RNN]
    conv_s = jnp.concatenate([state_conv[0][:, 1:], rnn_x_s[:, None, :]], axis=1)[None]
    rglru_s = h_s[None]
    return (y_prompt, y_sample, *states_p, conv_p, rglru_p, *new_caches, conv_s, rglru_s)
```

```python
import functools

import jax
import jax.numpy as jnp
from jax import lax
from jax.experimental import pallas as pl
from jax.experimental.pallas import tpu as pltpu

F32 = jnp.float32
BF16 = jnp.bfloat16
I32 = jnp.int32

D_MODEL = 2048
SEQ = 8192
DEC_BATCH = 128
PAST_LEN = 2048
T_ALL = SEQ + DEC_BATCH

HEAD_DIM = 128
HEADS_PER_GROUP = 4
WINDOWS = (128, 512, 2048)
DILATIONS = (1, 4, 16)
N_GROUPS = 3
N_HEADS = N_GROUPS * HEADS_PER_GROUP
ATTN_WIDTH = N_HEADS * HEAD_DIM
ATTN_OUT = HEADS_PER_GROUP * HEAD_DIM
ATTN_BLOCK = 128
ROT_DIM = HEAD_DIM // 4
ROPE_THETA = 500000.0
D_RNN = D_MODEL
RNN_BLOCK_W = 128
CONV_W = 4
LRU_C = 8.0
N_EXPERTS = 32
TOP_K = 4
D_FF = D_MODEL
SWIGLU_LIMIT = 7.0
SWIGLU_ALPHA = 1.702
EPS = 1e-6
IN_WIDTH = 3 * ATTN_WIDTH + 2 * D_RNN + 2 * D_MODEL

LANES = 128
SUBLANES = 8
VMEM_LIMIT_BYTES = 56 * 1024 * 1024

COL_TILE = 512
COL_RNN_X = (3 * ATTN_WIDTH) // COL_TILE
COL_RNN_GATE = (3 * ATTN_WIDTH + D_RNN) // COL_TILE
COL_GATE_ATTN = (3 * ATTN_WIDTH + 2 * D_RNN) // COL_TILE
COL_GATE_RNN = (3 * ATTN_WIDTH + 2 * D_RNN + D_MODEL) // COL_TILE

ROW_TILE = 1664
MIX_ROW_TILE = 832
NORM_TILE = 640
COMBINE_TILE = 320
SPAN = 2048
NEG_BIG = -1e30

EXPERT_BLOCK = 128
EXPERT_CAP_BLOCKS = 12
EXPERT_CAP = EXPERT_BLOCK * EXPERT_CAP_BLOCKS
FF_TILE = 256
N_SLOTS = T_ALL * TOP_K
N_SORTED_ROWS = N_SLOTS + N_EXPERTS * EXPERT_BLOCK
N_SORTED_BLOCKS = N_SORTED_ROWS // EXPERT_BLOCK
N_WORK_ITEMS = N_EXPERTS + -(-N_SORTED_BLOCKS // EXPERT_CAP_BLOCKS)
XS_PARTS = 4
XS_PART_BLOCKS = N_SORTED_BLOCKS // XS_PARTS
assert XS_PART_BLOCKS * XS_PARTS == N_SORTED_BLOCKS


def _params(semantics, vmem=VMEM_LIMIT_BYTES):
    return pltpu.CompilerParams(dimension_semantics=semantics, vmem_limit_bytes=vmem)


def _rmsnorm_kernel(x_ref, g_ref, o_ref):
    x = x_ref[...]
    y = x * lax.rsqrt(jnp.mean(x * x, axis=-1, keepdims=True) + EPS)
    o_ref[...] = (y * g_ref[...]).astype(o_ref.dtype)


def _rmsnorm_bf16(x, g):
    t = x.shape[0]
    return pl.pallas_call(
        _rmsnorm_kernel,
        out_shape=jax.ShapeDtypeStruct((t, D_MODEL), BF16),
        grid=(t // NORM_TILE,),
        in_specs=[pl.BlockSpec((NORM_TILE, D_MODEL), lambda i: (i, 0)),
                  pl.BlockSpec((1, D_MODEL), lambda i: (0, 0))],
        out_specs=pl.BlockSpec((NORM_TILE, D_MODEL), lambda i: (i, 0)),
        compiler_params=_params(("parallel",)),
        name="rmsnorm_bf16",
    )(x, g)


def _in_proj_kernel(x_ref, w_ref, cos_ref, sin_ref, gain_ref, o_ref, wbf_ref):
    j = pl.program_id(0)

    @pl.when(pl.program_id(1) == 0)
    def _():
        wbf_ref[...] = w_ref[...].astype(BF16)

    o_ref[...] = jnp.dot(x_ref[...], wbf_ref[...], preferred_element_type=F32)

    @pl.when(j < 2 * ATTN_WIDTH // COL_TILE)
    def _():
        lane = lax.broadcasted_iota(I32, (ROW_TILE, HEAD_DIM), 1)
        first_half = lane < ROT_DIM // 2
        gain = gain_ref[...]
        cos = cos_ref[...]
        sin = sin_ref[...]
        for h in range(COL_TILE // HEAD_DIM):
            cols = slice(h * HEAD_DIM, (h + 1) * HEAD_DIM)
            xh = o_ref[:, cols]
            y = xh * lax.rsqrt(jnp.mean(xh * xh, axis=-1, keepdims=True) + EPS) * gain
            partner = jnp.where(first_half,
                                pltpu.roll(y, HEAD_DIM - ROT_DIM // 2, 1),
                                pltpu.roll(y, ROT_DIM // 2, 1))
            o_ref[:, cols] = y * cos + partner * sin


def _in_proj(xn, w_in, cos_t, sin_t, gains):
    n_tiles = IN_WIDTH // COL_TILE
    qk_tiles = ATTN_WIDTH // COL_TILE
    return pl.pallas_call(
        _in_proj_kernel,
        out_shape=jax.ShapeDtypeStruct((T_ALL, IN_WIDTH), F32),
        grid=(n_tiles, T_ALL // ROW_TILE),
        in_specs=[
            pl.BlockSpec((ROW_TILE, D_MODEL), lambda j, i: (i, 0)),
            pl.BlockSpec((None, D_MODEL, COL_TILE), lambda j, i: (0, 0, j)),
            pl.BlockSpec((ROW_TILE, HEAD_DIM), lambda j, i: (i, 0)),
            pl.BlockSpec((ROW_TILE, HEAD_DIM), lambda j, i: (i, 0)),
            pl.BlockSpec((None, 1, HEAD_DIM), lambda j, i: (jnp.minimum(j // qk_tiles, 1), 0, 0)),
        ],
        out_specs=pl.BlockSpec((ROW_TILE, COL_TILE), lambda j, i: (i, j)),
        scratch_shapes=[pltpu.VMEM((D_MODEL, COL_TILE), BF16)],
        compiler_params=_params(("arbitrary", "arbitrary")),
        name="in_proj",
    )(xn, w_in, cos_t, sin_t, gains)


def _dot_nt(a, b):
    return lax.dot_general(a, b, (((1,), (1,)), ((), ())), preferred_element_type=F32)


def _attn_prompt_kernel(*refs):
    ins = refs[:15]
    o_ref = refs[15]
    og_ref, lse_ref = refs[16], refs[17]
    span_idx = pl.program_id(0)
    n_sub = SPAN // ATTN_BLOCK
    shape3 = (n_sub, ATTN_BLOCK, ATTN_BLOCK)
    sub = lax.broadcasted_iota(I32, shape3, 0)
    qi = lax.broadcasted_iota(I32, shape3, 1)
    kj = lax.broadcasted_iota(I32, shape3, 2)
    mask_cur = qi >= kj
    mask_prev_band = kj >= qi
    scale = HEAD_DIM ** -0.5

    def bmm_nt(a, b):
        return lax.dot_general(a, b, (((2,), (2,)), ((0,), (0,))), preferred_element_type=F32)

    def bmm(a, b):
        return lax.dot_general(a, b, (((2,), (1,)), ((0,), (0,))), preferred_element_type=F32)

    for g, dil in enumerate(DILATIONS):
        q_ref, kc_ref, vc_ref, kp_ref, vp_ref = ins[5 * g:5 * g + 5]
        blk = ATTN_BLOCK * dil
        blocks = [(m, r) for m in range(SPAN // blk) for r in range(dil)]

        def rows_of(m, r, dil=dil, blk=blk):
            return pl.ds(m * blk + r, ATTN_BLOCK, stride=dil) if dil > 1 else pl.ds(m * blk, ATTN_BLOCK)

        q3 = jnp.stack([(q_ref[rows_of(m, r), :] * scale).astype(BF16) for m, r in blocks])
        kc = [kc_ref[rows_of(m, r), :].astype(BF16) for m, r in blocks]
        vc = [vc_ref[rows_of(m, r), :].astype(BF16) for m, r in blocks]
        kp = [kp_ref[rows_of(0, r), :].astype(BF16) if m == 0 else kc[(m - 1) * dil + r] for m, r in blocks]
        vp = [vp_ref[rows_of(0, r), :].astype(BF16) if m == 0 else vc[(m - 1) * dil + r] for m, r in blocks]
        mask_prev = jnp.logical_and(mask_prev_band, jnp.logical_or(sub >= dil, span_idx > 0))
        s_cur = jnp.where(mask_cur, bmm_nt(q3, jnp.stack(kc)), NEG_BIG)
        s_prev = jnp.where(mask_prev, bmm_nt(q3, jnp.stack(kp)), NEG_BIG)
        mx = jnp.maximum(jnp.max(s_cur, axis=-1, keepdims=True), jnp.max(s_prev, axis=-1, keepdims=True))
        p_cur = jnp.exp(s_cur - mx)
        p_prev = jnp.exp(s_prev - mx)
        den = jnp.sum(p_cur, axis=-1, keepdims=True) + jnp.sum(p_prev, axis=-1, keepdims=True)
        pv = bmm(p_cur.astype(BF16), jnp.stack(vc)) + bmm(p_prev.astype(BF16), jnp.stack(vp))
        out3 = pv / den
        lse3 = jnp.broadcast_to(mx + jnp.log(den), (n_sub, ATTN_BLOCK, HEAD_DIM))
        for n, (m, r) in enumerate(blocks):
            og_ref[g, rows_of(m, r), :] = out3[n]
            lse_ref[g, rows_of(m, r), :] = lse3[n]

    lse = lse_ref[...]
    top = jnp.max(lse, axis=0)
    w = jnp.exp(lse - top[None])
    o_ref[...] = jnp.sum(w * og_ref[...], axis=0) / jnp.sum(w, axis=0)


def _attn_prompt(p):
    q_cols, k_cols, v_cols = 0, N_HEADS, 2 * N_HEADS
    in_specs = []
    for g, dil in enumerate(DILATIONS):
        blk = ATTN_BLOCK * dil
        per_span = SPAN // blk

        def cur(off, g=g):
            return pl.BlockSpec((SPAN, HEAD_DIM), lambda s, h: (s, off + g * HEADS_PER_GROUP + h))

        def prev(off, g=g, blk=blk, per_span=per_span):
            return pl.BlockSpec((blk, HEAD_DIM),
                                lambda s, h: (jnp.maximum(s * per_span - 1, 0), off + g * HEADS_PER_GROUP + h))

        in_specs += [cur(q_cols), cur(k_cols), cur(v_cols), prev(k_cols), prev(v_cols)]
    return pl.pallas_call(
        _attn_prompt_kernel,
        out_shape=jax.ShapeDtypeStruct((SEQ, ATTN_OUT), F32),
        grid=(SEQ // SPAN, HEADS_PER_GROUP),
        in_specs=in_specs,
        out_specs=pl.BlockSpec((SPAN, HEAD_DIM), lambda s, h: (s, h)),
        scratch_shapes=[pltpu.VMEM((N_GROUPS, SPAN, HEAD_DIM), F32),
                        pltpu.VMEM((N_GROUPS, SPAN, HEAD_DIM), F32)],
        compiler_params=_params(("parallel", "parallel")),
        name="attn_prompt",
    )(*([p] * 15))


SAMPLE_BB = 8


def _attn_sample_kernel(q_ref, k_ref, v_ref, ck0, cv0, ck1, cv1, ck2, cv2, o_ref):
    caches = ((ck0, cv0), (ck1, cv1), (ck2, cv2))
    scale = HEAD_DIM ** -0.5
    for b in range(SAMPLE_BB):
        outs, lses = [], []
        for g in range(N_GROUPS):
            heads = slice(g * HEADS_PER_GROUP, (g + 1) * HEADS_PER_GROUP)
            q = q_ref[b, heads, :] * scale
            k_new = k_ref[b, heads, :]
            v_new = v_ref[b, heads, :]
            k_old = caches[g][0][b]
            v_old = caches[g][1][b]
            s_old = jnp.sum(k_old * q[None], axis=-1, keepdims=True)
            s_new = jnp.sum(k_new * q, axis=-1, keepdims=True)
            mx = jnp.maximum(jnp.max(s_old, axis=0), s_new)
            p_old = jnp.exp(s_old - mx[None])
            p_new = jnp.exp(s_new - mx)
            den = jnp.sum(p_old, axis=0) + p_new
            pv = jnp.sum(p_old * v_old, axis=0) + p_new * v_new
            outs.append(pv / den)
            lses.append(mx + jnp.log(den))
        top = jnp.maximum(jnp.maximum(lses[0], lses[1]), lses[2])
        ws = [jnp.exp(l - top) for l in lses]
        o_ref[b] = (ws[0] * outs[0] + ws[1] * outs[1] + ws[2] * outs[2]) / (ws[0] + ws[1] + ws[2])


def _attn_sample(q_s, k_s, v_s, cache_views):
    row = pl.BlockSpec((SAMPLE_BB, N_HEADS, HEAD_DIM), lambda b: (b, 0, 0))
    cache_spec = pl.BlockSpec((SAMPLE_BB, ATTN_BLOCK, None, HEADS_PER_GROUP, HEAD_DIM),
                              lambda b: (b, 0, 0, 0, 0))
    return pl.pallas_call(
        _attn_sample_kernel,
        out_shape=jax.ShapeDtypeStruct((DEC_BATCH, HEADS_PER_GROUP, HEAD_DIM), F32),
        grid=(DEC_BATCH // SAMPLE_BB,),
        in_specs=[row, row, row] + [cache_spec] * 6,
        out_specs=pl.BlockSpec((SAMPLE_BB, HEADS_PER_GROUP, HEAD_DIM), lambda b: (b, 0, 0)),
        compiler_params=_params(("parallel",)),
        name="attn_sample",
    )(q_s, k_s, v_s, *cache_views)


SHIFT_CHUNK_ROWS = 1024
SHIFT_CHUNKS_PER_STEP = 2
SHIFT_SLOTS = 2 * SHIFT_CHUNKS_PER_STEP
SHIFT_DMA_QUEUE = 1
FUSED_SHIFT_WINDOWS = (WINDOWS[2], WINDOWS[2])
N_FUSED_SHIFT = len(FUSED_SHIFT_WINDOWS)
STANDALONE_SHIFT_ROWS = 2048


def _shift_steps(window):
    return DEC_BATCH * window // (SHIFT_CHUNK_ROWS * SHIFT_CHUNKS_PER_STEP)


def _shift_schedule():
    spans, lo = [], 0
    for w in FUSED_SHIFT_WINDOWS:
        spans.append((lo, lo + _shift_steps(w)))
        lo += _shift_steps(w)
    return spans


SHIFT_TOTAL_STEPS = _shift_schedule()[-1][1]


def _shift_one_cache(cache, new, out):
    w = cache.shape[2]
    bb = STANDALONE_SHIFT_ROWS // w
    n_chunks = DEC_BATCH // bb

    def run(buf, sem_in, sem_out):
        def in_copy(c, slot):
            return pltpu.make_async_copy(cache.at[0, pl.ds(c * bb, bb), pl.ds(1, w - 1)],
                                         buf.at[slot, :, pl.ds(0, w - 1)], sem_in.at[slot])

        def out_copy(c, slot):
            return pltpu.make_async_copy(buf.at[slot], out.at[0, pl.ds(c * bb, bb)], sem_out.at[slot])

        in_copy(0, 0).start()

        @pl.loop(0, n_chunks)
        def _(c):
            slot = c % 2
            other = 1 - slot

            @pl.when(c + 1 < n_chunks)
            def _():
                @pl.when(c >= 1)
                def _():
                    out_copy(c - 1, other).wait()

                in_copy(c + 1, other).start()

            in_copy(c, slot).wait()
            buf[slot, :, w - 1] = new[pl.ds(c * bb, bb)]
            out_copy(c, slot).start()

        out_copy(n_chunks - 2, n_chunks % 2).wait()
        out_copy(n_chunks - 1, (n_chunks - 1) % 2).wait()

    pl.run_scoped(run, pltpu.VMEM((2, bb, w, HEADS_PER_GROUP, HEAD_DIM), F32),
                  pltpu.SemaphoreType.DMA((2,)), pltpu.SemaphoreType.DMA((2,)))


def _cache_shift_kernel(*refs):
    n = len(refs) // 3
    for a in range(n):
        _shift_one_cache(refs[a], refs[n + a], refs[2 * n + a])


def _cache_shift(caches, news):
    n = len(caches)
    any_spec = pl.BlockSpec(memory_space=pl.ANY)
    return pl.pallas_call(
        _cache_shift_kernel,
        out_shape=[jax.ShapeDtypeStruct(c.shape, c.dtype) for c in caches],
        in_specs=[any_spec] * n + [pl.BlockSpec(memory_space=pltpu.VMEM)] * n,
        out_specs=[any_spec] * n,
        compiler_params=pltpu.CompilerParams(vmem_limit_bytes=VMEM_LIMIT_BYTES),
        name="cache_shift",
    )(*caches, *news)


def _shift_chunk_copies(cache, out, sbuf, sem_in, sem_out, local_step, p, slot):
    w = cache.shape[2]
    ins, outs, new_rows = [], [], []
    copy = functools.partial(functools.partial, pltpu.make_async_copy)
    if w <= SHIFT_CHUNK_ROWS:
        per_chunk = SHIFT_CHUNK_ROWS // w
        b0 = (local_step * SHIFT_CHUNKS_PER_STEP + p) * per_chunk
        for bl in range(per_chunk):
            ins.append(copy(cache.at[0, b0 + bl, pl.ds(1, w - 1)], sbuf.at[slot, pl.ds(bl * w, w - 1)],
                            sem_in.at[slot]))
            outs.append(copy(sbuf.at[slot, pl.ds(bl * w, w)], out.at[0, b0 + bl], sem_out.at[slot]))
            new_rows.append((bl * w + w - 1, b0 + bl))
    else:
        assert w == SHIFT_CHUNK_ROWS * SHIFT_CHUNKS_PER_STEP
        last = p == SHIFT_CHUNKS_PER_STEP - 1
        n_in = SHIFT_CHUNK_ROWS - 1 if last else SHIFT_CHUNK_ROWS
        ins.append(copy(cache.at[0, local_step, pl.ds(1 + p * SHIFT_CHUNK_ROWS, n_in)],
                        sbuf.at[slot, pl.ds(0, n_in)], sem_in.at[slot]))
        outs.append(copy(sbuf.at[slot], out.at[0, local_step, pl.ds(p * SHIFT_CHUNK_ROWS, SHIFT_CHUNK_ROWS)],
                         sem_out.at[slot]))
        if last:
            new_rows.append((SHIFT_CHUNK_ROWS - 1, local_step))
    return ins, outs, new_rows


def _shift_step(step, caches, news, outs, sbuf, sem_in, sem_out):
    spans = _shift_schedule()

    def for_step(t, fn):
        for a, (lo, hi) in enumerate(spans):
            @pl.when(jnp.logical_and(t >= lo, t < hi))
            def _(a=a, lo=lo):
                for p in range(SHIFT_CHUNKS_PER_STEP):
                    slot = (t % 2) * SHIFT_CHUNKS_PER_STEP + p
                    ins, outs_, new_rows = _shift_chunk_copies(caches[a], outs[a], sbuf, sem_in, sem_out,
                                                               t - lo, p, slot)
                    fn(a, slot, ins, outs_, new_rows)

    def drain(a, slot, ins, outs_, new_rows):
        for make in outs_:
            make().wait()

    def prefetch(a, slot, ins, outs_, new_rows):
        for make in ins:
            make().start(priority=SHIFT_DMA_QUEUE)

    def forward(a, slot, ins, outs_, new_rows):
        for make in ins:
            make().wait()
        for row, b in new_rows:
            sbuf[slot, row] = news[a][b]
        for make in outs_:
            make().start(priority=SHIFT_DMA_QUEUE)

    @pl.when(step == 0)
    def _():
        for_step(step, prefetch)

    for_step(step - 1, drain)
    for_step(step + 1, prefetch)
    for_step(step, forward)


def _rglru_gates(xc, wa_ref, ba_ref, wx_ref, bx_ref, lam_ref):
    r_parts, i_parts = [], []
    for n in range(COL_TILE // RNN_BLOCK_W):
        xb = xc[:, n * RNN_BLOCK_W:(n + 1) * RNN_BLOCK_W].astype(BF16)
        r_parts.append(jnp.dot(xb, wa_ref[n].astype(BF16), preferred_element_type=F32))
        i_parts.append(jnp.dot(xb, wx_ref[n].astype(BF16), preferred_element_type=F32))
    r = jax.nn.sigmoid(jnp.concatenate(r_parts, axis=-1) + ba_ref[...])
    i = jax.nn.sigmoid(jnp.concatenate(i_parts, axis=-1) + bx_ref[...])
    neg_lam = -lam_ref[...]
    softplus = jnp.maximum(neg_lam, 0.0) + jnp.log1p(jnp.exp(-jnp.abs(neg_lam)))
    log_a = -LRU_C * r * softplus
    a = jnp.exp(log_a)
    b = jnp.sqrt(-jnp.tanh(log_a) * (jnp.exp(2.0 * log_a) + 1.0)) * i * xc
    return a, b


RNN_T_TILE = 512
CONV_PAD = SUBLANES


def _rglru_prompt_kernel(x_ref, gate_ref, cw_ref, cb_ref, wa_ref, ba_ref, wx_ref, bx_ref, lam_ref,
                         y_ref, hlast_ref, xbuf, a_scr, b_scr, h_scr, carry):
    t = pl.program_id(1)

    @pl.when(t == 0)
    def _():
        xbuf[0:CONV_PAD, :] = jnp.zeros((CONV_PAD, COL_TILE), F32)
        carry[...] = jnp.zeros((SUBLANES, COL_TILE), F32)

    @pl.when(t > 0)
    def _():
        xbuf[0:CONV_PAD, :] = xbuf[RNN_T_TILE:RNN_T_TILE + CONV_PAD, :]

    xbuf[CONV_PAD:CONV_PAD + RNN_T_TILE, :] = x_ref[...]
    xc = cb_ref[...] + sum(
        xbuf[pl.ds(CONV_PAD - (CONV_W - 1) + j, RNN_T_TILE), :] * cw_ref[j:j + 1, :] for j in range(CONV_W))
    a, b = _rglru_gates(xc, wa_ref, ba_ref, wx_ref, bx_ref, lam_ref)
    a_scr[...] = a
    b_scr[...] = b

    row = lax.broadcasted_iota(I32, (SUBLANES, COL_TILE), 0)

    def chunk(c, h):
        rows = pl.ds(pl.multiple_of(c * SUBLANES, SUBLANES), SUBLANES)
        ac = a_scr[rows, :]
        bc = b_scr[rows, :]
        for s in (1, 2, 4):
            a_sh = jnp.where(row >= s, pltpu.roll(ac, s, 0), 1.0)
            b_sh = jnp.where(row >= s, pltpu.roll(bc, s, 0), 0.0)
            bc = ac * b_sh + bc
            ac = ac * a_sh
        hh = ac * h + bc
        h_scr[rows, :] = hh
        return jnp.broadcast_to(hh[SUBLANES - 1:SUBLANES, :], (SUBLANES, COL_TILE))

    h_end = lax.fori_loop(0, RNN_T_TILE // SUBLANES, chunk, carry[...], unroll=4)
    carry[...] = h_end
    hlast_ref[...] = h_end[0:1, :]
    y_ref[...] = (h_scr[...] * jax.nn.gelu(gate_ref[...])).astype(y_ref.dtype)


def _rnn_param_specs(idx):
    return [
        pl.BlockSpec((None, CONV_W, COL_TILE), lambda *g: (0, 0, idx(*g))),
        pl.BlockSpec((1, COL_TILE), lambda *g: (0, idx(*g))),
        pl.BlockSpec((None, COL_TILE // RNN_BLOCK_W, RNN_BLOCK_W, RNN_BLOCK_W), lambda *g: (0, idx(*g), 0, 0)),
        pl.BlockSpec((1, COL_TILE), lambda *g: (0, idx(*g))),
        pl.BlockSpec((None, COL_TILE // RNN_BLOCK_W, RNN_BLOCK_W, RNN_BLOCK_W), lambda *g: (0, idx(*g), 0, 0)),
        pl.BlockSpec((1, COL_TILE), lambda *g: (0, idx(*g))),
        pl.BlockSpec((1, COL_TILE), lambda *g: (0, idx(*g))),
    ]


def _rglru_prompt(p, conv_w, conv_b, w_rg_a, b_rg_a, w_rg_x, b_rg_x, lru_lambda):
    n_c = D_RNN // COL_TILE
    return pl.pallas_call(
        _rglru_prompt_kernel,
        out_shape=[jax.ShapeDtypeStruct((SEQ, D_RNN), BF16), jax.ShapeDtypeStruct((1, D_RNN), F32)],
        grid=(n_c, SEQ // RNN_T_TILE),
        in_specs=[pl.BlockSpec((RNN_T_TILE, COL_TILE), lambda c, t: (t, COL_RNN_X + c)),
                  pl.BlockSpec((RNN_T_TILE, COL_TILE), lambda c, t: (t, COL_RNN_GATE + c))]
                 + _rnn_param_specs(lambda c, t: c),
        out_specs=[pl.BlockSpec((RNN_T_TILE, COL_TILE), lambda c, t: (t, c)),
                   pl.BlockSpec((1, COL_TILE), lambda c, t: (0, c))],
        scratch_shapes=[pltpu.VMEM((CONV_PAD + RNN_T_TILE, COL_TILE), F32),
                        pltpu.VMEM((RNN_T_TILE, COL_TILE), F32),
                        pltpu.VMEM((RNN_T_TILE, COL_TILE), F32),
                        pltpu.VMEM((RNN_T_TILE, COL_TILE), F32),
                        pltpu.VMEM((SUBLANES, COL_TILE), F32)],
        compiler_params=_params(("parallel", "arbitrary")),
        name="rglru_prompt",
    )(p, p, conv_w, conv_b, w_rg_a, b_rg_a, w_rg_x, b_rg_x, lru_lambda)


def _rglru_sample_kernel(x_ref, gate_ref, hist_ref, h0_ref, cw_ref, cb_ref, wa_ref, ba_ref, wx_ref, bx_ref,
                         lam_ref, y_ref, h_ref):
    xc = cb_ref[...] + x_ref[...] * cw_ref[CONV_W - 1:CONV_W, :]
    for j in range(CONV_W - 1):
        xc = xc + hist_ref[j] * cw_ref[j:j + 1, :]
    a, b = _rglru_gates(xc, wa_ref, ba_ref, wx_ref, bx_ref, lam_ref)
    h = a * h0_ref[...] + b
    h_ref[...] = h
    y_ref[...] = (h * jax.nn.gelu(gate_ref[...])).astype(y_ref.dtype)


def _rglru_sample(p, hist, h0, conv_w, conv_b, w_rg_a, b_rg_a, w_rg_x, b_rg_x, lru_lambda):
    n_c = D_RNN // COL_TILE
    row_blk = SEQ // DEC_BATCH
    return pl.pallas_call(
        _rglru_sample_kernel,
        out_shape=[jax.ShapeDtypeStruct((DEC_BATCH, D_RNN), BF16), jax.ShapeDtypeStruct((DEC_BATCH, D_RNN), F32)],
        grid=(n_c,),
        in_specs=[pl.BlockSpec((DEC_BATCH, COL_TILE), lambda c: (row_blk, COL_RNN_X + c)),
                  pl.BlockSpec((DEC_BATCH, COL_TILE), lambda c: (row_blk, COL_RNN_GATE + c)),
                  pl.BlockSpec((CONV_W - 1, DEC_BATCH, COL_TILE), lambda c: (0, 0, c)),
                  pl.BlockSpec((DEC_BATCH, COL_TILE), lambda c: (0, c))]
                 + _rnn_param_specs(lambda c: c),
        out_specs=[pl.BlockSpec((DEC_BATCH, COL_TILE), lambda c: (0, c)),
                   pl.BlockSpec((DEC_BATCH, COL_TILE), lambda c: (0, c))],
        compiler_params=_params(("parallel",)),
        name="rglru_sample",
    )(p, p, hist, h0, conv_w, conv_b, w_rg_a, b_rg_a, w_rg_x, b_rg_x, lru_lambda)


def _mix_kernel(attn_ref, y_ref, ga_ref, gr_ref, wap_ref, wrp_ref, o_ref, wap_bf, wrp_bf):
    @pl.when(pl.program_id(1) == 0)
    def _():
        wap_bf[...] = wap_ref[...].astype(BF16)
        wrp_bf[...] = wrp_ref[...].astype(BF16)

    attn_d = jnp.dot(attn_ref[...].astype(BF16), wap_bf[...], preferred_element_type=F32)
    rnn_d = jnp.dot(y_ref[...], wrp_bf[...], preferred_element_type=F32)
    mix = jax.nn.sigmoid(ga_ref[...]) * attn_d + jax.nn.sigmoid(gr_ref[...]) * rnn_d
    o_ref[...] = mix.astype(o_ref.dtype)


def _mix(attn, y, p, w_attn_proj, w_rnn_proj):
    return pl.pallas_call(
        _mix_kernel,
        out_shape=jax.ShapeDtypeStruct((T_ALL, D_MODEL), BF16),
        grid=(D_MODEL // COL_TILE, T_ALL // MIX_ROW_TILE),
        in_specs=[pl.BlockSpec((MIX_ROW_TILE, ATTN_OUT), lambda c, i: (i, 0)),
                  pl.BlockSpec((MIX_ROW_TILE, D_RNN), lambda c, i: (i, 0)),
                  pl.BlockSpec((MIX_ROW_TILE, COL_TILE), lambda c, i: (i, COL_GATE_ATTN + c)),
                  pl.BlockSpec((MIX_ROW_TILE, COL_TILE), lambda c, i: (i, COL_GATE_RNN + c)),
                  pl.BlockSpec((None, ATTN_OUT, COL_TILE), lambda c, i: (0, 0, c)),
                  pl.BlockSpec((None, D_RNN, COL_TILE), lambda c, i: (0, 0, c))],
        out_specs=pl.BlockSpec((MIX_ROW_TILE, COL_TILE), lambda c, i: (i, c)),
        scratch_shapes=[pltpu.VMEM((ATTN_OUT, COL_TILE), BF16), pltpu.VMEM((D_RNN, COL_TILE), BF16)],
        compiler_params=_params(("arbitrary", "arbitrary")),
        name="mix",
    )(attn, y, p, p, w_attn_proj, w_rnn_proj)


def _out_proj_kernel(mix_ref, w_ref, x_ref, o_ref, wbf):
    @pl.when(pl.program_id(1) == 0)
    def _():
        wbf[...] = w_ref[...].astype(BF16)

    o_ref[...] = x_ref[...] + jnp.dot(mix_ref[...], wbf[...], preferred_element_type=F32)


def _out_proj(mix, w_out, x):
    return pl.pallas_call(
        _out_proj_kernel,
        out_shape=jax.ShapeDtypeStruct((T_ALL, D_MODEL), F32),
        grid=(D_MODEL // COL_TILE, T_ALL // ROW_TILE),
        in_specs=[pl.BlockSpec((ROW_TILE, D_MODEL), lambda n, i: (i, 0)),
                  pl.BlockSpec((None, D_MODEL, COL_TILE), lambda n, i: (0, 0, n)),
                  pl.BlockSpec((ROW_TILE, COL_TILE), lambda n, i: (i, n))],
        out_specs=pl.BlockSpec((ROW_TILE, COL_TILE), lambda n, i: (i, n)),
        scratch_shapes=[pltpu.VMEM((D_MODEL, COL_TILE), BF16)],
        compiler_params=_params(("arbitrary", "arbitrary")),
        name="out_proj",
    )(mix, w_out, x)


HALF_MODEL = D_MODEL // 2
HIGH_HALF_MASK = -65536


def _pack_bf16_pairs(x):
    hi = pltpu.bitcast(x[:, :HALF_MODEL].astype(BF16).astype(F32), I32)
    lo = pltpu.bitcast(x[:, HALF_MODEL:].astype(BF16).astype(F32), I32)
    return pltpu.bitcast(hi | lax.shift_right_logical(lo, 16), F32)


def _unpack_bf16_pairs(words):
    bits = pltpu.bitcast(words, I32)
    first = pltpu.bitcast(bits & HIGH_HALF_MASK, F32).astype(BF16)
    second = pltpu.bitcast(lax.shift_left(bits, 16), F32).astype(BF16)
    return first, second


def _split_bf16(x):
    hi = x.astype(BF16)
    lo = (x - hi.astype(F32)).astype(BF16)
    return hi, lo


def _router_kernel(x_ref, g_ref, w_ref, b_ref, xn_ref, sel_ref, gate_ref, rank_ref, cnt_ref, carry, tri):
    step = pl.program_id(0)

    @pl.when(step == 0)
    def _():
        carry[...] = jnp.zeros((1, LANES), F32)
        ri = lax.broadcasted_iota(I32, (NORM_TILE, NORM_TILE), 0)
        ci = lax.broadcasted_iota(I32, (NORM_TILE, NORM_TILE), 1)
        tri[...] = jnp.where(ci < ri, 1.0, 0.0).astype(BF16)

    x = x_ref[...]
    xn = x * lax.rsqrt(jnp.mean(x * x, axis=-1, keepdims=True) + EPS) * g_ref[...]
    xn_ref[...] = _pack_bf16_pairs(xn)

    x_hi, x_lo = _split_bf16(xn)
    w_hi, w_lo = _split_bf16(w_ref[...])
    logits = (jnp.dot(x_hi, w_hi, preferred_element_type=F32)
              + jnp.dot(x_hi, w_lo, preferred_element_type=F32)
              + jnp.dot(x_lo, w_hi, preferred_element_type=F32)) + b_ref[...]

    lane = lax.broadcasted_iota(I32, (NORM_TILE, LANES), 1)
    work = logits
    vals, idxs = [], []
    for _ in range(TOP_K):
        mk = jnp.max(work, axis=-1, keepdims=True)
        ik = jnp.min(jnp.where(work == mk, lane, LANES), axis=-1, keepdims=True)
        vals.append(mk)
        idxs.append(ik)
        work = jnp.where(lane == ik, -jnp.inf, work)
    exps = [jnp.exp(v - vals[0]) for v in vals]
    den = exps[0] + exps[1] + exps[2] + exps[3]

    member = jnp.zeros((NORM_TILE, LANES), F32)
    for ik in idxs:
        member = member + jnp.where(lane == ik, 1.0, 0.0)
    before = jnp.dot(tri[...], member.astype(BF16), preferred_element_type=F32) + carry[...]
    sel = jnp.zeros((NORM_TILE, LANES), I32)
    gates = jnp.zeros((NORM_TILE, LANES), F32)
    ranks = jnp.zeros((NORM_TILE, LANES), F32)
    for k in range(TOP_K):
        rk = jnp.sum(jnp.where(lane == idxs[k], before, 0.0), axis=-1, keepdims=True)
        sel = jnp.where(lane == k, idxs[k], sel)
        gates = jnp.where(lane == k, exps[k] / den, gates)
        ranks = jnp.where(lane == k, rk, ranks)
    sel_ref[...] = sel
    gate_ref[...] = gates
    rank_ref[...] = ranks.astype(I32)
    carry[...] = carry[...] + jnp.sum(member, axis=0, keepdims=True)
    cnt_ref[...] = carry[...].astype(I32)


def _router(x2, g_ffn, w_router_pad, b_router_pad):
    tile_spec = pl.BlockSpec((NORM_TILE, LANES), lambda i: (i, 0))
    return pl.pallas_call(
        _router_kernel,
        out_shape=[jax.ShapeDtypeStruct((T_ALL, HALF_MODEL), F32),
                   jax.ShapeDtypeStruct((T_ALL, LANES), I32),
                   jax.ShapeDtypeStruct((T_ALL, LANES), F32),
                   jax.ShapeDtypeStruct((T_ALL, LANES), I32),
                   jax.ShapeDtypeStruct((1, LANES), I32)],
        grid=(T_ALL // NORM_TILE,),
        in_specs=[pl.BlockSpec((NORM_TILE, D_MODEL), lambda i: (i, 0)),
                  pl.BlockSpec((1, D_MODEL), lambda i: (0, 0)),
                  pl.BlockSpec((D_MODEL, LANES), lambda i: (0, 0)),
                  pl.BlockSpec((1, LANES), lambda i: (0, 0))],
        out_specs=[pl.BlockSpec((NORM_TILE, HALF_MODEL), lambda i: (i, 0)),
                   tile_spec, tile_spec, tile_spec,
                   pl.BlockSpec((1, LANES), lambda i: (0, 0))],
        scratch_shapes=[pltpu.VMEM((1, LANES), F32), pltpu.VMEM((NORM_TILE, NORM_TILE), BF16)],
        compiler_params=_params(("arbitrary",)),
        name="router",
    )(x2, g_ffn, w_router_pad, b_router_pad)


def _experts_kernel(item_e, item_row0, item_nblk, used_blocks, *rest):
    xs_parts, rest = rest[:XS_PARTS], rest[XS_PARTS:]
    wg_ref, wu_ref, wd_ref, bg_ref, bu_ref, bd_ref = rest[:6]
    rest = rest[6:]
    caches, news = rest[:N_FUSED_SHIFT], rest[N_FUSED_SHIFT:2 * N_FUSED_SHIFT]
    out_hbm = rest[2 * N_FUSED_SHIFT]
    cache_outs = rest[2 * N_FUSED_SHIFT + 1:3 * N_FUSED_SHIFT + 1]
    (xbuf, acc, wg_bf, wu_bf, wd_bf, sem_in, sem_out, sbuf, sem_shift_in,
     sem_shift_out) = rest[3 * N_FUSED_SHIFT + 1:]
    i = pl.program_id(0)
    j = pl.program_id(1)
    n_j = pl.num_programs(1)
    nblk = item_nblk[i]

    _shift_step(i * n_j + j, caches, news, cache_outs, sbuf, sem_shift_in, sem_shift_out)

    @pl.when(jnp.logical_and(i == 0, j == 0))
    def _():
        acc[0:EXPERT_BLOCK, :] = jnp.zeros((EXPERT_BLOCK, D_MODEL), F32)

        def zero_copy(c):
            dst = pl.ds(pl.multiple_of((used_blocks[0] + c) * EXPERT_BLOCK, EXPERT_BLOCK), EXPERT_BLOCK)
            return pltpu.make_async_copy(acc.at[0:EXPERT_BLOCK, :], out_hbm.at[dst, :], sem_out)

        n_slack = N_SORTED_BLOCKS - used_blocks[0]
        pl.loop(0, n_slack)(lambda c: zero_copy(c).start())
        pl.loop(0, n_slack)(lambda c: zero_copy(c).wait())

    n_items = pl.num_programs(0)
    x_slot = i % 2

    def x_copy(item, c, action):
        rows = pl.ds(pl.multiple_of(c * EXPERT_BLOCK, EXPERT_BLOCK), EXPERT_BLOCK)
        block = item_row0[item] // EXPERT_BLOCK + c
        for part, xs_hbm in enumerate(xs_parts):
            @pl.when(block // XS_PART_BLOCKS == part)
            def _(part=part, xs_hbm=xs_hbm):
                src = pl.ds(pl.multiple_of((block - part * XS_PART_BLOCKS) * EXPERT_BLOCK, EXPERT_BLOCK),
                            EXPERT_BLOCK)
                cp = pltpu.make_async_copy(xs_hbm.at[src, :], xbuf.at[item % 2, rows, :], sem_in.at[item % 2])
                getattr(cp, action)()

    def out_copy(item, c):
        rows = pl.ds(pl.multiple_of(c * EXPERT_BLOCK, EXPERT_BLOCK), EXPERT_BLOCK)
        dst = pl.ds(pl.multiple_of(item_row0[item] + c * EXPERT_BLOCK, EXPERT_BLOCK), EXPERT_BLOCK)
        return pltpu.make_async_copy(acc.at[rows, :], out_hbm.at[dst, :], sem_out)

    @pl.when(j == 0)
    def _():
        @pl.when(i == 0)
        def _():
            pl.loop(0, nblk)(lambda c: x_copy(i, c, "start"))

        pl.loop(0, nblk)(lambda c: x_copy(i, c, "wait"))
        nxt = jnp.minimum(i + 1, n_items - 1)

        @pl.when(i + 1 < n_items)
        def _():
            pl.loop(0, item_nblk[nxt])(lambda c: x_copy(nxt, c, "start"))

    @pl.when(nblk > 0)
    def _():
        wg_bf[...] = wg_ref[...].astype(BF16)
        wu_bf[...] = wu_ref[...].astype(BF16)
        wd_bf[...] = wd_ref[...].astype(BF16)

    prev = jnp.maximum(i - 1, 0)

    @pl.when(jnp.logical_and(j == 0, i > 0))
    def _():
        pl.loop(0, item_nblk[prev])(lambda c: out_copy(prev, c).wait())

    @pl.when(nblk > 0)
    def _():
        @pl.when(j == 0)
        def _():
            @pl.loop(0, nblk)
            def _(c):
                rows = pl.ds(pl.multiple_of(c * EXPERT_BLOCK, EXPERT_BLOCK), EXPERT_BLOCK)
                acc[rows, :] = jnp.broadcast_to(bd_ref[...], (EXPERT_BLOCK, D_MODEL))

        def process(start, size):
            rows = pl.ds(pl.multiple_of(start, EXPERT_BLOCK), size)
            xa, xb = _unpack_bf16_pairs(xbuf[x_slot, rows, :])
            gt = (jnp.dot(xa, wg_bf[0:HALF_MODEL, :], preferred_element_type=F32)
                  + jnp.dot(xb, wg_bf[HALF_MODEL:D_MODEL, :], preferred_element_type=F32)) + bg_ref[...]
            up = (jnp.dot(xa, wu_bf[0:HALF_MODEL, :], preferred_element_type=F32)
                  + jnp.dot(xb, wu_bf[HALF_MODEL:D_MODEL, :], preferred_element_type=F32)) + bu_ref[...]
            gt = jnp.minimum(gt, SWIGLU_LIMIT)
            up = jnp.clip(up, -SWIGLU_LIMIT, SWIGLU_LIMIT)
            act = gt * jax.nn.sigmoid(SWIGLU_ALPHA * gt) * (up + 1.0)
            acc[rows, :] += jnp.dot(act.astype(BF16), wd_bf[...], preferred_element_type=F32)

        quad = 4 * EXPERT_BLOCK

        @pl.loop(0, nblk // 4)
        def _(c):
            process(c * quad, 2 * EXPERT_BLOCK)
            process(c * quad + 2 * EXPERT_BLOCK, 2 * EXPERT_BLOCK)

        tail = (nblk // 4) * quad

        @pl.when(nblk % 4 >= 2)
        def _():
            process(tail, 2 * EXPERT_BLOCK)

        @pl.when(nblk % 2 == 1)
        def _():
            process((nblk - 1) * EXPERT_BLOCK, EXPERT_BLOCK)

    @pl.when(j == n_j - 1)
    def _():
        pl.loop(0, nblk)(lambda c: out_copy(i, c).start())

        @pl.when(i == n_items - 1)
        def _():
            pl.loop(0, nblk)(lambda c: out_copy(i, c).wait())


def _experts(item_e, item_row0, item_nblk, used_blocks, xs_parts, w_gate, w_up, w_down, b_gate, b_up, b_down,
             caches, news):
    n_j = D_FF // FF_TILE
    assert N_WORK_ITEMS * n_j > SHIFT_TOTAL_STEPS

    def jj(i, j, nblk):
        return jnp.where(nblk[i] > 0, j, n_j - 1)

    any_spec = pl.BlockSpec(memory_space=pl.ANY)
    grid_spec = pltpu.PrefetchScalarGridSpec(
        num_scalar_prefetch=4,
        grid=(N_WORK_ITEMS, n_j),
        in_specs=[any_spec] * XS_PARTS + [
            pl.BlockSpec((None, None, D_MODEL, FF_TILE), lambda i, j, e, r, n, u: (0, e[i], 0, jj(i, j, n))),
            pl.BlockSpec((None, None, D_MODEL, FF_TILE), lambda i, j, e, r, n, u: (0, e[i], 0, jj(i, j, n))),
            pl.BlockSpec((None, None, FF_TILE, D_MODEL), lambda i, j, e, r, n, u: (0, e[i], jj(i, j, n), 0)),
            pl.BlockSpec((None, 1, FF_TILE), lambda i, j, e, r, n, u: (e[i], 0, jj(i, j, n))),
            pl.BlockSpec((None, 1, FF_TILE), lambda i, j, e, r, n, u: (e[i], 0, jj(i, j, n))),
            pl.BlockSpec((None, 1, D_MODEL), lambda i, j, e, r, n, u: (e[i], 0, 0)),
        ] + [any_spec] * N_FUSED_SHIFT + [pl.BlockSpec(memory_space=pltpu.VMEM)] * N_FUSED_SHIFT,
        out_specs=[any_spec] * (1 + N_FUSED_SHIFT),
        scratch_shapes=[pltpu.VMEM((2, EXPERT_CAP, HALF_MODEL), F32),
                        pltpu.VMEM((EXPERT_CAP, D_MODEL), F32),
                        pltpu.VMEM((D_MODEL, FF_TILE), BF16),
                        pltpu.VMEM((D_MODEL, FF_TILE), BF16),
                        pltpu.VMEM((FF_TILE, D_MODEL), BF16),
                        pltpu.SemaphoreType.DMA((2,)),
                        pltpu.SemaphoreType.DMA(()),
                        pltpu.VMEM((SHIFT_SLOTS, SHIFT_CHUNK_ROWS, HEADS_PER_GROUP, HEAD_DIM), F32),
                        pltpu.SemaphoreType.DMA((SHIFT_SLOTS,)),
                        pltpu.SemaphoreType.DMA((SHIFT_SLOTS,))],
    )
    results = pl.pallas_call(
        _experts_kernel,
        out_shape=[jax.ShapeDtypeStruct((N_SORTED_ROWS, D_MODEL), F32)]
                  + [jax.ShapeDtypeStruct(c.shape, c.dtype) for c in caches],
        grid_spec=grid_spec,
        compiler_params=_params(("arbitrary", "arbitrary")),
        name="experts",
    )(item_e, item_row0, item_nblk, used_blocks, *xs_parts, w_gate, w_up, w_down, b_gate, b_up, b_down,
      *caches, *news)
    return results[0], results[1:]


def _combine_kernel(x_ref, gate_ref, o0, o1, o2, o3, y_ref):
    gates = gate_ref[...]
    y = x_ref[...]
    for k, o_ref in enumerate((o0, o1, o2, o3)):
        y = y + gates[:, k:k + 1] * o_ref[...]
    y_ref[...] = y


def _combine(x2, gates, outs):
    row = pl.BlockSpec((COMBINE_TILE, D_MODEL), lambda i: (i, 0))
    return pl.pallas_call(
        _combine_kernel,
        out_shape=jax.ShapeDtypeStruct((T_ALL, D_MODEL), F32),
        grid=(T_ALL // COMBINE_TILE,),
        in_specs=[row, pl.BlockSpec((COMBINE_TILE, LANES), lambda i: (i, 0)), row, row, row, row],
        out_specs=row,
        compiler_params=_params(("parallel",)),
        name="combine",
    )(x2, gates, *outs)


def _rope_tables():
    half = ROT_DIM // 2
    inv = ROPE_THETA ** (-2.0 * jnp.arange(half, dtype=F32) / ROT_DIM)
    pos = jnp.concatenate([jnp.arange(SEQ), jnp.full((DEC_BATCH,), PAST_LEN)]).astype(F32)
    ang = pos[:, None] * inv[None, :]
    cos, sin = jnp.cos(ang), jnp.sin(ang)
    rest = HEAD_DIM - ROT_DIM
    cos_t = jnp.concatenate([cos, cos, jnp.ones((T_ALL, rest), F32)], axis=-1)
    sin_t = jnp.concatenate([-sin, sin, jnp.zeros((T_ALL, rest), F32)], axis=-1)
    return cos_t, sin_t


def _dispatch_plan(sel, ranks, counts):
    nb = (counts + EXPERT_BLOCK - 1) // EXPERT_BLOCK
    padded = nb * EXPERT_BLOCK
    pad_start = jnp.cumsum(padded) - padded
    dest = pad_start[sel] + ranks
    n_items = (nb + EXPERT_CAP_BLOCKS - 1) // EXPERT_CAP_BLOCKS
    item_end = jnp.cumsum(n_items)
    item_start = item_end - n_items
    w = jnp.arange(N_WORK_ITEMS, dtype=I32)
    total = item_end[-1]
    w_eff = jnp.minimum(w, total - 1)
    e_w = jnp.minimum(jnp.searchsorted(item_end, w_eff, side='right'), N_EXPERTS - 1).astype(I32)
    k_w = w_eff - item_start[e_w]
    row0 = pad_start[e_w] + k_w * EXPERT_CAP
    nblk = jnp.clip(nb[e_w] - k_w * EXPERT_CAP_BLOCKS, 0, EXPERT_CAP_BLOCKS)
    nblk = jnp.where(w < total, nblk, 0)
    used_blocks = jnp.sum(nb).astype(I32).reshape(1)
    return dest, e_w, row0.astype(I32), nblk.astype(I32), used_blocks


def kernel(x_prompt, x_sample, cache_k_w128, cache_v_w128, cache_k_w512, cache_v_w512, cache_k_w2048,
           cache_v_w2048, state_conv, state_rglru, g_mix, w_in, q_norm, k_norm, w_attn_proj, conv_w, conv_b,
           w_rg_a, b_rg_a, w_rg_x, b_rg_x, lru_lambda, w_rnn_proj, w_out, g_ffn, w_router, b_router,
           w_gate, b_gate, w_up, b_up, w_down, b_down):
    caches = (cache_k_w128, cache_v_w128, cache_k_w512, cache_v_w512, cache_k_w2048, cache_v_w2048)
    x = jnp.concatenate([x_prompt[0], x_sample[:, 0]], axis=0)

    xn = _rmsnorm_bf16(x, g_mix)
    cos_t, sin_t = _rope_tables()
    gains = jnp.stack([q_norm, k_norm])
    p = _in_proj(xn, w_in, cos_t, sin_t, gains)

    attn_p = _attn_prompt(p)
    qkv_s = p[SEQ:, :3 * ATTN_WIDTH].reshape(DEC_BATCH, 3, N_HEADS, HEAD_DIM)
    q_s, k_s, v_s = qkv_s[:, 0], qkv_s[:, 1], qkv_s[:, 2]
    views = [c.reshape(DEC_BATCH, ATTN_BLOCK, DILATIONS[n // 2], HEADS_PER_GROUP, HEAD_DIM)
             for n, c in enumerate(caches)]
    attn_s = _attn_sample(q_s, k_s, v_s, views).reshape(DEC_BATCH, ATTN_OUT)
    news = []
    for g in range(N_GROUPS):
        heads = slice(g * HEADS_PER_GROUP, (g + 1) * HEADS_PER_GROUP)
        news += [k_s[:, heads], v_s[:, heads]]
    attn = jnp.concatenate([attn_p, attn_s], axis=0)

    y_p, h_p = _rglru_prompt(p, conv_w, conv_b, w_rg_a, b_rg_a, w_rg_x, b_rg_x, lru_lambda)
    hist = jnp.transpose(state_conv[0], (1, 0, 2))
    y_s, h_s = _rglru_sample(p, hist, state_rglru[0], conv_w, conv_b, w_rg_a, b_rg_a, w_rg_x, b_rg_x,
                             lru_lambda)
    y = jnp.concatenate([y_p, y_s], axis=0)

    mix = _mix(attn, y, p, w_attn_proj, w_rnn_proj)
    x2 = _out_proj(mix, w_out, x)

    w_router_pad = jnp.pad(w_router[0], ((0, 0), (0, LANES - N_EXPERTS)))
    b_router_pad = jnp.pad(b_router, ((0, 0), (0, LANES - N_EXPERTS)), constant_values=NEG_BIG)
    xn2, sel, gates, ranks, counts = _router(x2, g_ffn, w_router_pad, b_router_pad)
    dest, item_e, item_row0, item_nblk, used_blocks = _dispatch_plan(
        sel[:, :TOP_K], ranks[:, :TOP_K], counts[0, :N_EXPERTS])
    tok = jnp.repeat(jnp.arange(T_ALL, dtype=I32), TOP_K)
    src_tok = jnp.zeros((N_SORTED_ROWS,), I32).at[dest.reshape(-1)].set(tok)
    part_rows = XS_PART_BLOCKS * EXPERT_BLOCK
    xs_parts = [xn2[src_tok[k * part_rows:(k + 1) * part_rows]] for k in range(XS_PARTS)]
    n_small = len(caches) - N_FUSED_SHIFT
    small_caches = _cache_shift(caches[:n_small], news[:n_small])
    out_sorted, big_caches = _experts(item_e, item_row0, item_nblk, used_blocks, xs_parts, w_gate, w_up, w_down,
                                      b_gate.reshape(N_EXPERTS, 1, D_FF), b_up.reshape(N_EXPERTS, 1, D_FF),
                                      b_down.reshape(N_EXPERTS, 1, D_MODEL), caches[n_small:], news[n_small:])
    new_caches = (*small_caches, *big_caches)
    outs = [out_sorted[dest[:, k]] for k in range(TOP_K)]
    y_all = _combine(x2, gates, outs)

    y_prompt = y_all[:SEQ].reshape(1, SEQ, D_MODEL)
    y_sample = y_all[SEQ:].reshape(DEC_BATCH, 1, D_MODEL)
    states_p = []
    for g, w in enumerate(WINDOWS):
        keep = min(w, SEQ)
        for off in (ATTN_WIDTH, 2 * ATTN_WIDTH):
            c0 = off + g * ATTN_OUT
            states_p.append(p[SEQ - keep:SEQ, c0:c0 + ATTN_OUT].reshape(1, 1, keep, HEADS_PER_GROUP, HEAD_DIM))
    conv_p = p[SEQ - (CONV_W - 1):SEQ, 3 * ATTN_WIDTH:3 * ATTN_WIDTH + D_RNN].reshape(1, 1, CONV_W - 1, D_RNN)
    rglru_p = h_p.reshape(1, 1, D_RNN)
    rnn_x_s = p[SEQ:, 3 * ATTN_WIDTH:3 * ATTN_WIDTH + D_RNN]
    conv_s = jnp.concatenate([state_conv[0][:, 1:], rnn_x_s[:, None, :]], axis=1)[None]
    rglru_s = h_s[None]
    return (y_prompt, y_sample, *states_p, conv_p, rglru_p, *new_caches, conv_s, rglru_s)
```

```python
import functools

import jax
import jax.numpy as jnp
from jax import lax
from jax.experimental import pallas as pl
from jax.experimental.pallas import tpu as pltpu

F32 = jnp.float32
BF16 = jnp.bfloat16
I32 = jnp.int32

D_MODEL = 2048
SEQ = 8192
DEC_BATCH = 128
PAST_LEN = 2048
T_ALL = SEQ + DEC_BATCH

HEAD_DIM = 128
HEADS_PER_GROUP = 4
WINDOWS = (128, 512, 2048)
DILATIONS = (1, 4, 16)
N_GROUPS = 3
N_HEADS = N_GROUPS * HEADS_PER_GROUP
ATTN_WIDTH = N_HEADS * HEAD_DIM
ATTN_OUT = HEADS_PER_GROUP * HEAD_DIM
ATTN_BLOCK = 128
ROT_DIM = HEAD_DIM // 4
ROPE_THETA = 500000.0
D_RNN = D_MODEL
RNN_BLOCK_W = 128
CONV_W = 4
LRU_C = 8.0
N_EXPERTS = 32
TOP_K = 4
D_FF = D_MODEL
SWIGLU_LIMIT = 7.0
SWIGLU_ALPHA = 1.702
EPS = 1e-6
IN_WIDTH = 3 * ATTN_WIDTH + 2 * D_RNN + 2 * D_MODEL

LANES = 128
SUBLANES = 8
VMEM_LIMIT_BYTES = 56 * 1024 * 1024

COL_TILE = 512
COL_RNN_X = (3 * ATTN_WIDTH) // COL_TILE
COL_RNN_GATE = (3 * ATTN_WIDTH + D_RNN) // COL_TILE
COL_GATE_ATTN = (3 * ATTN_WIDTH + 2 * D_RNN) // COL_TILE
COL_GATE_RNN = (3 * ATTN_WIDTH + 2 * D_RNN + D_MODEL) // COL_TILE

ROW_TILE = 1664
MIX_ROW_TILE = 832
NORM_TILE = 640
COMBINE_TILE = 128
SPAN = 2048
NEG_BIG = -1e30

EXPERT_BLOCK = 128
EXPERT_CAP_BLOCKS = 12
EXPERT_CAP = EXPERT_BLOCK * EXPERT_CAP_BLOCKS
FF_TILE = 256
N_SLOTS = T_ALL * TOP_K
N_SORTED_ROWS = N_SLOTS + N_EXPERTS * EXPERT_BLOCK
N_SORTED_BLOCKS = N_SORTED_ROWS // EXPERT_BLOCK
N_WORK_ITEMS = N_EXPERTS + -(-N_SORTED_BLOCKS // EXPERT_CAP_BLOCKS)
XS_PARTS = 4
XS_PART_BLOCKS = N_SORTED_BLOCKS // XS_PARTS
assert XS_PART_BLOCKS * XS_PARTS == N_SORTED_BLOCKS


def _params(semantics, vmem=VMEM_LIMIT_BYTES):
    return pltpu.CompilerParams(dimension_semantics=semantics, vmem_limit_bytes=vmem)


def _rmsnorm_kernel(x_ref, g_ref, o_ref):
    x = x_ref[...]
    y = x * lax.rsqrt(jnp.mean(x * x, axis=-1, keepdims=True) + EPS)
    o_ref[...] = (y * g_ref[...]).astype(o_ref.dtype)


def _rmsnorm_bf16(x, g):
    t = x.shape[0]
    return pl.pallas_call(
        _rmsnorm_kernel,
        out_shape=jax.ShapeDtypeStruct((t, D_MODEL), BF16),
        grid=(t // NORM_TILE,),
        in_specs=[pl.BlockSpec((NORM_TILE, D_MODEL), lambda i: (i, 0)),
                  pl.BlockSpec((1, D_MODEL), lambda i: (0, 0))],
        out_specs=pl.BlockSpec((NORM_TILE, D_MODEL), lambda i: (i, 0)),
        compiler_params=_params(("parallel",)),
        name="rmsnorm_bf16",
    )(x, g)


def _in_proj_kernel(x_ref, w_ref, cos_ref, sin_ref, gain_ref, o_ref, wbf_ref):
    j = pl.program_id(0)

    @pl.when(pl.program_id(1) == 0)
    def _():
        wbf_ref[...] = w_ref[...].astype(BF16)

    o_ref[...] = jnp.dot(x_ref[...], wbf_ref[...], preferred_element_type=F32)

    @pl.when(j < 2 * ATTN_WIDTH // COL_TILE)
    def _():
        lane = lax.broadcasted_iota(I32, (ROW_TILE, HEAD_DIM), 1)
        first_half = lane < ROT_DIM // 2
        gain = gain_ref[...]
        cos = cos_ref[...]
        sin = sin_ref[...]
        for h in range(COL_TILE // HEAD_DIM):
            cols = slice(h * HEAD_DIM, (h + 1) * HEAD_DIM)
            xh = o_ref[:, cols]
            y = xh * lax.rsqrt(jnp.mean(xh * xh, axis=-1, keepdims=True) + EPS) * gain
            partner = jnp.where(first_half,
                                pltpu.roll(y, HEAD_DIM - ROT_DIM // 2, 1),
                                pltpu.roll(y, ROT_DIM // 2, 1))
            o_ref[:, cols] = y * cos + partner * sin


def _in_proj(xn, w_in, cos_t, sin_t, gains):
    n_tiles = IN_WIDTH // COL_TILE
    qk_tiles = ATTN_WIDTH // COL_TILE
    return pl.pallas_call(
        _in_proj_kernel,
        out_shape=jax.ShapeDtypeStruct((T_ALL, IN_WIDTH), F32),
        grid=(n_tiles, T_ALL // ROW_TILE),
        in_specs=[
            pl.BlockSpec((ROW_TILE, D_MODEL), lambda j, i: (i, 0)),
            pl.BlockSpec((None, D_MODEL, COL_TILE), lambda j, i: (0, 0, j)),
            pl.BlockSpec((ROW_TILE, HEAD_DIM), lambda j, i: (i, 0)),
            pl.BlockSpec((ROW_TILE, HEAD_DIM), lambda j, i: (i, 0)),
            pl.BlockSpec((None, 1, HEAD_DIM), lambda j, i: (jnp.minimum(j // qk_tiles, 1), 0, 0)),
        ],
        out_specs=pl.BlockSpec((ROW_TILE, COL_TILE), lambda j, i: (i, j)),
        scratch_shapes=[pltpu.VMEM((D_MODEL, COL_TILE), BF16)],
        compiler_params=_params(("arbitrary", "arbitrary")),
        name="in_proj",
    )(xn, w_in, cos_t, sin_t, gains)


def _dot_nt(a, b):
    return lax.dot_general(a, b, (((1,), (1,)), ((), ())), preferred_element_type=F32)


def _attn_prompt_kernel(*refs):
    ins = refs[:15]
    o_ref = refs[15]
    og_ref, lse_ref = refs[16], refs[17]
    span_idx = pl.program_id(0)
    n_sub = SPAN // ATTN_BLOCK
    shape3 = (n_sub, ATTN_BLOCK, ATTN_BLOCK)
    sub = lax.broadcasted_iota(I32, shape3, 0)
    qi = lax.broadcasted_iota(I32, shape3, 1)
    kj = lax.broadcasted_iota(I32, shape3, 2)
    mask_cur = qi >= kj
    mask_prev_band = kj >= qi
    scale = HEAD_DIM ** -0.5

    def bmm_nt(a, b):
        return lax.dot_general(a, b, (((2,), (2,)), ((0,), (0,))), preferred_element_type=F32)

    def bmm(a, b):
        return lax.dot_general(a, b, (((2,), (1,)), ((0,), (0,))), preferred_element_type=F32)

    for g, dil in enumerate(DILATIONS):
        q_ref, kc_ref, vc_ref, kp_ref, vp_ref = ins[5 * g:5 * g + 5]
        blk = ATTN_BLOCK * dil
        blocks = [(m, r) for m in range(SPAN // blk) for r in range(dil)]

        def rows_of(m, r, dil=dil, blk=blk):
            return pl.ds(m * blk + r, ATTN_BLOCK, stride=dil) if dil > 1 else pl.ds(m * blk, ATTN_BLOCK)

        q3 = jnp.stack([(q_ref[rows_of(m, r), :] * scale).astype(BF16) for m, r in blocks])
        kc = [kc_ref[rows_of(m, r), :].astype(BF16) for m, r in blocks]
        vc = [vc_ref[rows_of(m, r), :].astype(BF16) for m, r in blocks]
        kp = [kp_ref[rows_of(0, r), :].astype(BF16) if m == 0 else kc[(m - 1) * dil + r] for m, r in blocks]
        vp = [vp_ref[rows_of(0, r), :].astype(BF16) if m == 0 else vc[(m - 1) * dil + r] for m, r in blocks]
        mask_prev = jnp.logical_and(mask_prev_band, jnp.logical_or(sub >= dil, span_idx > 0))
        s_cur = jnp.where(mask_cur, bmm_nt(q3, jnp.stack(kc)), NEG_BIG)
        s_prev = jnp.where(mask_prev, bmm_nt(q3, jnp.stack(kp)), NEG_BIG)
        mx = jnp.maximum(jnp.max(s_cur, axis=-1, keepdims=True), jnp.max(s_prev, axis=-1, keepdims=True))
        p_cur = jnp.exp(s_cur - mx)
        p_prev = jnp.exp(s_prev - mx)
        den = jnp.sum(p_cur, axis=-1, keepdims=True) + jnp.sum(p_prev, axis=-1, keepdims=True)
        pv = bmm(p_cur.astype(BF16), jnp.stack(vc)) + bmm(p_prev.astype(BF16), jnp.stack(vp))
        out3 = pv / den
        lse3 = jnp.broadcast_to(mx + jnp.log(den), (n_sub, ATTN_BLOCK, HEAD_DIM))
        for n, (m, r) in enumerate(blocks):
            og_ref[g, rows_of(m, r), :] = out3[n]
            lse_ref[g, rows_of(m, r), :] = lse3[n]

    lse = lse_ref[...]
    top = jnp.max(lse, axis=0)
    w = jnp.exp(lse - top[None])
    o_ref[...] = jnp.sum(w * og_ref[...], axis=0) / jnp.sum(w, axis=0)


def _attn_prompt(p):
    q_cols, k_cols, v_cols = 0, N_HEADS, 2 * N_HEADS
    in_specs = []
    for g, dil in enumerate(DILATIONS):
        blk = ATTN_BLOCK * dil
        per_span = SPAN // blk

        def cur(off, g=g):
            return pl.BlockSpec((SPAN, HEAD_DIM), lambda s, h: (s, off + g * HEADS_PER_GROUP + h))

        def prev(off, g=g, blk=blk, per_span=per_span):
            return pl.BlockSpec((blk, HEAD_DIM),
                                lambda s, h: (jnp.maximum(s * per_span - 1, 0), off + g * HEADS_PER_GROUP + h))

        in_specs += [cur(q_cols), cur(k_cols), cur(v_cols), prev(k_cols), prev(v_cols)]
    return pl.pallas_call(
        _attn_prompt_kernel,
        out_shape=jax.ShapeDtypeStruct((SEQ, ATTN_OUT), F32),
        grid=(SEQ // SPAN, HEADS_PER_GROUP),
        in_specs=in_specs,
        out_specs=pl.BlockSpec((SPAN, HEAD_DIM), lambda s, h: (s, h)),
        scratch_shapes=[pltpu.VMEM((N_GROUPS, SPAN, HEAD_DIM), F32),
                        pltpu.VMEM((N_GROUPS, SPAN, HEAD_DIM), F32)],
        compiler_params=_params(("parallel", "parallel")),
        name="attn_prompt",
    )(*([p] * 15))


SAMPLE_BB = 8


def _attn_sample_kernel(q_ref, k_ref, v_ref, ck0, cv0, ck1, cv1, ck2, cv2, o_ref):
    caches = ((ck0, cv0), (ck1, cv1), (ck2, cv2))
    scale = HEAD_DIM ** -0.5
    for b in range(SAMPLE_BB):
        outs, lses = [], []
        for g in range(N_GROUPS):
            heads = slice(g * HEADS_PER_GROUP, (g + 1) * HEADS_PER_GROUP)
            q = q_ref[b, heads, :] * scale
            k_new = k_ref[b, heads, :]
            v_new = v_ref[b, heads, :]
            k_old = caches[g][0][b]
            v_old = caches[g][1][b]
            s_old = jnp.sum(k_old * q[None], axis=-1, keepdims=True)
            s_new = jnp.sum(k_new * q, axis=-1, keepdims=True)
            mx = jnp.maximum(jnp.max(s_old, axis=0), s_new)
            p_old = jnp.exp(s_old - mx[None])
            p_new = jnp.exp(s_new - mx)
            den = jnp.sum(p_old, axis=0) + p_new
            pv = jnp.sum(p_old * v_old, axis=0) + p_new * v_new
            outs.append(pv / den)
            lses.append(mx + jnp.log(den))
        top = jnp.maximum(jnp.maximum(lses[0], lses[1]), lses[2])
        ws = [jnp.exp(l - top) for l in lses]
        o_ref[b] = (ws[0] * outs[0] + ws[1] * outs[1] + ws[2] * outs[2]) / (ws[0] + ws[1] + ws[2])


def _attn_sample(q_s, k_s, v_s, cache_views):
    row = pl.BlockSpec((SAMPLE_BB, N_HEADS, HEAD_DIM), lambda b: (b, 0, 0))
    cache_spec = pl.BlockSpec((SAMPLE_BB, ATTN_BLOCK, None, HEADS_PER_GROUP, HEAD_DIM),
                              lambda b: (b, 0, 0, 0, 0))
    return pl.pallas_call(
        _attn_sample_kernel,
        out_shape=jax.ShapeDtypeStruct((DEC_BATCH, HEADS_PER_GROUP, HEAD_DIM), F32),
        grid=(DEC_BATCH // SAMPLE_BB,),
        in_specs=[row, row, row] + [cache_spec] * 6,
        out_specs=pl.BlockSpec((SAMPLE_BB, HEADS_PER_GROUP, HEAD_DIM), lambda b: (b, 0, 0)),
        compiler_params=_params(("parallel",)),
        name="attn_sample",
    )(q_s, k_s, v_s, *cache_views)


SHIFT_CHUNK_ROWS = 1024
SHIFT_CHUNKS_PER_STEP = 2
SHIFT_SLOTS = 2 * SHIFT_CHUNKS_PER_STEP
SHIFT_DMA_QUEUE = 1
FUSED_SHIFT_WINDOWS = (WINDOWS[2], WINDOWS[2])
N_FUSED_SHIFT = len(FUSED_SHIFT_WINDOWS)
N_ORDER_ONLY = 2
STANDALONE_SHIFT_ROWS = 2048


def _shift_steps(window):
    return DEC_BATCH * window // (SHIFT_CHUNK_ROWS * SHIFT_CHUNKS_PER_STEP)


def _shift_schedule():
    spans, lo = [], 0
    for w in FUSED_SHIFT_WINDOWS:
        spans.append((lo, lo + _shift_steps(w)))
        lo += _shift_steps(w)
    return spans


SHIFT_TOTAL_STEPS = _shift_schedule()[-1][1]


def _shift_one_cache(cache, new, out):
    w = cache.shape[2]
    bb = STANDALONE_SHIFT_ROWS // w
    n_chunks = DEC_BATCH // bb

    def run(buf, sem_in, sem_out):
        def in_copy(c, slot):
            return pltpu.make_async_copy(cache.at[0, pl.ds(c * bb, bb), pl.ds(1, w - 1)],
                                         buf.at[slot, :, pl.ds(0, w - 1)], sem_in.at[slot])

        def out_copy(c, slot):
            return pltpu.make_async_copy(buf.at[slot], out.at[0, pl.ds(c * bb, bb)], sem_out.at[slot])

        in_copy(0, 0).start()

        @pl.loop(0, n_chunks)
        def _(c):
            slot = c % 2
            other = 1 - slot

            @pl.when(c + 1 < n_chunks)
            def _():
                @pl.when(c >= 1)
                def _():
                    out_copy(c - 1, other).wait()

                in_copy(c + 1, other).start()

            in_copy(c, slot).wait()
            buf[slot, :, w - 1] = new[pl.ds(c * bb, bb)]
            out_copy(c, slot).start()

        out_copy(n_chunks - 2, n_chunks % 2).wait()
        out_copy(n_chunks - 1, (n_chunks - 1) % 2).wait()

    pl.run_scoped(run, pltpu.VMEM((2, bb, w, HEADS_PER_GROUP, HEAD_DIM), F32),
                  pltpu.SemaphoreType.DMA((2,)), pltpu.SemaphoreType.DMA((2,)))


def _cache_shift_kernel(n, *refs):
    caches, news, outs = refs[:n], refs[n:2 * n], refs[len(refs) - n:]
    for a in range(n):
        _shift_one_cache(caches[a], news[a], outs[a])


def _cache_shift(caches, news, after, name):
    n = len(caches)
    any_spec = pl.BlockSpec(memory_space=pl.ANY)
    return pl.pallas_call(
        functools.partial(_cache_shift_kernel, n),
        out_shape=[jax.ShapeDtypeStruct(c.shape, c.dtype) for c in caches],
        in_specs=[any_spec] * n + [pl.BlockSpec(memory_space=pltpu.VMEM)] * n + [any_spec],
        out_specs=[any_spec] * n,
        compiler_params=pltpu.CompilerParams(vmem_limit_bytes=VMEM_LIMIT_BYTES),
        name=name,
    )(*caches, *news, after)


def _shift_chunk_copies(cache, out, sbuf, sem_in, sem_out, local_step, p, slot):
    w = cache.shape[2]
    ins, outs, new_rows = [], [], []
    copy = functools.partial(functools.partial, pltpu.make_async_copy)
    if w <= SHIFT_CHUNK_ROWS:
        per_chunk = SHIFT_CHUNK_ROWS // w
        b0 = (local_step * SHIFT_CHUNKS_PER_STEP + p) * per_chunk
        for bl in range(per_chunk):
            ins.append(copy(cache.at[0, b0 + bl, pl.ds(1, w - 1)], sbuf.at[slot, pl.ds(bl * w, w - 1)],
                            sem_in.at[slot]))
            outs.append(copy(sbuf.at[slot, pl.ds(bl * w, w)], out.at[0, b0 + bl], sem_out.at[slot]))
            new_rows.append((bl * w + w - 1, b0 + bl))
    else:
        assert w == SHIFT_CHUNK_ROWS * SHIFT_CHUNKS_PER_STEP
        last = p == SHIFT_CHUNKS_PER_STEP - 1
        n_in = SHIFT_CHUNK_ROWS - 1 if last else SHIFT_CHUNK_ROWS
        ins.append(copy(cache.at[0, local_step, pl.ds(1 + p * SHIFT_CHUNK_ROWS, n_in)],
                        sbuf.at[slot, pl.ds(0, n_in)], sem_in.at[slot]))
        outs.append(copy(sbuf.at[slot], out.at[0, local_step, pl.ds(p * SHIFT_CHUNK_ROWS, SHIFT_CHUNK_ROWS)],
                         sem_out.at[slot]))
        if last:
            new_rows.append((SHIFT_CHUNK_ROWS - 1, local_step))
    return ins, outs, new_rows


def _shift_step(step, caches, news, outs, sbuf, sem_in, sem_out):
    spans = _shift_schedule()

    def for_step(t, fn):
        for a, (lo, hi) in enumerate(spans):
            @pl.when(jnp.logical_and(t >= lo, t < hi))
            def _(a=a, lo=lo):
                for p in range(SHIFT_CHUNKS_PER_STEP):
                    slot = (t % 2) * SHIFT_CHUNKS_PER_STEP + p
                    ins, outs_, new_rows = _shift_chunk_copies(caches[a], outs[a], sbuf, sem_in, sem_out,
                                                               t - lo, p, slot)
                    fn(a, slot, ins, outs_, new_rows)

    def drain(a, slot, ins, outs_, new_rows):
        for make in outs_:
            make().wait()

    def prefetch(a, slot, ins, outs_, new_rows):
        for make in ins:
            make().start(priority=SHIFT_DMA_QUEUE)

    def forward(a, slot, ins, outs_, new_rows):
        for make in ins:
            make().wait()
        for row, b in new_rows:
            sbuf[slot, row] = news[a][b]
        for make in outs_:
            make().start(priority=SHIFT_DMA_QUEUE)

    @pl.when(step == 0)
    def _():
        for_step(step, prefetch)

    for_step(step - 1, drain)
    for_step(step + 1, prefetch)
    for_step(step, forward)


def _rglru_gates(xc, wa_ref, ba_ref, wx_ref, bx_ref, lam_ref):
    r_parts, i_parts = [], []
    for n in range(COL_TILE // RNN_BLOCK_W):
        xb = xc[:, n * RNN_BLOCK_W:(n + 1) * RNN_BLOCK_W].astype(BF16)
        r_parts.append(jnp.dot(xb, wa_ref[n].astype(BF16), preferred_element_type=F32))
        i_parts.append(jnp.dot(xb, wx_ref[n].astype(BF16), preferred_element_type=F32))
    r = jax.nn.sigmoid(jnp.concatenate(r_parts, axis=-1) + ba_ref[...])
    i = jax.nn.sigmoid(jnp.concatenate(i_parts, axis=-1) + bx_ref[...])
    neg_lam = -lam_ref[...]
    softplus = jnp.maximum(neg_lam, 0.0) + jnp.log1p(jnp.exp(-jnp.abs(neg_lam)))
    log_a = -LRU_C * r * softplus
    a = jnp.exp(log_a)
    b = jnp.sqrt(-jnp.tanh(log_a) * (jnp.exp(2.0 * log_a) + 1.0)) * i * xc
    return a, b


RNN_T_TILE = 512
CONV_PAD = SUBLANES


def _rglru_prompt_kernel(x_ref, gate_ref, cw_ref, cb_ref, wa_ref, ba_ref, wx_ref, bx_ref, lam_ref,
                         y_ref, hlast_ref, xbuf, a_scr, b_scr, h_scr, carry):
    t = pl.program_id(1)

    @pl.when(t == 0)
    def _():
        xbuf[0:CONV_PAD, :] = jnp.zeros((CONV_PAD, COL_TILE), F32)
        carry[...] = jnp.zeros((SUBLANES, COL_TILE), F32)

    @pl.when(t > 0)
    def _():
        xbuf[0:CONV_PAD, :] = xbuf[RNN_T_TILE:RNN_T_TILE + CONV_PAD, :]

    xbuf[CONV_PAD:CONV_PAD + RNN_T_TILE, :] = x_ref[...]
    xc = cb_ref[...] + sum(
        xbuf[pl.ds(CONV_PAD - (CONV_W - 1) + j, RNN_T_TILE), :] * cw_ref[j:j + 1, :] for j in range(CONV_W))
    a, b = _rglru_gates(xc, wa_ref, ba_ref, wx_ref, bx_ref, lam_ref)
    a_scr[...] = a
    b_scr[...] = b

    row = lax.broadcasted_iota(I32, (SUBLANES, COL_TILE), 0)

    def chunk(c, h):
        rows = pl.ds(pl.multiple_of(c * SUBLANES, SUBLANES), SUBLANES)
        ac = a_scr[rows, :]
        bc = b_scr[rows, :]
        for s in (1, 2, 4):
            a_sh = jnp.where(row >= s, pltpu.roll(ac, s, 0), 1.0)
            b_sh = jnp.where(row >= s, pltpu.roll(bc, s, 0), 0.0)
            bc = ac * b_sh + bc
            ac = ac * a_sh
        hh = ac * h + bc
        h_scr[rows, :] = hh
        return jnp.broadcast_to(hh[SUBLANES - 1:SUBLANES, :], (SUBLANES, COL_TILE))

    h_end = lax.fori_loop(0, RNN_T_TILE // SUBLANES, chunk, carry[...], unroll=4)
    carry[...] = h_end
    hlast_ref[...] = h_end[0:1, :]
    y_ref[...] = (h_scr[...] * jax.nn.gelu(gate_ref[...])).astype(y_ref.dtype)


def _rnn_param_specs(idx):
    return [
        pl.BlockSpec((None, CONV_W, COL_TILE), lambda *g: (0, 0, idx(*g))),
        pl.BlockSpec((1, COL_TILE), lambda *g: (0, idx(*g))),
        pl.BlockSpec((None, COL_TILE // RNN_BLOCK_W, RNN_BLOCK_W, RNN_BLOCK_W), lambda *g: (0, idx(*g), 0, 0)),
        pl.BlockSpec((1, COL_TILE), lambda *g: (0, idx(*g))),
        pl.BlockSpec((None, COL_TILE // RNN_BLOCK_W, RNN_BLOCK_W, RNN_BLOCK_W), lambda *g: (0, idx(*g), 0, 0)),
        pl.BlockSpec((1, COL_TILE), lambda *g: (0, idx(*g))),
        pl.BlockSpec((1, COL_TILE), lambda *g: (0, idx(*g))),
    ]


def _rglru_prompt(p, conv_w, conv_b, w_rg_a, b_rg_a, w_rg_x, b_rg_x, lru_lambda):
    n_c = D_RNN // COL_TILE
    return pl.pallas_call(
        _rglru_prompt_kernel,
        out_shape=[jax.ShapeDtypeStruct((SEQ, D_RNN), BF16), jax.ShapeDtypeStruct((1, D_RNN), F32)],
        grid=(n_c, SEQ // RNN_T_TILE),
        in_specs=[pl.BlockSpec((RNN_T_TILE, COL_TILE), lambda c, t: (t, COL_RNN_X + c)),
                  pl.BlockSpec((RNN_T_TILE, COL_TILE), lambda c, t: (t, COL_RNN_GATE + c))]
                 + _rnn_param_specs(lambda c, t: c),
        out_specs=[pl.BlockSpec((RNN_T_TILE, COL_TILE), lambda c, t: (t, c)),
                   pl.BlockSpec((1, COL_TILE), lambda c, t: (0, c))],
        scratch_shapes=[pltpu.VMEM((CONV_PAD + RNN_T_TILE, COL_TILE), F32),
                        pltpu.VMEM((RNN_T_TILE, COL_TILE), F32),
                        pltpu.VMEM((RNN_T_TILE, COL_TILE), F32),
                        pltpu.VMEM((RNN_T_TILE, COL_TILE), F32),
                        pltpu.VMEM((SUBLANES, COL_TILE), F32)],
        compiler_params=_params(("parallel", "arbitrary")),
        name="rglru_prompt",
    )(p, p, conv_w, conv_b, w_rg_a, b_rg_a, w_rg_x, b_rg_x, lru_lambda)


def _rglru_sample_kernel(x_ref, gate_ref, hist_ref, h0_ref, cw_ref, cb_ref, wa_ref, ba_ref, wx_ref, bx_ref,
                         lam_ref, y_ref, h_ref):
    xc = cb_ref[...] + x_ref[...] * cw_ref[CONV_W - 1:CONV_W, :]
    for j in range(CONV_W - 1):
        xc = xc + hist_ref[j] * cw_ref[j:j + 1, :]
    a, b = _rglru_gates(xc, wa_ref, ba_ref, wx_ref, bx_ref, lam_ref)
    h = a * h0_ref[...] + b
    h_ref[...] = h
    y_ref[...] = (h * jax.nn.gelu(gate_ref[...])).astype(y_ref.dtype)


def _rglru_sample(p, hist, h0, conv_w, conv_b, w_rg_a, b_rg_a, w_rg_x, b_rg_x, lru_lambda):
    n_c = D_RNN // COL_TILE
    row_blk = SEQ // DEC_BATCH
    return pl.pallas_call(
        _rglru_sample_kernel,
        out_shape=[jax.ShapeDtypeStruct((DEC_BATCH, D_RNN), BF16), jax.ShapeDtypeStruct((DEC_BATCH, D_RNN), F32)],
        grid=(n_c,),
        in_specs=[pl.BlockSpec((DEC_BATCH, COL_TILE), lambda c: (row_blk, COL_RNN_X + c)),
                  pl.BlockSpec((DEC_BATCH, COL_TILE), lambda c: (row_blk, COL_RNN_GATE + c)),
                  pl.BlockSpec((CONV_W - 1, DEC_BATCH, COL_TILE), lambda c: (0, 0, c)),
                  pl.BlockSpec((DEC_BATCH, COL_TILE), lambda c: (0, c))]
                 + _rnn_param_specs(lambda c: c),
        out_specs=[pl.BlockSpec((DEC_BATCH, COL_TILE), lambda c: (0, c)),
                   pl.BlockSpec((DEC_BATCH, COL_TILE), lambda c: (0, c))],
        compiler_params=_params(("parallel",)),
        name="rglru_sample",
    )(p, p, hist, h0, conv_w, conv_b, w_rg_a, b_rg_a, w_rg_x, b_rg_x, lru_lambda)


def _mix_kernel(attn_ref, y_ref, ga_ref, gr_ref, wap_ref, wrp_ref, o_ref, wap_bf, wrp_bf):
    @pl.when(pl.program_id(1) == 0)
    def _():
        wap_bf[...] = wap_ref[...].astype(BF16)
        wrp_bf[...] = wrp_ref[...].astype(BF16)

    attn_d = jnp.dot(attn_ref[...].astype(BF16), wap_bf[...], preferred_element_type=F32)
    rnn_d = jnp.dot(y_ref[...], wrp_bf[...], preferred_element_type=F32)
    mix = jax.nn.sigmoid(ga_ref[...]) * attn_d + jax.nn.sigmoid(gr_ref[...]) * rnn_d
    o_ref[...] = mix.astype(o_ref.dtype)


def _mix(attn, y, p, w_attn_proj, w_rnn_proj):
    return pl.pallas_call(
        _mix_kernel,
        out_shape=jax.ShapeDtypeStruct((T_ALL, D_MODEL), BF16),
        grid=(D_MODEL // COL_TILE, T_ALL // MIX_ROW_TILE),
        in_specs=[pl.BlockSpec((MIX_ROW_TILE, ATTN_OUT), lambda c, i: (i, 0)),
                  pl.BlockSpec((MIX_ROW_TILE, D_RNN), lambda c, i: (i, 0)),
                  pl.BlockSpec((MIX_ROW_TILE, COL_TILE), lambda c, i: (i, COL_GATE_ATTN + c)),
                  pl.BlockSpec((MIX_ROW_TILE, COL_TILE), lambda c, i: (i, COL_GATE_RNN + c)),
                  pl.BlockSpec((None, ATTN_OUT, COL_TILE), lambda c, i: (0, 0, c)),
                  pl.BlockSpec((None, D_RNN, COL_TILE), lambda c, i: (0, 0, c))],
        out_specs=pl.BlockSpec((MIX_ROW_TILE, COL_TILE), lambda c, i: (i, c)),
        scratch_shapes=[pltpu.VMEM((ATTN_OUT, COL_TILE), BF16), pltpu.VMEM((D_RNN, COL_TILE), BF16)],
        compiler_params=_params(("arbitrary", "arbitrary")),
        name="mix",
    )(attn, y, p, p, w_attn_proj, w_rnn_proj)


def _out_proj_kernel(mix_ref, w_ref, x_ref, o_ref, wbf):
    @pl.when(pl.program_id(1) == 0)
    def _():
        wbf[...] = w_ref[...].astype(BF16)

    o_ref[...] = x_ref[...] + jnp.dot(mix_ref[...], wbf[...], preferred_element_type=F32)


def _out_proj(mix, w_out, x):
    return pl.pallas_call(
        _out_proj_kernel,
        out_shape=jax.ShapeDtypeStruct((T_ALL, D_MODEL), F32),
        grid=(D_MODEL // COL_TILE, T_ALL // ROW_TILE),
        in_specs=[pl.BlockSpec((ROW_TILE, D_MODEL), lambda n, i: (i, 0)),
                  pl.BlockSpec((None, D_MODEL, COL_TILE), lambda n, i: (0, 0, n)),
                  pl.BlockSpec((ROW_TILE, COL_TILE), lambda n, i: (i, n))],
        out_specs=pl.BlockSpec((ROW_TILE, COL_TILE), lambda n, i: (i, n)),
        scratch_shapes=[pltpu.VMEM((D_MODEL, COL_TILE), BF16)],
        compiler_params=_params(("arbitrary", "arbitrary")),
        name="out_proj",
    )(mix, w_out, x)


HALF_MODEL = D_MODEL // 2
HIGH_HALF_MASK = -65536


def _pack_bf16_pairs(x):
    hi = pltpu.bitcast(x[:, :HALF_MODEL].astype(BF16).astype(F32), I32)
    lo = pltpu.bitcast(x[:, HALF_MODEL:].astype(BF16).astype(F32), I32)
    return pltpu.bitcast(hi | lax.shift_right_logical(lo, 16), F32)


def _unpack_bf16_pairs(words):
    bits = pltpu.bitcast(words, I32)
    first = pltpu.bitcast(bits & HIGH_HALF_MASK, F32).astype(BF16)
    second = pltpu.bitcast(lax.shift_left(bits, 16), F32).astype(BF16)
    return first, second


def _split_bf16(x):
    hi = x.astype(BF16)
    lo = (x - hi.astype(F32)).astype(BF16)
    return hi, lo


def _router_kernel(x_ref, g_ref, w_ref, b_ref, xn_ref, sel_ref, gate_ref, rank_ref, cnt_ref, carry, tri):
    step = pl.program_id(0)

    @pl.when(step == 0)
    def _():
        carry[...] = jnp.zeros((1, LANES), F32)
        ri = lax.broadcasted_iota(I32, (NORM_TILE, NORM_TILE), 0)
        ci = lax.broadcasted_iota(I32, (NORM_TILE, NORM_TILE), 1)
        tri[...] = jnp.where(ci < ri, 1.0, 0.0).astype(BF16)

    x = x_ref[...]
    xn = x * lax.rsqrt(jnp.mean(x * x, axis=-1, keepdims=True) + EPS) * g_ref[...]
    xn_ref[...] = _pack_bf16_pairs(xn)

    x_hi, x_lo = _split_bf16(xn)
    w_hi, w_lo = _split_bf16(w_ref[...])
    logits = (jnp.dot(x_hi, w_hi, preferred_element_type=F32)
              + jnp.dot(x_hi, w_lo, preferred_element_type=F32)
              + jnp.dot(x_lo, w_hi, preferred_element_type=F32)) + b_ref[...]

    lane = lax.broadcasted_iota(I32, (NORM_TILE, LANES), 1)
    work = logits
    vals, idxs = [], []
    for _ in range(TOP_K):
        mk = jnp.max(work, axis=-1, keepdims=True)
        ik = jnp.min(jnp.where(work == mk, lane, LANES), axis=-1, keepdims=True)
        vals.append(mk)
        idxs.append(ik)
        work = jnp.where(lane == ik, -jnp.inf, work)
    exps = [jnp.exp(v - vals[0]) for v in vals]
    den = exps[0] + exps[1] + exps[2] + exps[3]

    member = jnp.zeros((NORM_TILE, LANES), F32)
    for ik in idxs:
        member = member + jnp.where(lane == ik, 1.0, 0.0)
    before = jnp.dot(tri[...], member.astype(BF16), preferred_element_type=F32) + carry[...]
    sel = jnp.zeros((NORM_TILE, LANES), I32)
    gates = jnp.zeros((NORM_TILE, LANES), F32)
    ranks = jnp.zeros((NORM_TILE, LANES), F32)
    for k in range(TOP_K):
        rk = jnp.sum(jnp.where(lane == idxs[k], before, 0.0), axis=-1, keepdims=True)
        sel = jnp.where(lane == k, idxs[k], sel)
        gates = jnp.where(lane == k, exps[k] / den, gates)
        ranks = jnp.where(lane == k, rk, ranks)
    sel_ref[...] = sel
    gate_ref[...] = gates
    rank_ref[...] = ranks.astype(I32)
    carry[...] = carry[...] + jnp.sum(member, axis=0, keepdims=True)
    cnt_ref[...] = carry[...].astype(I32)


def _router(x2, g_ffn, w_router_pad, b_router_pad):
    tile_spec = pl.BlockSpec((NORM_TILE, LANES), lambda i: (i, 0))
    return pl.pallas_call(
        _router_kernel,
        out_shape=[jax.ShapeDtypeStruct((T_ALL, HALF_MODEL), F32),
                   jax.ShapeDtypeStruct((T_ALL, LANES), I32),
                   jax.ShapeDtypeStruct((T_ALL, LANES), F32),
                   jax.ShapeDtypeStruct((T_ALL, LANES), I32),
                   jax.ShapeDtypeStruct((1, LANES), I32)],
        grid=(T_ALL // NORM_TILE,),
        in_specs=[pl.BlockSpec((NORM_TILE, D_MODEL), lambda i: (i, 0)),
                  pl.BlockSpec((1, D_MODEL), lambda i: (0, 0)),
                  pl.BlockSpec((D_MODEL, LANES), lambda i: (0, 0)),
                  pl.BlockSpec((1, LANES), lambda i: (0, 0))],
        out_specs=[pl.BlockSpec((NORM_TILE, HALF_MODEL), lambda i: (i, 0)),
                   tile_spec, tile_spec, tile_spec,
                   pl.BlockSpec((1, LANES), lambda i: (0, 0))],
        scratch_shapes=[pltpu.VMEM((1, LANES), F32), pltpu.VMEM((NORM_TILE, NORM_TILE), BF16)],
        compiler_params=_params(("arbitrary",)),
        name="router",
    )(x2, g_ffn, w_router_pad, b_router_pad)


def _experts_kernel(item_e, item_row0, item_nblk, used_blocks, *rest):
    xs_parts, rest = rest[:XS_PARTS], rest[XS_PARTS:]
    wg_ref, wu_ref, wd_ref, bg_ref, bu_ref, bd_ref = rest[:6]
    rest = rest[6:]
    caches, news = rest[:N_FUSED_SHIFT], rest[N_FUSED_SHIFT:2 * N_FUSED_SHIFT]
    rest = rest[2 * N_FUSED_SHIFT + N_ORDER_ONLY:]
    out_hbm = rest[0]
    cache_outs = rest[1:N_FUSED_SHIFT + 1]
    (xbuf, acc, wg_bf, wu_bf, wd_bf, sem_in, sem_out, sbuf, sem_shift_in,
     sem_shift_out) = rest[N_FUSED_SHIFT + 1:]
    i = pl.program_id(0)
    j = pl.program_id(1)
    n_j = pl.num_programs(1)
    nblk = item_nblk[i]

    _shift_step(i * n_j + j, caches, news, cache_outs, sbuf, sem_shift_in, sem_shift_out)

    @pl.when(jnp.logical_and(i == 0, j == 0))
    def _():
        acc[0:EXPERT_BLOCK, :] = jnp.zeros((EXPERT_BLOCK, D_MODEL), F32)

        def zero_copy(c):
            dst = pl.ds(pl.multiple_of((used_blocks[0] + c) * EXPERT_BLOCK, EXPERT_BLOCK), EXPERT_BLOCK)
            return pltpu.make_async_copy(acc.at[0:EXPERT_BLOCK, :], out_hbm.at[dst, :], sem_out)

        n_slack = N_SORTED_BLOCKS - used_blocks[0]
        pl.loop(0, n_slack)(lambda c: zero_copy(c).start())
        pl.loop(0, n_slack)(lambda c: zero_copy(c).wait())

    n_items = pl.num_programs(0)
    x_slot = i % 2

    def x_copy(item, c, action):
        rows = pl.ds(pl.multiple_of(c * EXPERT_BLOCK, EXPERT_BLOCK), EXPERT_BLOCK)
        block = item_row0[item] // EXPERT_BLOCK + c
        for part, xs_hbm in enumerate(xs_parts):
            @pl.when(block // XS_PART_BLOCKS == part)
            def _(part=part, xs_hbm=xs_hbm):
                src = pl.ds(pl.multiple_of((block - part * XS_PART_BLOCKS) * EXPERT_BLOCK, EXPERT_BLOCK),
                            EXPERT_BLOCK)
                cp = pltpu.make_async_copy(xs_hbm.at[src, :], xbuf.at[item % 2, rows, :], sem_in.at[item % 2])
                getattr(cp, action)()

    def out_copy(item, c):
        rows = pl.ds(pl.multiple_of(c * EXPERT_BLOCK, EXPERT_BLOCK), EXPERT_BLOCK)
        dst = pl.ds(pl.multiple_of(item_row0[item] + c * EXPERT_BLOCK, EXPERT_BLOCK), EXPERT_BLOCK)
        return pltpu.make_async_copy(acc.at[rows, :], out_hbm.at[dst, :], sem_out)

    @pl.when(j == 0)
    def _():
        @pl.when(i == 0)
        def _():
            pl.loop(0, nblk)(lambda c: x_copy(i, c, "start"))

        pl.loop(0, nblk)(lambda c: x_copy(i, c, "wait"))
        nxt = jnp.minimum(i + 1, n_items - 1)

        @pl.when(i + 1 < n_items)
        def _():
            pl.loop(0, item_nblk[nxt])(lambda c: x_copy(nxt, c, "start"))

    @pl.when(nblk > 0)
    def _():
        wg_bf[...] = wg_ref[...].astype(BF16)
        wu_bf[...] = wu_ref[...].astype(BF16)
        wd_bf[...] = wd_ref[...].astype(BF16)

    prev = jnp.maximum(i - 1, 0)

    @pl.when(jnp.logical_and(j == 0, i > 0))
    def _():
        pl.loop(0, item_nblk[prev])(lambda c: out_copy(prev, c).wait())

    @pl.when(nblk > 0)
    def _():
        @pl.when(j == 0)
        def _():
            @pl.loop(0, nblk)
            def _(c):
                rows = pl.ds(pl.multiple_of(c * EXPERT_BLOCK, EXPERT_BLOCK), EXPERT_BLOCK)
                acc[rows, :] = jnp.broadcast_to(bd_ref[...], (EXPERT_BLOCK, D_MODEL))

        def process(start, size):
            rows = pl.ds(pl.multiple_of(start, EXPERT_BLOCK), size)
            xa, xb = _unpack_bf16_pairs(xbuf[x_slot, rows, :])
            gt = (jnp.dot(xa, wg_bf[0:HALF_MODEL, :], preferred_element_type=F32)
                  + jnp.dot(xb, wg_bf[HALF_MODEL:D_MODEL, :], preferred_element_type=F32)) + bg_ref[...]
            up = (jnp.dot(xa, wu_bf[0:HALF_MODEL, :], preferred_element_type=F32)
                  + jnp.dot(xb, wu_bf[HALF_MODEL:D_MODEL, :], preferred_element_type=F32)) + bu_ref[...]
            gt = jnp.minimum(gt, SWIGLU_LIMIT)
            up = jnp.clip(up, -SWIGLU_LIMIT, SWIGLU_LIMIT)
            act = gt * jax.nn.sigmoid(SWIGLU_ALPHA * gt) * (up + 1.0)
            acc[rows, :] += jnp.dot(act.astype(BF16), wd_bf[...], preferred_element_type=F32)

        quad = 4 * EXPERT_BLOCK

        @pl.loop(0, nblk // 4)
        def _(c):
            process(c * quad, 2 * EXPERT_BLOCK)
            process(c * quad + 2 * EXPERT_BLOCK, 2 * EXPERT_BLOCK)

        tail = (nblk // 4) * quad

        @pl.when(nblk % 4 >= 2)
        def _():
            process(tail, 2 * EXPERT_BLOCK)

        @pl.when(nblk % 2 == 1)
        def _():
            process((nblk - 1) * EXPERT_BLOCK, EXPERT_BLOCK)

    @pl.when(j == n_j - 1)
    def _():
        pl.loop(0, nblk)(lambda c: out_copy(i, c).start())

        @pl.when(i == n_items - 1)
        def _():
            pl.loop(0, nblk)(lambda c: out_copy(i, c).wait())


def _experts(item_e, item_row0, item_nblk, used_blocks, xs_parts, w_gate, w_up, w_down, b_gate, b_up, b_down,
             caches, news, run_after):
    assert len(run_after) == N_ORDER_ONLY
    n_j = D_FF // FF_TILE
    assert N_WORK_ITEMS * n_j > SHIFT_TOTAL_STEPS

    def jj(i, j, nblk):
        return jnp.where(nblk[i] > 0, j, n_j - 1)

    any_spec = pl.BlockSpec(memory_space=pl.ANY)
    grid_spec = pltpu.PrefetchScalarGridSpec(
        num_scalar_prefetch=4,
        grid=(N_WORK_ITEMS, n_j),
        in_specs=[any_spec] * XS_PARTS + [
            pl.BlockSpec((None, None, D_MODEL, FF_TILE), lambda i, j, e, r, n, u: (0, e[i], 0, jj(i, j, n))),
            pl.BlockSpec((None, None, D_MODEL, FF_TILE), lambda i, j, e, r, n, u: (0, e[i], 0, jj(i, j, n))),
            pl.BlockSpec((None, None, FF_TILE, D_MODEL), lambda i, j, e, r, n, u: (0, e[i], jj(i, j, n), 0)),
            pl.BlockSpec((None, 1, FF_TILE), lambda i, j, e, r, n, u: (e[i], 0, jj(i, j, n))),
            pl.BlockSpec((None, 1, FF_TILE), lambda i, j, e, r, n, u: (e[i], 0, jj(i, j, n))),
            pl.BlockSpec((None, 1, D_MODEL), lambda i, j, e, r, n, u: (e[i], 0, 0)),
        ] + [any_spec] * N_FUSED_SHIFT + [pl.BlockSpec(memory_space=pltpu.VMEM)] * N_FUSED_SHIFT
          + [any_spec] * N_ORDER_ONLY,
        out_specs=[any_spec] * (1 + N_FUSED_SHIFT),
        scratch_shapes=[pltpu.VMEM((2, EXPERT_CAP, HALF_MODEL), F32),
                        pltpu.VMEM((EXPERT_CAP, D_MODEL), F32),
                        pltpu.VMEM((D_MODEL, FF_TILE), BF16),
                        pltpu.VMEM((D_MODEL, FF_TILE), BF16),
                        pltpu.VMEM((FF_TILE, D_MODEL), BF16),
                        pltpu.SemaphoreType.DMA((2,)),
                        pltpu.SemaphoreType.DMA(()),
                        pltpu.VMEM((SHIFT_SLOTS, SHIFT_CHUNK_ROWS, HEADS_PER_GROUP, HEAD_DIM), F32),
                        pltpu.SemaphoreType.DMA((SHIFT_SLOTS,)),
                        pltpu.SemaphoreType.DMA((SHIFT_SLOTS,))],
    )
    results = pl.pallas_call(
        _experts_kernel,
        out_shape=[jax.ShapeDtypeStruct((N_SORTED_ROWS, D_MODEL), F32)]
                  + [jax.ShapeDtypeStruct(c.shape, c.dtype) for c in caches],
        grid_spec=grid_spec,
        compiler_params=_params(("arbitrary", "arbitrary")),
        name="experts",
    )(item_e, item_row0, item_nblk, used_blocks, *xs_parts, w_gate, w_up, w_down, b_gate, b_up, b_down,
      *caches, *news, *run_after)
    return results[0], results[1:]


def _combine_kernel(x_ref, gate_ref, o0, o1, o2, o3, *rest):
    yp_ref, ys_ref = rest[N_ORDER_ONLY:]
    gates = gate_ref[...]
    y = x_ref[...]
    for k, o_ref in enumerate((o0, o1, o2, o3)):
        y = y + gates[:, k:k + 1] * o_ref[...]
    is_prompt = pl.program_id(0) < SEQ // COMBINE_TILE

    @pl.when(is_prompt)
    def _():
        yp_ref[...] = y

    @pl.when(jnp.logical_not(is_prompt))
    def _():
        ys_ref[...] = y


def _combine(x2, gates, outs, run_after):
    assert DEC_BATCH == COMBINE_TILE and len(run_after) == N_ORDER_ONLY
    n_prompt = SEQ // COMBINE_TILE
    row = pl.BlockSpec((COMBINE_TILE, D_MODEL), lambda i: (i, 0))
    return pl.pallas_call(
        _combine_kernel,
        out_shape=[jax.ShapeDtypeStruct((SEQ, D_MODEL), F32), jax.ShapeDtypeStruct((DEC_BATCH, D_MODEL), F32)],
        grid=(n_prompt + 1,),
        in_specs=[row, pl.BlockSpec((COMBINE_TILE, LANES), lambda i: (i, 0)), row, row, row, row]
                 + [pl.BlockSpec(memory_space=pl.ANY)] * N_ORDER_ONLY,
        out_specs=[pl.BlockSpec((COMBINE_TILE, D_MODEL), lambda i: (jnp.minimum(i, n_prompt - 1), 0)),
                   pl.BlockSpec((COMBINE_TILE, D_MODEL), lambda i: (0, 0))],
        compiler_params=_params(("arbitrary",)),
        name="combine",
    )(x2, gates, *outs, *run_after)


def _rope_tables():
    half = ROT_DIM // 2
    inv = ROPE_THETA ** (-2.0 * jnp.arange(half, dtype=F32) / ROT_DIM)
    pos = jnp.concatenate([jnp.arange(SEQ), jnp.full((DEC_BATCH,), PAST_LEN)]).astype(F32)
    ang = pos[:, None] * inv[None, :]
    cos, sin = jnp.cos(ang), jnp.sin(ang)
    rest = HEAD_DIM - ROT_DIM
    cos_t = jnp.concatenate([cos, cos, jnp.ones((T_ALL, rest), F32)], axis=-1)
    sin_t = jnp.concatenate([-sin, sin, jnp.zeros((T_ALL, rest), F32)], axis=-1)
    return cos_t, sin_t


def _dispatch_plan(sel, ranks, counts):
    nb = (counts + EXPERT_BLOCK - 1) // EXPERT_BLOCK
    padded = nb * EXPERT_BLOCK
    pad_start = jnp.cumsum(padded) - padded
    dest = pad_start[sel] + ranks
    n_items = (nb + EXPERT_CAP_BLOCKS - 1) // EXPERT_CAP_BLOCKS
    item_end = jnp.cumsum(n_items)
    item_start = item_end - n_items
    w = jnp.arange(N_WORK_ITEMS, dtype=I32)
    total = item_end[-1]
    w_eff = jnp.minimum(w, total - 1)
    e_w = jnp.minimum(jnp.searchsorted(item_end, w_eff, side='right'), N_EXPERTS - 1).astype(I32)
    k_w = w_eff - item_start[e_w]
    row0 = pad_start[e_w] + k_w * EXPERT_CAP
    nblk = jnp.clip(nb[e_w] - k_w * EXPERT_CAP_BLOCKS, 0, EXPERT_CAP_BLOCKS)
    nblk = jnp.where(w < total, nblk, 0)
    used_blocks = jnp.sum(nb).astype(I32).reshape(1)
    return dest, e_w, row0.astype(I32), nblk.astype(I32), used_blocks


def kernel(x_prompt, x_sample, cache_k_w128, cache_v_w128, cache_k_w512, cache_v_w512, cache_k_w2048,
           cache_v_w2048, state_conv, state_rglru, g_mix, w_in, q_norm, k_norm, w_attn_proj, conv_w, conv_b,
           w_rg_a, b_rg_a, w_rg_x, b_rg_x, lru_lambda, w_rnn_proj, w_out, g_ffn, w_router, b_router,
           w_gate, b_gate, w_up, b_up, w_down, b_down):
    caches = (cache_k_w128, cache_v_w128, cache_k_w512, cache_v_w512, cache_k_w2048, cache_v_w2048)
    x = jnp.concatenate([x_prompt[0], x_sample[:, 0]], axis=0)

    xn = _rmsnorm_bf16(x, g_mix)
    cos_t, sin_t = _rope_tables()
    gains = jnp.stack([q_norm, k_norm])
    p = _in_proj(xn, w_in, cos_t, sin_t, gains)

    attn_p = _attn_prompt(p)
    qkv_s = p[SEQ:, :3 * ATTN_WIDTH].reshape(DEC_BATCH, 3, N_HEADS, HEAD_DIM)
    q_s, k_s, v_s = qkv_s[:, 0], qkv_s[:, 1], qkv_s[:, 2]
    views = [c.reshape(DEC_BATCH, ATTN_BLOCK, DILATIONS[n // 2], HEADS_PER_GROUP, HEAD_DIM)
             for n, c in enumerate(caches)]
    attn_s = _attn_sample(q_s, k_s, v_s, views).reshape(DEC_BATCH, ATTN_OUT)
    news = []
    for g in range(N_GROUPS):
        heads = slice(g * HEADS_PER_GROUP, (g + 1) * HEADS_PER_GROUP)
        news += [k_s[:, heads], v_s[:, heads]]
    attn = jnp.concatenate([attn_p, attn_s], axis=0)

    y_p, h_p = _rglru_prompt(p, conv_w, conv_b, w_rg_a, b_rg_a, w_rg_x, b_rg_x, lru_lambda)
    hist = jnp.transpose(state_conv[0], (1, 0, 2))
    y_s, h_s = _rglru_sample(p, hist, state_rglru[0], conv_w, conv_b, w_rg_a, b_rg_a, w_rg_x, b_rg_x,
                             lru_lambda)
    y = jnp.concatenate([y_p, y_s], axis=0)

    mix = _mix(attn, y, p, w_attn_proj, w_rnn_proj)
    x2 = _out_proj(mix, w_out, x)

    w_router_pad = jnp.pad(w_router[0], ((0, 0), (0, LANES - N_EXPERTS)))
    b_router_pad = jnp.pad(b_router, ((0, 0), (0, LANES - N_EXPERTS)), constant_values=NEG_BIG)
    xn2, sel, gates, ranks, counts = _router(x2, g_ffn, w_router_pad, b_router_pad)
    dest, item_e, item_row0, item_nblk, used_blocks = _dispatch_plan(
        sel[:, :TOP_K], ranks[:, :TOP_K], counts[0, :N_EXPERTS])
    tok = jnp.repeat(jnp.arange(T_ALL, dtype=I32), TOP_K)
    src_tok = jnp.zeros((N_SORTED_ROWS,), I32).at[dest.reshape(-1)].set(tok)
    part_rows = XS_PART_BLOCKS * EXPERT_BLOCK
    xs_parts = [xn2[src_tok[k * part_rows:(k + 1) * part_rows]] for k in range(XS_PARTS)]
    mid_caches = _cache_shift(caches[2:4], news[2:4], counts, "cache_shift_mid")
    out_sorted, big_caches = _experts(item_e, item_row0, item_nblk, used_blocks, xs_parts, w_gate, w_up, w_down,
                                      b_gate.reshape(N_EXPERTS, 1, D_FF), b_up.reshape(N_EXPERTS, 1, D_FF),
                                      b_down.reshape(N_EXPERTS, 1, D_MODEL), caches[4:], news[4:], mid_caches)
    small_caches = _cache_shift(caches[:2], news[:2], out_sorted, "cache_shift_small")
    new_caches = (*small_caches, *mid_caches, *big_caches)
    outs = [out_sorted[dest[:, k]] for k in range(TOP_K)]
    y_p, y_s = _combine(x2, gates, outs, small_caches)

    y_prompt = y_p.reshape(1, SEQ, D_MODEL)
    y_sample = y_s.reshape(DEC_BATCH, 1, D_MODEL)
    states_p = []
    for g, w in enumerate(WINDOWS):
        keep = min(w, SEQ)
        for off in (ATTN_WIDTH, 2 * ATTN_WIDTH):
            c0 = off + g * ATTN_OUT
            states_p.append(p[SEQ - keep:SEQ, c0:c0 + ATTN_OUT].reshape(1, 1, keep, HEADS_PER_GROUP, HEAD_DIM))
    conv_p = p[SEQ - (CONV_W - 1):SEQ, 3 * ATTN_WIDTH:3 * ATTN_WIDTH + D_RNN].reshape(1, 1, CONV_W - 1, D_RNN)
    rglru_p = h_p.reshape(1, 1, D_RNN)
    rnn_x_s = p[SEQ:, 3 * ATTN_WIDTH:3 * ATTN_WIDTH + D_RNN]
    conv_s = jnp.concatenate([state_conv[0][:, 1:], rnn_x_s[:, None, :]], axis=1)[None]
    rglru_s = h_s[None]
    return (y_prompt, y_sample, *states_p, conv_p, rglru_p, *new_caches, conv_s, rglru_s)
```

```python
import functools

import jax
import jax.numpy as jnp
from jax import lax
from jax.experimental import pallas as pl
from jax.experimental.pallas import tpu as pltpu

F32 = jnp.float32
BF16 = jnp.bfloat16
I32 = jnp.int32

D_MODEL = 2048
SEQ = 8192
DEC_BATCH = 128
PAST_LEN = 2048
T_ALL = SEQ + DEC_BATCH

HEAD_DIM = 128
HEADS_PER_GROUP = 4
WINDOWS = (128, 512, 2048)
DILATIONS = (1, 4, 16)
N_GROUPS = 3
N_HEADS = N_GROUPS * HEADS_PER_GROUP
ATTN_WIDTH = N_HEADS * HEAD_DIM
ATTN_OUT = HEADS_PER_GROUP * HEAD_DIM
ATTN_BLOCK = 128
ROT_DIM = HEAD_DIM // 4
ROPE_THETA = 500000.0
D_RNN = D_MODEL
RNN_BLOCK_W = 128
CONV_W = 4
LRU_C = 8.0
N_EXPERTS = 32
TOP_K = 4
D_FF = D_MODEL
SWIGLU_LIMIT = 7.0
SWIGLU_ALPHA = 1.702
EPS = 1e-6
IN_WIDTH = 3 * ATTN_WIDTH + 2 * D_RNN + 2 * D_MODEL

LANES = 128
SUBLANES = 8
VMEM_LIMIT_BYTES = 56 * 1024 * 1024

COL_TILE = 512
COL_RNN_X = (3 * ATTN_WIDTH) // COL_TILE
COL_RNN_GATE = (3 * ATTN_WIDTH + D_RNN) // COL_TILE
COL_GATE_ATTN = (3 * ATTN_WIDTH + 2 * D_RNN) // COL_TILE
COL_GATE_RNN = (3 * ATTN_WIDTH + 2 * D_RNN + D_MODEL) // COL_TILE

ROW_TILE = 1664
MIX_ROW_TILE = 832
NORM_TILE = 640
COMBINE_TILE = 128
SPAN = 2048
NEG_BIG = -1e30

EXPERT_BLOCK = 128
EXPERT_CAP_BLOCKS = 12
EXPERT_CAP = EXPERT_BLOCK * EXPERT_CAP_BLOCKS
FF_TILE = 256
N_SLOTS = T_ALL * TOP_K
N_SORTED_ROWS = N_SLOTS + N_EXPERTS * EXPERT_BLOCK
N_SORTED_BLOCKS = N_SORTED_ROWS // EXPERT_BLOCK
N_WORK_ITEMS = N_EXPERTS + -(-N_SORTED_BLOCKS // EXPERT_CAP_BLOCKS)
XS_PARTS = 4
XS_PART_BLOCKS = N_SORTED_BLOCKS // XS_PARTS
assert XS_PART_BLOCKS * XS_PARTS == N_SORTED_BLOCKS


def _params(semantics, vmem=VMEM_LIMIT_BYTES):
    return pltpu.CompilerParams(dimension_semantics=semantics, vmem_limit_bytes=vmem)


def _rmsnorm_kernel(x_ref, g_ref, o_ref):
    x = x_ref[...]
    y = x * lax.rsqrt(jnp.mean(x * x, axis=-1, keepdims=True) + EPS)
    o_ref[...] = (y * g_ref[...]).astype(o_ref.dtype)


def _rmsnorm_bf16(x, g):
    t = x.shape[0]
    return pl.pallas_call(
        _rmsnorm_kernel,
        out_shape=jax.ShapeDtypeStruct((t, D_MODEL), BF16),
        grid=(t // NORM_TILE,),
        in_specs=[pl.BlockSpec((NORM_TILE, D_MODEL), lambda i: (i, 0)),
                  pl.BlockSpec((1, D_MODEL), lambda i: (0, 0))],
        out_specs=pl.BlockSpec((NORM_TILE, D_MODEL), lambda i: (i, 0)),
        compiler_params=_params(("parallel",)),
        name="rmsnorm_bf16",
    )(x, g)


def _in_proj_kernel(x_ref, w_ref, cos_ref, sin_ref, gain_ref, o_ref, wbf_ref):
    j = pl.program_id(0)

    @pl.when(pl.program_id(1) == 0)
    def _():
        wbf_ref[...] = w_ref[...].astype(BF16)

    o_ref[...] = jnp.dot(x_ref[...], wbf_ref[...], preferred_element_type=F32)

    @pl.when(j < 2 * ATTN_WIDTH // COL_TILE)
    def _():
        lane = lax.broadcasted_iota(I32, (ROW_TILE, HEAD_DIM), 1)
        first_half = lane < ROT_DIM // 2
        gain = gain_ref[...]
        cos = cos_ref[...]
        sin = sin_ref[...]
        for h in range(COL_TILE // HEAD_DIM):
            cols = slice(h * HEAD_DIM, (h + 1) * HEAD_DIM)
            xh = o_ref[:, cols]
            y = xh * lax.rsqrt(jnp.mean(xh * xh, axis=-1, keepdims=True) + EPS) * gain
            partner = jnp.where(first_half,
                                pltpu.roll(y, HEAD_DIM - ROT_DIM // 2, 1),
                                pltpu.roll(y, ROT_DIM // 2, 1))
            o_ref[:, cols] = y * cos + partner * sin


def _in_proj(xn, w_in, cos_t, sin_t, gains):
    n_tiles = IN_WIDTH // COL_TILE
    qk_tiles = ATTN_WIDTH // COL_TILE
    return pl.pallas_call(
        _in_proj_kernel,
        out_shape=jax.ShapeDtypeStruct((T_ALL, IN_WIDTH), F32),
        grid=(n_tiles, T_ALL // ROW_TILE),
        in_specs=[
            pl.BlockSpec((ROW_TILE, D_MODEL), lambda j, i: (i, 0)),
            pl.BlockSpec((None, D_MODEL, COL_TILE), lambda j, i: (0, 0, j)),
            pl.BlockSpec((ROW_TILE, HEAD_DIM), lambda j, i: (i, 0)),
            pl.BlockSpec((ROW_TILE, HEAD_DIM), lambda j, i: (i, 0)),
            pl.BlockSpec((None, 1, HEAD_DIM), lambda j, i: (jnp.minimum(j // qk_tiles, 1), 0, 0)),
        ],
        out_specs=pl.BlockSpec((ROW_TILE, COL_TILE), lambda j, i: (i, j)),
        scratch_shapes=[pltpu.VMEM((D_MODEL, COL_TILE), BF16)],
        compiler_params=_params(("arbitrary", "arbitrary")),
        name="in_proj",
    )(xn, w_in, cos_t, sin_t, gains)


def _dot_nt(a, b):
    return lax.dot_general(a, b, (((1,), (1,)), ((), ())), preferred_element_type=F32)


def _attn_prompt_kernel(*refs):
    ins = refs[:15]
    o_ref = refs[15]
    og_ref, lse_ref = refs[16], refs[17]
    span_idx = pl.program_id(0)
    n_sub = SPAN // ATTN_BLOCK
    shape3 = (n_sub, ATTN_BLOCK, ATTN_BLOCK)
    sub = lax.broadcasted_iota(I32, shape3, 0)
    qi = lax.broadcasted_iota(I32, shape3, 1)
    kj = lax.broadcasted_iota(I32, shape3, 2)
    mask_cur = qi >= kj
    mask_prev_band = kj >= qi
    scale = HEAD_DIM ** -0.5

    def bmm_nt(a, b):
        return lax.dot_general(a, b, (((2,), (2,)), ((0,), (0,))), preferred_element_type=F32)

    def bmm(a, b):
        return lax.dot_general(a, b, (((2,), (1,)), ((0,), (0,))), preferred_element_type=F32)

    for g, dil in enumerate(DILATIONS):
        q_ref, kc_ref, vc_ref, kp_ref, vp_ref = ins[5 * g:5 * g + 5]
        blk = ATTN_BLOCK * dil
        blocks = [(m, r) for m in range(SPAN // blk) for r in range(dil)]

        def rows_of(m, r, dil=dil, blk=blk):
            return pl.ds(m * blk + r, ATTN_BLOCK, stride=dil) if dil > 1 else pl.ds(m * blk, ATTN_BLOCK)

        q3 = jnp.stack([(q_ref[rows_of(m, r), :] * scale).astype(BF16) for m, r in blocks])
        kc = [kc_ref[rows_of(m, r), :].astype(BF16) for m, r in blocks]
        vc = [vc_ref[rows_of(m, r), :].astype(BF16) for m, r in blocks]
        kp = [kp_ref[rows_of(0, r), :].astype(BF16) if m == 0 else kc[(m - 1) * dil + r] for m, r in blocks]
        vp = [vp_ref[rows_of(0, r), :].astype(BF16) if m == 0 else vc[(m - 1) * dil + r] for m, r in blocks]
        mask_prev = jnp.logical_and(mask_prev_band, jnp.logical_or(sub >= dil, span_idx > 0))
        s_cur = jnp.where(mask_cur, bmm_nt(q3, jnp.stack(kc)), NEG_BIG)
        s_prev = jnp.where(mask_prev, bmm_nt(q3, jnp.stack(kp)), NEG_BIG)
        mx = jnp.maximum(jnp.max(s_cur, axis=-1, keepdims=True), jnp.max(s_prev, axis=-1, keepdims=True))
        p_cur = jnp.exp(s_cur - mx)
        p_prev = jnp.exp(s_prev - mx)
        den = jnp.sum(p_cur, axis=-1, keepdims=True) + jnp.sum(p_prev, axis=-1, keepdims=True)
        pv = bmm(p_cur.astype(BF16), jnp.stack(vc)) + bmm(p_prev.astype(BF16), jnp.stack(vp))
        out3 = pv / den
        lse3 = jnp.broadcast_to(mx + jnp.log(den), (n_sub, ATTN_BLOCK, HEAD_DIM))
        for n, (m, r) in enumerate(blocks):
            og_ref[g, rows_of(m, r), :] = out3[n]
            lse_ref[g, rows_of(m, r), :] = lse3[n]

    lse = lse_ref[...]
    top = jnp.max(lse, axis=0)
    w = jnp.exp(lse - top[None])
    o_ref[...] = jnp.sum(w * og_ref[...], axis=0) / jnp.sum(w, axis=0)


def _attn_prompt(p):
    q_cols, k_cols, v_cols = 0, N_HEADS, 2 * N_HEADS
    in_specs = []
    for g, dil in enumerate(DILATIONS):
        blk = ATTN_BLOCK * dil
        per_span = SPAN // blk

        def cur(off, g=g):
            return pl.BlockSpec((SPAN, HEAD_DIM), lambda s, h: (s, off + g * HEADS_PER_GROUP + h))

        def prev(off, g=g, blk=blk, per_span=per_span):
            return pl.BlockSpec((blk, HEAD_DIM),
                                lambda s, h: (jnp.maximum(s * per_span - 1, 0), off + g * HEADS_PER_GROUP + h))

        in_specs += [cur(q_cols), cur(k_cols), cur(v_cols), prev(k_cols), prev(v_cols)]
    return pl.pallas_call(
        _attn_prompt_kernel,
        out_shape=jax.ShapeDtypeStruct((SEQ, ATTN_OUT), F32),
        grid=(SEQ // SPAN, HEADS_PER_GROUP),
        in_specs=in_specs,
        out_specs=pl.BlockSpec((SPAN, HEAD_DIM), lambda s, h: (s, h)),
        scratch_shapes=[pltpu.VMEM((N_GROUPS, SPAN, HEAD_DIM), F32),
                        pltpu.VMEM((N_GROUPS, SPAN, HEAD_DIM), F32)],
        compiler_params=_params(("parallel", "parallel")),
        name="attn_prompt",
    )(*([p] * 15))


SAMPLE_BB = 8


def _attn_sample_kernel(q_ref, k_ref, v_ref, ck0, cv0, ck1, cv1, ck2, cv2, o_ref):
    caches = ((ck0, cv0), (ck1, cv1), (ck2, cv2))
    scale = HEAD_DIM ** -0.5
    for b in range(SAMPLE_BB):
        outs, lses = [], []
        for g in range(N_GROUPS):
            heads = slice(g * HEADS_PER_GROUP, (g + 1) * HEADS_PER_GROUP)
            q = q_ref[b, heads, :] * scale
            k_new = k_ref[b, heads, :]
            v_new = v_ref[b, heads, :]
            k_old = caches[g][0][b]
            v_old = caches[g][1][b]
            s_old = jnp.sum(k_old * q[None], axis=-1, keepdims=True)
            s_new = jnp.sum(k_new * q, axis=-1, keepdims=True)
            mx = jnp.maximum(jnp.max(s_old, axis=0), s_new)
            p_old = jnp.exp(s_old - mx[None])
            p_new = jnp.exp(s_new - mx)
            den = jnp.sum(p_old, axis=0) + p_new
            pv = jnp.sum(p_old * v_old, axis=0) + p_new * v_new
            outs.append(pv / den)
            lses.append(mx + jnp.log(den))
        top = jnp.maximum(jnp.maximum(lses[0], lses[1]), lses[2])
        ws = [jnp.exp(l - top) for l in lses]
        o_ref[b] = (ws[0] * outs[0] + ws[1] * outs[1] + ws[2] * outs[2]) / (ws[0] + ws[1] + ws[2])


def _attn_sample(q_s, k_s, v_s, cache_views):
    row = pl.BlockSpec((SAMPLE_BB, N_HEADS, HEAD_DIM), lambda b: (b, 0, 0))
    cache_spec = pl.BlockSpec((SAMPLE_BB, ATTN_BLOCK, None, HEADS_PER_GROUP, HEAD_DIM),
                              lambda b: (b, 0, 0, 0, 0))
    return pl.pallas_call(
        _attn_sample_kernel,
        out_shape=jax.ShapeDtypeStruct((DEC_BATCH, HEADS_PER_GROUP, HEAD_DIM), F32),
        grid=(DEC_BATCH // SAMPLE_BB,),
        in_specs=[row, row, row] + [cache_spec] * 6,
        out_specs=pl.BlockSpec((SAMPLE_BB, HEADS_PER_GROUP, HEAD_DIM), lambda b: (b, 0, 0)),
        compiler_params=_params(("parallel",)),
        name="attn_sample",
    )(q_s, k_s, v_s, *cache_views)


SHIFT_CHUNK_ROWS = 1024
SHIFT_CHUNKS_PER_STEP = 2
SHIFT_SLOTS = 2 * SHIFT_CHUNKS_PER_STEP
SHIFT_DMA_QUEUE = 1
FUSED_SHIFT_WINDOWS = (WINDOWS[2], WINDOWS[2])
N_FUSED_SHIFT = len(FUSED_SHIFT_WINDOWS)
N_ORDER_ONLY = 2
STANDALONE_SHIFT_ROWS = 2048


def _shift_steps(window):
    return DEC_BATCH * window // (SHIFT_CHUNK_ROWS * SHIFT_CHUNKS_PER_STEP)


def _shift_schedule():
    spans, lo = [], 0
    for w in FUSED_SHIFT_WINDOWS:
        spans.append((lo, lo + _shift_steps(w)))
        lo += _shift_steps(w)
    return spans


SHIFT_TOTAL_STEPS = _shift_schedule()[-1][1]


def _shift_caches_together(caches, news, outs):
    n = len(caches)
    w = caches[0].shape[2]
    assert all(c.shape == caches[0].shape for c in caches)
    bb = STANDALONE_SHIFT_ROWS // w
    n_chunks = DEC_BATCH // bb

    def run(buf, sem_in, sem_out):
        def in_copy(a, c, slot):
            return pltpu.make_async_copy(caches[a].at[0, pl.ds(c * bb, bb), pl.ds(1, w - 1)],
                                         buf.at[a, slot, :, pl.ds(0, w - 1)], sem_in.at[a, slot])

        def out_copy(a, c, slot):
            return pltpu.make_async_copy(buf.at[a, slot], outs[a].at[0, pl.ds(c * bb, bb)], sem_out.at[a, slot])

        for a in range(n):
            in_copy(a, 0, 0).start()

        @pl.loop(0, n_chunks)
        def _(c):
            slot = c % 2
            other = 1 - slot

            @pl.when(c + 1 < n_chunks)
            def _():
                @pl.when(c >= 1)
                def _():
                    for a in range(n):
                        out_copy(a, c - 1, other).wait()

                for a in range(n):
                    in_copy(a, c + 1, other).start()

            for a in range(n):
                in_copy(a, c, slot).wait()
                buf[a, slot, :, w - 1] = news[a][pl.ds(c * bb, bb)]
                out_copy(a, c, slot).start()

        for a in range(n):
            out_copy(a, n_chunks - 2, n_chunks % 2).wait()
            out_copy(a, n_chunks - 1, (n_chunks - 1) % 2).wait()

    pl.run_scoped(run, pltpu.VMEM((n, 2, bb, w, HEADS_PER_GROUP, HEAD_DIM), F32),
                  pltpu.SemaphoreType.DMA((n, 2)), pltpu.SemaphoreType.DMA((n, 2)))


def _cache_shift_kernel(n, *refs):
    caches, news, outs = refs[:n], refs[n:2 * n], refs[len(refs) - n:]
    _shift_caches_together(caches, news, outs)


def _cache_shift(caches, news, after, name):
    n = len(caches)
    any_spec = pl.BlockSpec(memory_space=pl.ANY)
    return pl.pallas_call(
        functools.partial(_cache_shift_kernel, n),
        out_shape=[jax.ShapeDtypeStruct(c.shape, c.dtype) for c in caches],
        in_specs=[any_spec] * n + [pl.BlockSpec(memory_space=pltpu.VMEM)] * n + [any_spec],
        out_specs=[any_spec] * n,
        compiler_params=pltpu.CompilerParams(vmem_limit_bytes=VMEM_LIMIT_BYTES),
        name=name,
    )(*caches, *news, after)


def _shift_chunk_copies(cache, out, sbuf, sem_in, sem_out, local_step, p, slot):
    w = cache.shape[2]
    ins, outs, new_rows = [], [], []
    copy = functools.partial(functools.partial, pltpu.make_async_copy)
    if w <= SHIFT_CHUNK_ROWS:
        per_chunk = SHIFT_CHUNK_ROWS // w
        b0 = (local_step * SHIFT_CHUNKS_PER_STEP + p) * per_chunk
        for bl in range(per_chunk):
            ins.append(copy(cache.at[0, b0 + bl, pl.ds(1, w - 1)], sbuf.at[slot, pl.ds(bl * w, w - 1)],
                            sem_in.at[slot]))
            outs.append(copy(sbuf.at[slot, pl.ds(bl * w, w)], out.at[0, b0 + bl], sem_out.at[slot]))
            new_rows.append((bl * w + w - 1, b0 + bl))
    else:
        assert w == SHIFT_CHUNK_ROWS * SHIFT_CHUNKS_PER_STEP
        last = p == SHIFT_CHUNKS_PER_STEP - 1
        n_in = SHIFT_CHUNK_ROWS - 1 if last else SHIFT_CHUNK_ROWS
        ins.append(copy(cache.at[0, local_step, pl.ds(1 + p * SHIFT_CHUNK_ROWS, n_in)],
                        sbuf.at[slot, pl.ds(0, n_in)], sem_in.at[slot]))
        outs.append(copy(sbuf.at[slot], out.at[0, local_step, pl.ds(p * SHIFT_CHUNK_ROWS, SHIFT_CHUNK_ROWS)],
                         sem_out.at[slot]))
        if last:
            new_rows.append((SHIFT_CHUNK_ROWS - 1, local_step))
    return ins, outs, new_rows


def _shift_step(step, caches, news, outs, sbuf, sem_in, sem_out):
    spans = _shift_schedule()

    def for_step(t, fn):
        for a, (lo, hi) in enumerate(spans):
            @pl.when(jnp.logical_and(t >= lo, t < hi))
            def _(a=a, lo=lo):
                for p in range(SHIFT_CHUNKS_PER_STEP):
                    slot = (t % 2) * SHIFT_CHUNKS_PER_STEP + p
                    ins, outs_, new_rows = _shift_chunk_copies(caches[a], outs[a], sbuf, sem_in, sem_out,
                                                               t - lo, p, slot)
                    fn(a, slot, ins, outs_, new_rows)

    def drain(a, slot, ins, outs_, new_rows):
        for make in outs_:
            make().wait()

    def prefetch(a, slot, ins, outs_, new_rows):
        for make in ins:
            make().start(priority=SHIFT_DMA_QUEUE)

    def forward(a, slot, ins, outs_, new_rows):
        for make in ins:
            make().wait()
        for row, b in new_rows:
            sbuf[slot, row] = news[a][b]
        for make in outs_:
            make().start(priority=SHIFT_DMA_QUEUE)

    @pl.when(step == 0)
    def _():
        for_step(step, prefetch)

    for_step(step - 1, drain)
    for_step(step + 1, prefetch)
    for_step(step, forward)


def _rglru_gates(xc, wa_ref, ba_ref, wx_ref, bx_ref, lam_ref):
    r_parts, i_parts = [], []
    for n in range(COL_TILE // RNN_BLOCK_W):
        xb = xc[:, n * RNN_BLOCK_W:(n + 1) * RNN_BLOCK_W].astype(BF16)
        r_parts.append(jnp.dot(xb, wa_ref[n].astype(BF16), preferred_element_type=F32))
        i_parts.append(jnp.dot(xb, wx_ref[n].astype(BF16), preferred_element_type=F32))
    r = jax.nn.sigmoid(jnp.concatenate(r_parts, axis=-1) + ba_ref[...])
    i = jax.nn.sigmoid(jnp.concatenate(i_parts, axis=-1) + bx_ref[...])
    neg_lam = -lam_ref[...]
    softplus = jnp.maximum(neg_lam, 0.0) + jnp.log1p(jnp.exp(-jnp.abs(neg_lam)))
    log_a = -LRU_C * r * softplus
    a = jnp.exp(log_a)
    b = jnp.sqrt(-jnp.tanh(log_a) * (jnp.exp(2.0 * log_a) + 1.0)) * i * xc
    return a, b


RNN_T_TILE = 512
CONV_PAD = SUBLANES


def _rglru_prompt_kernel(x_ref, gate_ref, cw_ref, cb_ref, wa_ref, ba_ref, wx_ref, bx_ref, lam_ref,
                         y_ref, hlast_ref, xbuf, a_scr, b_scr, h_scr, carry):
    t = pl.program_id(1)

    @pl.when(t == 0)
    def _():
        xbuf[0:CONV_PAD, :] = jnp.zeros((CONV_PAD, COL_TILE), F32)
        carry[...] = jnp.zeros((SUBLANES, COL_TILE), F32)

    @pl.when(t > 0)
    def _():
        xbuf[0:CONV_PAD, :] = xbuf[RNN_T_TILE:RNN_T_TILE + CONV_PAD, :]

    xbuf[CONV_PAD:CONV_PAD + RNN_T_TILE, :] = x_ref[...]
    xc = cb_ref[...] + sum(
        xbuf[pl.ds(CONV_PAD - (CONV_W - 1) + j, RNN_T_TILE), :] * cw_ref[j:j + 1, :] for j in range(CONV_W))
    a, b = _rglru_gates(xc, wa_ref, ba_ref, wx_ref, bx_ref, lam_ref)
    a_scr[...] = a
    b_scr[...] = b

    row = lax.broadcasted_iota(I32, (SUBLANES, COL_TILE), 0)

    def chunk(c, h):
        rows = pl.ds(pl.multiple_of(c * SUBLANES, SUBLANES), SUBLANES)
        ac = a_scr[rows, :]
        bc = b_scr[rows, :]
        for s in (1, 2, 4):
            a_sh = jnp.where(row >= s, pltpu.roll(ac, s, 0), 1.0)
            b_sh = jnp.where(row >= s, pltpu.roll(bc, s, 0), 0.0)
            bc = ac * b_sh + bc
            ac = ac * a_sh
        hh = ac * h + bc
        h_scr[rows, :] = hh
        return jnp.broadcast_to(hh[SUBLANES - 1:SUBLANES, :], (SUBLANES, COL_TILE))

    h_end = lax.fori_loop(0, RNN_T_TILE // SUBLANES, chunk, carry[...], unroll=4)
    carry[...] = h_end
    hlast_ref[...] = h_end[0:1, :]
    y_ref[...] = (h_scr[...] * jax.nn.gelu(gate_ref[...])).astype(y_ref.dtype)


def _rnn_param_specs(idx):
    return [
        pl.BlockSpec((None, CONV_W, COL_TILE), lambda *g: (0, 0, idx(*g))),
        pl.BlockSpec((1, COL_TILE), lambda *g: (0, idx(*g))),
        pl.BlockSpec((None, COL_TILE // RNN_BLOCK_W, RNN_BLOCK_W, RNN_BLOCK_W), lambda *g: (0, idx(*g), 0, 0)),
        pl.BlockSpec((1, COL_TILE), lambda *g: (0, idx(*g))),
        pl.BlockSpec((None, COL_TILE // RNN_BLOCK_W, RNN_BLOCK_W, RNN_BLOCK_W), lambda *g: (0, idx(*g), 0, 0)),
        pl.BlockSpec((1, COL_TILE), lambda *g: (0, idx(*g))),
        pl.BlockSpec((1, COL_TILE), lambda *g: (0, idx(*g))),
    ]


def _rglru_prompt(p, conv_w, conv_b, w_rg_a, b_rg_a, w_rg_x, b_rg_x, lru_lambda):
    n_c = D_RNN // COL_TILE
    return pl.pallas_call(
        _rglru_prompt_kernel,
        out_shape=[jax.ShapeDtypeStruct((SEQ, D_RNN), BF16), jax.ShapeDtypeStruct((1, D_RNN), F32)],
        grid=(n_c, SEQ // RNN_T_TILE),
        in_specs=[pl.BlockSpec((RNN_T_TILE, COL_TILE), lambda c, t: (t, COL_RNN_X + c)),
                  pl.BlockSpec((RNN_T_TILE, COL_TILE), lambda c, t: (t, COL_RNN_GATE + c))]
                 + _rnn_param_specs(lambda c, t: c),
        out_specs=[pl.BlockSpec((RNN_T_TILE, COL_TILE), lambda c, t: (t, c)),
                   pl.BlockSpec((1, COL_TILE), lambda c, t: (0, c))],
        scratch_shapes=[pltpu.VMEM((CONV_PAD + RNN_T_TILE, COL_TILE), F32),
                        pltpu.VMEM((RNN_T_TILE, COL_TILE), F32),
                        pltpu.VMEM((RNN_T_TILE, COL_TILE), F32),
                        pltpu.VMEM((RNN_T_TILE, COL_TILE), F32),
                        pltpu.VMEM((SUBLANES, COL_TILE), F32)],
        compiler_params=_params(("parallel", "arbitrary")),
        name="rglru_prompt",
    )(p, p, conv_w, conv_b, w_rg_a, b_rg_a, w_rg_x, b_rg_x, lru_lambda)


def _rglru_sample_kernel(x_ref, gate_ref, hist_ref, h0_ref, cw_ref, cb_ref, wa_ref, ba_ref, wx_ref, bx_ref,
                         lam_ref, y_ref, h_ref):
    xc = cb_ref[...] + x_ref[...] * cw_ref[CONV_W - 1:CONV_W, :]
    for j in range(CONV_W - 1):
        xc = xc + hist_ref[j] * cw_ref[j:j + 1, :]
    a, b = _rglru_gates(xc, wa_ref, ba_ref, wx_ref, bx_ref, lam_ref)
    h = a * h0_ref[...] + b
    h_ref[...] = h
    y_ref[...] = (h * jax.nn.gelu(gate_ref[...])).astype(y_ref.dtype)


def _rglru_sample(p, hist, h0, conv_w, conv_b, w_rg_a, b_rg_a, w_rg_x, b_rg_x, lru_lambda):
    n_c = D_RNN // COL_TILE
    row_blk = SEQ // DEC_BATCH
    return pl.pallas_call(
        _rglru_sample_kernel,
        out_shape=[jax.ShapeDtypeStruct((DEC_BATCH, D_RNN), BF16), jax.ShapeDtypeStruct((DEC_BATCH, D_RNN), F32)],
        grid=(n_c,),
        in_specs=[pl.BlockSpec((DEC_BATCH, COL_TILE), lambda c: (row_blk, COL_RNN_X + c)),
                  pl.BlockSpec((DEC_BATCH, COL_TILE), lambda c: (row_blk, COL_RNN_GATE + c)),
                  pl.BlockSpec((CONV_W - 1, DEC_BATCH, COL_TILE), lambda c: (0, 0, c)),
                  pl.BlockSpec((DEC_BATCH, COL_TILE), lambda c: (0, c))]
                 + _rnn_param_specs(lambda c: c),
        out_specs=[pl.BlockSpec((DEC_BATCH, COL_TILE), lambda c: (0, c)),
                   pl.BlockSpec((DEC_BATCH, COL_TILE), lambda c: (0, c))],
        compiler_params=_params(("parallel",)),
        name="rglru_sample",
    )(p, p, hist, h0, conv_w, conv_b, w_rg_a, b_rg_a, w_rg_x, b_rg_x, lru_lambda)


def _mix_kernel(attn_ref, y_ref, ga_ref, gr_ref, wap_ref, wrp_ref, o_ref, wap_bf, wrp_bf):
    @pl.when(pl.program_id(1) == 0)
    def _():
        wap_bf[...] = wap_ref[...].astype(BF16)
        wrp_bf[...] = wrp_ref[...].astype(BF16)

    attn_d = jnp.dot(attn_ref[...].astype(BF16), wap_bf[...], preferred_element_type=F32)
    rnn_d = jnp.dot(y_ref[...], wrp_bf[...], preferred_element_type=F32)
    mix = jax.nn.sigmoid(ga_ref[...]) * attn_d + jax.nn.sigmoid(gr_ref[...]) * rnn_d
    o_ref[...] = mix.astype(o_ref.dtype)


def _mix(attn, y, p, w_attn_proj, w_rnn_proj):
    return pl.pallas_call(
        _mix_kernel,
        out_shape=jax.ShapeDtypeStruct((T_ALL, D_MODEL), BF16),
        grid=(D_MODEL // COL_TILE, T_ALL // MIX_ROW_TILE),
        in_specs=[pl.BlockSpec((MIX_ROW_TILE, ATTN_OUT), lambda c, i: (i, 0)),
                  pl.BlockSpec((MIX_ROW_TILE, D_RNN), lambda c, i: (i, 0)),
                  pl.BlockSpec((MIX_ROW_TILE, COL_TILE), lambda c, i: (i, COL_GATE_ATTN + c)),
                  pl.BlockSpec((MIX_ROW_TILE, COL_TILE), lambda c, i: (i, COL_GATE_RNN + c)),
                  pl.BlockSpec((None, ATTN_OUT, COL_TILE), lambda c, i: (0, 0, c)),
                  pl.BlockSpec((None, D_RNN, COL_TILE), lambda c, i: (0, 0, c))],
        out_specs=pl.BlockSpec((MIX_ROW_TILE, COL_TILE), lambda c, i: (i, c)),
        scratch_shapes=[pltpu.VMEM((ATTN_OUT, COL_TILE), BF16), pltpu.VMEM((D_RNN, COL_TILE), BF16)],
        compiler_params=_params(("arbitrary", "arbitrary")),
        name="mix",
    )(attn, y, p, p, w_attn_proj, w_rnn_proj)


def _out_proj_kernel(mix_ref, w_ref, x_ref, o_ref, wbf):
    @pl.when(pl.program_id(1) == 0)
    def _():
        wbf[...] = w_ref[...].astype(BF16)

    o_ref[...] = x_ref[...] + jnp.dot(mix_ref[...], wbf[...], preferred_element_type=F32)


def _out_proj(mix, w_out, x):
    return pl.pallas_call(
        _out_proj_kernel,
        out_shape=jax.ShapeDtypeStruct((T_ALL, D_MODEL), F32),
        grid=(D_MODEL // COL_TILE, T_ALL // ROW_TILE),
        in_specs=[pl.BlockSpec((ROW_TILE, D_MODEL), lambda n, i: (i, 0)),
                  pl.BlockSpec((None, D_MODEL, COL_TILE), lambda n, i: (0, 0, n)),
                  pl.BlockSpec((ROW_TILE, COL_TILE), lambda n, i: (i, n))],
        out_specs=pl.BlockSpec((ROW_TILE, COL_TILE), lambda n, i: (i, n)),
        scratch_shapes=[pltpu.VMEM((D_MODEL, COL_TILE), BF16)],
        compiler_params=_params(("arbitrary", "arbitrary")),
        name="out_proj",
    )(mix, w_out, x)


HALF_MODEL = D_MODEL // 2
HIGH_HALF_MASK = -65536


def _pack_bf16_pairs(x):
    hi = pltpu.bitcast(x[:, :HALF_MODEL].astype(BF16).astype(F32), I32)
    lo = pltpu.bitcast(x[:, HALF_MODEL:].astype(BF16).astype(F32), I32)
    return pltpu.bitcast(hi | lax.shift_right_logical(lo, 16), F32)


def _unpack_bf16_pairs(words):
    bits = pltpu.bitcast(words, I32)
    first = pltpu.bitcast(bits & HIGH_HALF_MASK, F32).astype(BF16)
    second = pltpu.bitcast(lax.shift_left(bits, 16), F32).astype(BF16)
    return first, second


def _split_bf16(x):
    hi = x.astype(BF16)
    lo = (x - hi.astype(F32)).astype(BF16)
    return hi, lo


def _router_kernel(x_ref, g_ref, w_ref, b_ref, xn_ref, sel_ref, gate_ref, rank_ref, cnt_ref, carry, tri):
    step = pl.program_id(0)

    @pl.when(step == 0)
    def _():
        carry[...] = jnp.zeros((1, LANES), F32)
        ri = lax.broadcasted_iota(I32, (NORM_TILE, NORM_TILE), 0)
        ci = lax.broadcasted_iota(I32, (NORM_TILE, NORM_TILE), 1)
        tri[...] = jnp.where(ci < ri, 1.0, 0.0).astype(BF16)

    x = x_ref[...]
    xn = x * lax.rsqrt(jnp.mean(x * x, axis=-1, keepdims=True) + EPS) * g_ref[...]
    xn_ref[...] = _pack_bf16_pairs(xn)

    x_hi, x_lo = _split_bf16(xn)
    w_hi, w_lo = _split_bf16(w_ref[...])
    logits = (jnp.dot(x_hi, w_hi, preferred_element_type=F32)
              + jnp.dot(x_hi, w_lo, preferred_element_type=F32)
              + jnp.dot(x_lo, w_hi, preferred_element_type=F32)) + b_ref[...]

    lane = lax.broadcasted_iota(I32, (NORM_TILE, LANES), 1)
    work = logits
    vals, idxs = [], []
    for _ in range(TOP_K):
        mk = jnp.max(work, axis=-1, keepdims=True)
        ik = jnp.min(jnp.where(work == mk, lane, LANES), axis=-1, keepdims=True)
        vals.append(mk)
        idxs.append(ik)
        work = jnp.where(lane == ik, -jnp.inf, work)
    exps = [jnp.exp(v - vals[0]) for v in vals]
    den = exps[0] + exps[1] + exps[2] + exps[3]

    member = jnp.zeros((NORM_TILE, LANES), F32)
    for ik in idxs:
        member = member + jnp.where(lane == ik, 1.0, 0.0)
    before = jnp.dot(tri[...], member.astype(BF16), preferred_element_type=F32) + carry[...]
    sel = jnp.zeros((NORM_TILE, LANES), I32)
    gates = jnp.zeros((NORM_TILE, LANES), F32)
    ranks = jnp.zeros((NORM_TILE, LANES), F32)
    for k in range(TOP_K):
        rk = jnp.sum(jnp.where(lane == idxs[k], before, 0.0), axis=-1, keepdims=True)
        sel = jnp.where(lane == k, idxs[k], sel)
        gates = jnp.where(lane == k, exps[k] / den, gates)
        ranks = jnp.where(lane == k, rk, ranks)
    sel_ref[...] = sel
    gate_ref[...] = gates
    rank_ref[...] = ranks.astype(I32)
    carry[...] = carry[...] + jnp.sum(member, axis=0, keepdims=True)
    cnt_ref[...] = carry[...].astype(I32)


def _router(x2, g_ffn, w_router_pad, b_router_pad):
    tile_spec = pl.BlockSpec((NORM_TILE, LANES), lambda i: (i, 0))
    return pl.pallas_call(
        _router_kernel,
        out_shape=[jax.ShapeDtypeStruct((T_ALL, HALF_MODEL), F32),
                   jax.ShapeDtypeStruct((T_ALL, LANES), I32),
                   jax.ShapeDtypeStruct((T_ALL, LANES), F32),
                   jax.ShapeDtypeStruct((T_ALL, LANES), I32),
                   jax.ShapeDtypeStruct((1, LANES), I32)],
        grid=(T_ALL // NORM_TILE,),
        in_specs=[pl.BlockSpec((NORM_TILE, D_MODEL), lambda i: (i, 0)),
                  pl.BlockSpec((1, D_MODEL), lambda i: (0, 0)),
                  pl.BlockSpec((D_MODEL, LANES), lambda i: (0, 0)),
                  pl.BlockSpec((1, LANES), lambda i: (0, 0))],
        out_specs=[pl.BlockSpec((NORM_TILE, HALF_MODEL), lambda i: (i, 0)),
                   tile_spec, tile_spec, tile_spec,
                   pl.BlockSpec((1, LANES), lambda i: (0, 0))],
        scratch_shapes=[pltpu.VMEM((1, LANES), F32), pltpu.VMEM((NORM_TILE, NORM_TILE), BF16)],
        compiler_params=_params(("arbitrary",)),
        name="router",
    )(x2, g_ffn, w_router_pad, b_router_pad)


def _experts_kernel(item_e, item_row0, item_nblk, used_blocks, *rest):
    xs_parts, rest = rest[:XS_PARTS], rest[XS_PARTS:]
    wg_ref, wu_ref, wd_ref, bg_ref, bu_ref, bd_ref = rest[:6]
    rest = rest[6:]
    caches, news = rest[:N_FUSED_SHIFT], rest[N_FUSED_SHIFT:2 * N_FUSED_SHIFT]
    rest = rest[2 * N_FUSED_SHIFT + N_ORDER_ONLY:]
    out_hbm = rest[0]
    cache_outs = rest[1:N_FUSED_SHIFT + 1]
    (xbuf, acc, wg_bf, wu_bf, wd_bf, sem_in, sem_out, sbuf, sem_shift_in,
     sem_shift_out) = rest[N_FUSED_SHIFT + 1:]
    i = pl.program_id(0)
    j = pl.program_id(1)
    n_j = pl.num_programs(1)
    nblk = item_nblk[i]

    _shift_step(i * n_j + j, caches, news, cache_outs, sbuf, sem_shift_in, sem_shift_out)

    @pl.when(jnp.logical_and(i == 0, j == 0))
    def _():
        acc[0:EXPERT_BLOCK, :] = jnp.zeros((EXPERT_BLOCK, D_MODEL), F32)

        def zero_copy(c):
            dst = pl.ds(pl.multiple_of((used_blocks[0] + c) * EXPERT_BLOCK, EXPERT_BLOCK), EXPERT_BLOCK)
            return pltpu.make_async_copy(acc.at[0:EXPERT_BLOCK, :], out_hbm.at[dst, :], sem_out)

        n_slack = N_SORTED_BLOCKS - used_blocks[0]
        pl.loop(0, n_slack)(lambda c: zero_copy(c).start())
        pl.loop(0, n_slack)(lambda c: zero_copy(c).wait())

    n_items = pl.num_programs(0)
    x_slot = i % 2

    def x_copy(item, c, action):
        rows = pl.ds(pl.multiple_of(c * EXPERT_BLOCK, EXPERT_BLOCK), EXPERT_BLOCK)
        block = item_row0[item] // EXPERT_BLOCK + c
        for part, xs_hbm in enumerate(xs_parts):
            @pl.when(block // XS_PART_BLOCKS == part)
            def _(part=part, xs_hbm=xs_hbm):
                src = pl.ds(pl.multiple_of((block - part * XS_PART_BLOCKS) * EXPERT_BLOCK, EXPERT_BLOCK),
                            EXPERT_BLOCK)
                cp = pltpu.make_async_copy(xs_hbm.at[src, :], xbuf.at[item % 2, rows, :], sem_in.at[item % 2])
                getattr(cp, action)()

    def out_copy(item, c):
        rows = pl.ds(pl.multiple_of(c * EXPERT_BLOCK, EXPERT_BLOCK), EXPERT_BLOCK)
        dst = pl.ds(pl.multiple_of(item_row0[item] + c * EXPERT_BLOCK, EXPERT_BLOCK), EXPERT_BLOCK)
        return pltpu.make_async_copy(acc.at[rows, :], out_hbm.at[dst, :], sem_out)

    @pl.when(j == 0)
    def _():
        @pl.when(i == 0)
        def _():
            pl.loop(0, nblk)(lambda c: x_copy(i, c, "start"))

        pl.loop(0, nblk)(lambda c: x_copy(i, c, "wait"))
        nxt = jnp.minimum(i + 1, n_items - 1)

        @pl.when(i + 1 < n_items)
        def _():
            pl.loop(0, item_nblk[nxt])(lambda c: x_copy(nxt, c, "start"))

    @pl.when(nblk > 0)
    def _():
        wg_bf[...] = wg_ref[...].astype(BF16)
        wu_bf[...] = wu_ref[...].astype(BF16)
        wd_bf[...] = wd_ref[...].astype(BF16)

    prev = jnp.maximum(i - 1, 0)

    @pl.when(jnp.logical_and(j == 0, i > 0))
    def _():
        pl.loop(0, item_nblk[prev])(lambda c: out_copy(prev, c).wait())

    @pl.when(nblk > 0)
    def _():
        @pl.when(j == 0)
        def _():
            @pl.loop(0, nblk)
            def _(c):
                rows = pl.ds(pl.multiple_of(c * EXPERT_BLOCK, EXPERT_BLOCK), EXPERT_BLOCK)
                acc[rows, :] = jnp.broadcast_to(bd_ref[...], (EXPERT_BLOCK, D_MODEL))

        def process(start, size):
            rows = pl.ds(pl.multiple_of(start, EXPERT_BLOCK), size)
            xa, xb = _unpack_bf16_pairs(xbuf[x_slot, rows, :])
            gt = (jnp.dot(xa, wg_bf[0:HALF_MODEL, :], preferred_element_type=F32)
                  + jnp.dot(xb, wg_bf[HALF_MODEL:D_MODEL, :], preferred_element_type=F32)) + bg_ref[...]
            up = (jnp.dot(xa, wu_bf[0:HALF_MODEL, :], preferred_element_type=F32)
                  + jnp.dot(xb, wu_bf[HALF_MODEL:D_MODEL, :], preferred_element_type=F32)) + bu_ref[...]
            gt = jnp.minimum(gt, SWIGLU_LIMIT)
            up = jnp.clip(up, -SWIGLU_LIMIT, SWIGLU_LIMIT)
            act = gt * jax.nn.sigmoid(SWIGLU_ALPHA * gt) * (up + 1.0)
            acc[rows, :] += jnp.dot(act.astype(BF16), wd_bf[...], preferred_element_type=F32)

        quad = 4 * EXPERT_BLOCK

        @pl.loop(0, nblk // 4)
        def _(c):
            process(c * quad, 2 * EXPERT_BLOCK)
            process(c * quad + 2 * EXPERT_BLOCK, 2 * EXPERT_BLOCK)

        tail = (nblk // 4) * quad

        @pl.when(nblk % 4 >= 2)
        def _():
            process(tail, 2 * EXPERT_BLOCK)

        @pl.when(nblk % 2 == 1)
        def _():
            process((nblk - 1) * EXPERT_BLOCK, EXPERT_BLOCK)

    @pl.when(j == n_j - 1)
    def _():
        pl.loop(0, nblk)(lambda c: out_copy(i, c).start())

        @pl.when(i == n_items - 1)
        def _():
            pl.loop(0, nblk)(lambda c: out_copy(i, c).wait())


def _experts(item_e, item_row0, item_nblk, used_blocks, xs_parts, w_gate, w_up, w_down, b_gate, b_up, b_down,
             caches, news, run_after):
    assert len(run_after) == N_ORDER_ONLY
    n_j = D_FF // FF_TILE
    assert N_WORK_ITEMS * n_j > SHIFT_TOTAL_STEPS

    def jj(i, j, nblk):
        return jnp.where(nblk[i] > 0, j, n_j - 1)

    any_spec = pl.BlockSpec(memory_space=pl.ANY)
    grid_spec = pltpu.PrefetchScalarGridSpec(
        num_scalar_prefetch=4,
        grid=(N_WORK_ITEMS, n_j),
        in_specs=[any_spec] * XS_PARTS + [
            pl.BlockSpec((None, None, D_MODEL, FF_TILE), lambda i, j, e, r, n, u: (0, e[i], 0, jj(i, j, n))),
            pl.BlockSpec((None, None, D_MODEL, FF_TILE), lambda i, j, e, r, n, u: (0, e[i], 0, jj(i, j, n))),
            pl.BlockSpec((None, None, FF_TILE, D_MODEL), lambda i, j, e, r, n, u: (0, e[i], jj(i, j, n), 0)),
            pl.BlockSpec((None, 1, FF_TILE), lambda i, j, e, r, n, u: (e[i], 0, jj(i, j, n))),
            pl.BlockSpec((None, 1, FF_TILE), lambda i, j, e, r, n, u: (e[i], 0, jj(i, j, n))),
            pl.BlockSpec((None, 1, D_MODEL), lambda i, j, e, r, n, u: (e[i], 0, 0)),
        ] + [any_spec] * N_FUSED_SHIFT + [pl.BlockSpec(memory_space=pltpu.VMEM)] * N_FUSED_SHIFT
          + [any_spec] * N_ORDER_ONLY,
        out_specs=[any_spec] * (1 + N_FUSED_SHIFT),
        scratch_shapes=[pltpu.VMEM((2, EXPERT_CAP, HALF_MODEL), F32),
                        pltpu.VMEM((EXPERT_CAP, D_MODEL), F32),
                        pltpu.VMEM((D_MODEL, FF_TILE), BF16),
                        pltpu.VMEM((D_MODEL, FF_TILE), BF16),
                        pltpu.VMEM((FF_TILE, D_MODEL), BF16),
                        pltpu.SemaphoreType.DMA((2,)),
                        pltpu.SemaphoreType.DMA(()),
                        pltpu.VMEM((SHIFT_SLOTS, SHIFT_CHUNK_ROWS, HEADS_PER_GROUP, HEAD_DIM), F32),
                        pltpu.SemaphoreType.DMA((SHIFT_SLOTS,)),
                        pltpu.SemaphoreType.DMA((SHIFT_SLOTS,))],
    )
    results = pl.pallas_call(
        _experts_kernel,
        out_shape=[jax.ShapeDtypeStruct((N_SORTED_ROWS, D_MODEL), F32)]
                  + [jax.ShapeDtypeStruct(c.shape, c.dtype) for c in caches],
        grid_spec=grid_spec,
        compiler_params=_params(("arbitrary", "arbitrary")),
        name="experts",
    )(item_e, item_row0, item_nblk, used_blocks, *xs_parts, w_gate, w_up, w_down, b_gate, b_up, b_down,
      *caches, *news, *run_after)
    return results[0], results[1:]


def _combine_kernel(x_ref, gate_ref, o0, o1, o2, o3, *rest):
    yp_ref, ys_ref = rest[N_ORDER_ONLY:]
    gates = gate_ref[...]
    y = x_ref[...]
    for k, o_ref in enumerate((o0, o1, o2, o3)):
        y = y + gates[:, k:k + 1] * o_ref[...]
    is_prompt = pl.program_id(0) < SEQ // COMBINE_TILE

    @pl.when(is_prompt)
    def _():
        yp_ref[...] = y

    @pl.when(jnp.logical_not(is_prompt))
    def _():
        ys_ref[...] = y


def _combine(x2, gates, outs, run_after):
    assert DEC_BATCH == COMBINE_TILE and len(run_after) == N_ORDER_ONLY
    n_prompt = SEQ // COMBINE_TILE
    row = pl.BlockSpec((COMBINE_TILE, D_MODEL), lambda i: (i, 0))
    return pl.pallas_call(
        _combine_kernel,
        out_shape=[jax.ShapeDtypeStruct((SEQ, D_MODEL), F32), jax.ShapeDtypeStruct((DEC_BATCH, D_MODEL), F32)],
        grid=(n_prompt + 1,),
        in_specs=[row, pl.BlockSpec((COMBINE_TILE, LANES), lambda i: (i, 0)), row, row, row, row]
                 + [pl.BlockSpec(memory_space=pl.ANY)] * N_ORDER_ONLY,
        out_specs=[pl.BlockSpec((COMBINE_TILE, D_MODEL), lambda i: (jnp.minimum(i, n_prompt - 1), 0)),
                   pl.BlockSpec((COMBINE_TILE, D_MODEL), lambda i: (0, 0))],
        compiler_params=_params(("arbitrary",)),
        name="combine",
    )(x2, gates, *outs, *run_after)


def _rope_tables():
    half = ROT_DIM // 2
    inv = ROPE_THETA ** (-2.0 * jnp.arange(half, dtype=F32) / ROT_DIM)
    pos = jnp.concatenate([jnp.arange(SEQ), jnp.full((DEC_BATCH,), PAST_LEN)]).astype(F32)
    ang = pos[:, None] * inv[None, :]
    cos, sin = jnp.cos(ang), jnp.sin(ang)
    rest = HEAD_DIM - ROT_DIM
    cos_t = jnp.concatenate([cos, cos, jnp.ones((T_ALL, rest), F32)], axis=-1)
    sin_t = jnp.concatenate([-sin, sin, jnp.zeros((T_ALL, rest), F32)], axis=-1)
    return cos_t, sin_t


def _dispatch_plan(sel, ranks, counts):
    nb = (counts + EXPERT_BLOCK - 1) // EXPERT_BLOCK
    padded = nb * EXPERT_BLOCK
    pad_start = jnp.cumsum(padded) - padded
    dest = pad_start[sel] + ranks
    n_items = (nb + EXPERT_CAP_BLOCKS - 1) // EXPERT_CAP_BLOCKS
    item_end = jnp.cumsum(n_items)
    item_start = item_end - n_items
    w = jnp.arange(N_WORK_ITEMS, dtype=I32)
    total = item_end[-1]
    w_eff = jnp.minimum(w, total - 1)
    e_w = jnp.minimum(jnp.searchsorted(item_end, w_eff, side='right'), N_EXPERTS - 1).astype(I32)
    k_w = w_eff - item_start[e_w]
    row0 = pad_start[e_w] + k_w * EXPERT_CAP
    nblk = jnp.clip(nb[e_w] - k_w * EXPERT_CAP_BLOCKS, 0, EXPERT_CAP_BLOCKS)
    nblk = jnp.where(w < total, nblk, 0)
    used_blocks = jnp.sum(nb).astype(I32).reshape(1)
    return dest, e_w, row0.astype(I32), nblk.astype(I32), used_blocks


def kernel(x_prompt, x_sample, cache_k_w128, cache_v_w128, cache_k_w512, cache_v_w512, cache_k_w2048,
           cache_v_w2048, state_conv, state_rglru, g_mix, w_in, q_norm, k_norm, w_attn_proj, conv_w, conv_b,
           w_rg_a, b_rg_a, w_rg_x, b_rg_x, lru_lambda, w_rnn_proj, w_out, g_ffn, w_router, b_router,
           w_gate, b_gate, w_up, b_up, w_down, b_down):
    caches = (cache_k_w128, cache_v_w128, cache_k_w512, cache_v_w512, cache_k_w2048, cache_v_w2048)
    x = jnp.concatenate([x_prompt[0], x_sample[:, 0]], axis=0)

    xn = _rmsnorm_bf16(x, g_mix)
    cos_t, sin_t = _rope_tables()
    gains = jnp.stack([q_norm, k_norm])
    p = _in_proj(xn, w_in, cos_t, sin_t, gains)

    attn_p = _attn_prompt(p)
    qkv_s = p[SEQ:, :3 * ATTN_WIDTH].reshape(DEC_BATCH, 3, N_HEADS, HEAD_DIM)
    q_s, k_s, v_s = qkv_s[:, 0], qkv_s[:, 1], qkv_s[:, 2]
    views = [c.reshape(DEC_BATCH, ATTN_BLOCK, DILATIONS[n // 2], HEADS_PER_GROUP, HEAD_DIM)
             for n, c in enumerate(caches)]
    attn_s = _attn_sample(q_s, k_s, v_s, views).reshape(DEC_BATCH, ATTN_OUT)
    news = []
    for g in range(N_GROUPS):
        heads = slice(g * HEADS_PER_GROUP, (g + 1) * HEADS_PER_GROUP)
        news += [k_s[:, heads], v_s[:, heads]]
    attn = jnp.concatenate([attn_p, attn_s], axis=0)

    y_p, h_p = _rglru_prompt(p, conv_w, conv_b, w_rg_a, b_rg_a, w_rg_x, b_rg_x, lru_lambda)
    hist = jnp.transpose(state_conv[0], (1, 0, 2))
    y_s, h_s = _rglru_sample(p, hist, state_rglru[0], conv_w, conv_b, w_rg_a, b_rg_a, w_rg_x, b_rg_x,
                             lru_lambda)
    y = jnp.concatenate([y_p, y_s], axis=0)

    mix = _mix(attn, y, p, w_attn_proj, w_rnn_proj)
    x2 = _out_proj(mix, w_out, x)

    w_router_pad = jnp.pad(w_router[0], ((0, 0), (0, LANES - N_EXPERTS)))
    b_router_pad = jnp.pad(b_router, ((0, 0), (0, LANES - N_EXPERTS)), constant_values=NEG_BIG)
    xn2, sel, gates, ranks, counts = _router(x2, g_ffn, w_router_pad, b_router_pad)
    dest, item_e, item_row0, item_nblk, used_blocks = _dispatch_plan(
        sel[:, :TOP_K], ranks[:, :TOP_K], counts[0, :N_EXPERTS])
    tok = jnp.repeat(jnp.arange(T_ALL, dtype=I32), TOP_K)
    src_tok = jnp.zeros((N_SORTED_ROWS,), I32).at[dest.reshape(-1)].set(tok)
    part_rows = XS_PART_BLOCKS * EXPERT_BLOCK
    xs_parts = [xn2[src_tok[k * part_rows:(k + 1) * part_rows]] for k in range(XS_PARTS)]
    mid_caches = _cache_shift(caches[2:4], news[2:4], counts, "cache_shift_mid")
    out_sorted, big_caches = _experts(item_e, item_row0, item_nblk, used_blocks, xs_parts, w_gate, w_up, w_down,
                                      b_gate.reshape(N_EXPERTS, 1, D_FF), b_up.reshape(N_EXPERTS, 1, D_FF),
                                      b_down.reshape(N_EXPERTS, 1, D_MODEL), caches[4:], news[4:], mid_caches)
    small_caches = _cache_shift(caches[:2], news[:2], out_sorted, "cache_shift_small")
    new_caches = (*small_caches, *mid_caches, *big_caches)
    outs = [out_sorted[dest[:, k]] for k in range(TOP_K)]
    y_p, y_s = _combine(x2, gates, outs, small_caches)

    y_prompt = y_p.reshape(1, SEQ, D_MODEL)
    y_sample = y_s.reshape(DEC_BATCH, 1, D_MODEL)
    states_p = []
    for g, w in enumerate(WINDOWS):
        keep = min(w, SEQ)
        for off in (ATTN_WIDTH, 2 * ATTN_WIDTH):
            c0 = off + g * ATTN_OUT
            states_p.append(p[SEQ - keep:SEQ, c0:c0 + ATTN_OUT].reshape(1, 1, keep, HEADS_PER_GROUP, HEAD_DIM))
    conv_p = p[SEQ - (CONV_W - 1):SEQ, 3 * ATTN_WIDTH:3 * ATTN_WIDTH + D_RNN].reshape(1, 1, CONV_W - 1, D_RNN)
    rglru_p = h_p.reshape(1, 1, D_RNN)
    rnn_x_s = p[SEQ:, 3 * ATTN_WIDTH:3 * ATTN_WIDTH + D_RNN]
    conv_s = jnp.concatenate([state_conv[0][:, 1:], rnn_x_s[:, None, :]], axis=1)[None]
    rglru_s = h_s[None]
    return (y_prompt, y_sample, *states_p, conv_p, rglru_p, *new_caches, conv_s, rglru_s)
```

```python
import functools

import jax
import jax.numpy as jnp
from jax import lax
from jax.experimental import pallas as pl
from jax.experimental.pallas import tpu as pltpu

F32 = jnp.float32
BF16 = jnp.bfloat16
I32 = jnp.int32

D_MODEL = 2048
SEQ = 8192
DEC_BATCH = 128
PAST_LEN = 2048
T_ALL = SEQ + DEC_BATCH

HEAD_DIM = 128
HEADS_PER_GROUP = 4
WINDOWS = (128, 512, 2048)
DILATIONS = (1, 4, 16)
N_GROUPS = 3
N_HEADS = N_GROUPS * HEADS_PER_GROUP
ATTN_WIDTH = N_HEADS * HEAD_DIM
ATTN_OUT = HEADS_PER_GROUP * HEAD_DIM
ATTN_BLOCK = 128
ROT_DIM = HEAD_DIM // 4
ROPE_THETA = 500000.0
D_RNN = D_MODEL
RNN_BLOCK_W = 128
CONV_W = 4
LRU_C = 8.0
N_EXPERTS = 32
TOP_K = 4
D_FF = D_MODEL
SWIGLU_LIMIT = 7.0
SWIGLU_ALPHA = 1.702
EPS = 1e-6
IN_WIDTH = 3 * ATTN_WIDTH + 2 * D_RNN + 2 * D_MODEL

LANES = 128
SUBLANES = 8
VMEM_LIMIT_BYTES = 56 * 1024 * 1024

COL_TILE = 512
COL_RNN_X = (3 * ATTN_WIDTH) // COL_TILE
COL_RNN_GATE = (3 * ATTN_WIDTH + D_RNN) // COL_TILE
COL_GATE_ATTN = (3 * ATTN_WIDTH + 2 * D_RNN) // COL_TILE
COL_GATE_RNN = (3 * ATTN_WIDTH + 2 * D_RNN + D_MODEL) // COL_TILE

ROW_TILE = 1664
MIX_ROW_TILE = 832
NORM_TILE = 640
COMBINE_TILE = 128
SPAN = 2048
NEG_BIG = -1e30

EXPERT_BLOCK = 128
EXPERT_CAP_BLOCKS = 12
EXPERT_CAP = EXPERT_BLOCK * EXPERT_CAP_BLOCKS
FF_TILE = 256
N_SLOTS = T_ALL * TOP_K
N_SORTED_ROWS = N_SLOTS + N_EXPERTS * EXPERT_BLOCK
N_SORTED_BLOCKS = N_SORTED_ROWS // EXPERT_BLOCK
N_WORK_ITEMS = N_EXPERTS + -(-N_SORTED_BLOCKS // EXPERT_CAP_BLOCKS)
XS_PARTS = 4
XS_PART_BLOCKS = N_SORTED_BLOCKS // XS_PARTS
assert XS_PART_BLOCKS * XS_PARTS == N_SORTED_BLOCKS


def _params(semantics, vmem=VMEM_LIMIT_BYTES):
    return pltpu.CompilerParams(dimension_semantics=semantics, vmem_limit_bytes=vmem)


def _rmsnorm_kernel(x_ref, g_ref, o_ref):
    x = x_ref[...]
    y = x * lax.rsqrt(jnp.mean(x * x, axis=-1, keepdims=True) + EPS)
    o_ref[...] = (y * g_ref[...]).astype(o_ref.dtype)


def _rmsnorm_bf16(x, g):
    t = x.shape[0]
    return pl.pallas_call(
        _rmsnorm_kernel,
        out_shape=jax.ShapeDtypeStruct((t, D_MODEL), BF16),
        grid=(t // NORM_TILE,),
        in_specs=[pl.BlockSpec((NORM_TILE, D_MODEL), lambda i: (i, 0)),
                  pl.BlockSpec((1, D_MODEL), lambda i: (0, 0))],
        out_specs=pl.BlockSpec((NORM_TILE, D_MODEL), lambda i: (i, 0)),
        compiler_params=_params(("parallel",)),
        name="rmsnorm_bf16",
    )(x, g)


def _in_proj_kernel(x_ref, w_ref, cos_ref, sin_ref, gain_ref, o_ref, wbf_ref):
    j = pl.program_id(0)

    @pl.when(pl.program_id(1) == 0)
    def _():
        wbf_ref[...] = w_ref[...].astype(BF16)

    o_ref[...] = jnp.dot(x_ref[...], wbf_ref[...], preferred_element_type=F32)

    @pl.when(j < 2 * ATTN_WIDTH // COL_TILE)
    def _():
        lane = lax.broadcasted_iota(I32, (ROW_TILE, HEAD_DIM), 1)
        first_half = lane < ROT_DIM // 2
        gain = gain_ref[...]
        cos = cos_ref[...]
        sin = sin_ref[...]
        for h in range(COL_TILE // HEAD_DIM):
            cols = slice(h * HEAD_DIM, (h + 1) * HEAD_DIM)
            xh = o_ref[:, cols]
            y = xh * lax.rsqrt(jnp.mean(xh * xh, axis=-1, keepdims=True) + EPS) * gain
            partner = jnp.where(first_half,
                                pltpu.roll(y, HEAD_DIM - ROT_DIM // 2, 1),
                                pltpu.roll(y, ROT_DIM // 2, 1))
            o_ref[:, cols] = y * cos + partner * sin


def _in_proj(xn, w_in, cos_t, sin_t, gains):
    n_tiles = IN_WIDTH // COL_TILE
    qk_tiles = ATTN_WIDTH // COL_TILE
    return pl.pallas_call(
        _in_proj_kernel,
        out_shape=jax.ShapeDtypeStruct((T_ALL, IN_WIDTH), F32),
        grid=(n_tiles, T_ALL // ROW_TILE),
        in_specs=[
            pl.BlockSpec((ROW_TILE, D_MODEL), lambda j, i: (i, 0)),
            pl.BlockSpec((None, D_MODEL, COL_TILE), lambda j, i: (0, 0, j)),
            pl.BlockSpec((ROW_TILE, HEAD_DIM), lambda j, i: (i, 0)),
            pl.BlockSpec((ROW_TILE, HEAD_DIM), lambda j, i: (i, 0)),
            pl.BlockSpec((None, 1, HEAD_DIM), lambda j, i: (jnp.minimum(j // qk_tiles, 1), 0, 0)),
        ],
        out_specs=pl.BlockSpec((ROW_TILE, COL_TILE), lambda j, i: (i, j)),
        scratch_shapes=[pltpu.VMEM((D_MODEL, COL_TILE), BF16)],
        compiler_params=_params(("arbitrary", "arbitrary")),
        name="in_proj",
    )(xn, w_in, cos_t, sin_t, gains)


def _dot_nt(a, b):
    return lax.dot_general(a, b, (((1,), (1,)), ((), ())), preferred_element_type=F32)


def _attn_prompt_kernel(*refs):
    ins = refs[:15]
    o_ref = refs[15]
    og_ref, lse_ref = refs[16], refs[17]
    span_idx = pl.program_id(0)
    n_sub = SPAN // ATTN_BLOCK
    shape3 = (n_sub, ATTN_BLOCK, ATTN_BLOCK)
    sub = lax.broadcasted_iota(I32, shape3, 0)
    qi = lax.broadcasted_iota(I32, shape3, 1)
    kj = lax.broadcasted_iota(I32, shape3, 2)
    mask_cur = qi >= kj
    mask_prev_band = kj >= qi
    scale = HEAD_DIM ** -0.5

    def bmm_nt(a, b):
        return lax.dot_general(a, b, (((2,), (2,)), ((0,), (0,))), preferred_element_type=F32)

    def bmm(a, b):
        return lax.dot_general(a, b, (((2,), (1,)), ((0,), (0,))), preferred_element_type=F32)

    for g, dil in enumerate(DILATIONS):
        q_ref, kc_ref, vc_ref, kp_ref, vp_ref = ins[5 * g:5 * g + 5]
        blk = ATTN_BLOCK * dil
        blocks = [(m, r) for m in range(SPAN // blk) for r in range(dil)]

        def rows_of(m, r, dil=dil, blk=blk):
            return pl.ds(m * blk + r, ATTN_BLOCK, stride=dil) if dil > 1 else pl.ds(m * blk, ATTN_BLOCK)

        q3 = jnp.stack([(q_ref[rows_of(m, r), :] * scale).astype(BF16) for m, r in blocks])
        kc = [kc_ref[rows_of(m, r), :].astype(BF16) for m, r in blocks]
        vc = [vc_ref[rows_of(m, r), :].astype(BF16) for m, r in blocks]
        kp = [kp_ref[rows_of(0, r), :].astype(BF16) if m == 0 else kc[(m - 1) * dil + r] for m, r in blocks]
        vp = [vp_ref[rows_of(0, r), :].astype(BF16) if m == 0 else vc[(m - 1) * dil + r] for m, r in blocks]
        mask_prev = jnp.logical_and(mask_prev_band, jnp.logical_or(sub >= dil, span_idx > 0))
        s_cur = jnp.where(mask_cur, bmm_nt(q3, jnp.stack(kc)), NEG_BIG)
        s_prev = jnp.where(mask_prev, bmm_nt(q3, jnp.stack(kp)), NEG_BIG)
        mx = jnp.maximum(jnp.max(s_cur, axis=-1, keepdims=True), jnp.max(s_prev, axis=-1, keepdims=True))
        p_cur = jnp.exp(s_cur - mx)
        p_prev = jnp.exp(s_prev - mx)
        den = jnp.sum(p_cur, axis=-1, keepdims=True) + jnp.sum(p_prev, axis=-1, keepdims=True)
        pv = bmm(p_cur.astype(BF16), jnp.stack(vc)) + bmm(p_prev.astype(BF16), jnp.stack(vp))
        out3 = pv / den
        lse3 = jnp.broadcast_to(mx + jnp.log(den), (n_sub, ATTN_BLOCK, HEAD_DIM))
        for n, (m, r) in enumerate(blocks):
            og_ref[g, rows_of(m, r), :] = out3[n]
            lse_ref[g, rows_of(m, r), :] = lse3[n]

    lse = lse_ref[...]
    top = jnp.max(lse, axis=0)
    w = jnp.exp(lse - top[None])
    o_ref[...] = jnp.sum(w * og_ref[...], axis=0) / jnp.sum(w, axis=0)


def _attn_prompt(p):
    q_cols, k_cols, v_cols = 0, N_HEADS, 2 * N_HEADS
    in_specs = []
    for g, dil in enumerate(DILATIONS):
        blk = ATTN_BLOCK * dil
        per_span = SPAN // blk

        def cur(off, g=g):
            return pl.BlockSpec((SPAN, HEAD_DIM), lambda s, h: (s, off + g * HEADS_PER_GROUP + h))

        def prev(off, g=g, blk=blk, per_span=per_span):
            return pl.BlockSpec((blk, HEAD_DIM),
                                lambda s, h: (jnp.maximum(s * per_span - 1, 0), off + g * HEADS_PER_GROUP + h))

        in_specs += [cur(q_cols), cur(k_cols), cur(v_cols), prev(k_cols), prev(v_cols)]
    return pl.pallas_call(
        _attn_prompt_kernel,
        out_shape=jax.ShapeDtypeStruct((SEQ, ATTN_OUT), F32),
        grid=(SEQ // SPAN, HEADS_PER_GROUP),
        in_specs=in_specs,
        out_specs=pl.BlockSpec((SPAN, HEAD_DIM), lambda s, h: (s, h)),
        scratch_shapes=[pltpu.VMEM((N_GROUPS, SPAN, HEAD_DIM), F32),
                        pltpu.VMEM((N_GROUPS, SPAN, HEAD_DIM), F32)],
        compiler_params=_params(("parallel", "parallel")),
        name="attn_prompt",
    )(*([p] * 15))


SAMPLE_BB = 8


def _attn_sample_kernel(q_ref, k_ref, v_ref, ck0, cv0, ck1, cv1, ck2, cv2, o_ref):
    caches = ((ck0, cv0), (ck1, cv1), (ck2, cv2))
    scale = HEAD_DIM ** -0.5
    for b in range(SAMPLE_BB):
        outs, lses = [], []
        for g in range(N_GROUPS):
            heads = slice(g * HEADS_PER_GROUP, (g + 1) * HEADS_PER_GROUP)
            q = q_ref[b, heads, :] * scale
            k_new = k_ref[b, heads, :]
            v_new = v_ref[b, heads, :]
            k_old = caches[g][0][b]
            v_old = caches[g][1][b]
            s_old = jnp.sum(k_old * q[None], axis=-1, keepdims=True)
            s_new = jnp.sum(k_new * q, axis=-1, keepdims=True)
            mx = jnp.maximum(jnp.max(s_old, axis=0), s_new)
            p_old = jnp.exp(s_old - mx[None])
            p_new = jnp.exp(s_new - mx)
            den = jnp.sum(p_old, axis=0) + p_new
            pv = jnp.sum(p_old * v_old, axis=0) + p_new * v_new
            outs.append(pv / den)
            lses.append(mx + jnp.log(den))
        top = jnp.maximum(jnp.maximum(lses[0], lses[1]), lses[2])
        ws = [jnp.exp(l - top) for l in lses]
        o_ref[b] = (ws[0] * outs[0] + ws[1] * outs[1] + ws[2] * outs[2]) / (ws[0] + ws[1] + ws[2])


def _attn_sample(q_s, k_s, v_s, cache_views):
    row = pl.BlockSpec((SAMPLE_BB, N_HEADS, HEAD_DIM), lambda b: (b, 0, 0))
    cache_spec = pl.BlockSpec((SAMPLE_BB, ATTN_BLOCK, None, HEADS_PER_GROUP, HEAD_DIM),
                              lambda b: (b, 0, 0, 0, 0))
    return pl.pallas_call(
        _attn_sample_kernel,
        out_shape=jax.ShapeDtypeStruct((DEC_BATCH, HEADS_PER_GROUP, HEAD_DIM), F32),
        grid=(DEC_BATCH // SAMPLE_BB,),
        in_specs=[row, row, row] + [cache_spec] * 6,
        out_specs=pl.BlockSpec((SAMPLE_BB, HEADS_PER_GROUP, HEAD_DIM), lambda b: (b, 0, 0)),
        compiler_params=_params(("parallel",)),
        name="attn_sample",
    )(q_s, k_s, v_s, *cache_views)


SHIFT_CHUNK_ROWS = 1024
SHIFT_CHUNKS_PER_STEP = 2
SHIFT_SLOTS = 2 * SHIFT_CHUNKS_PER_STEP
SHIFT_DMA_QUEUE = 1
FUSED_SHIFT_WINDOWS = (WINDOWS[2], WINDOWS[2])
N_FUSED_SHIFT = len(FUSED_SHIFT_WINDOWS)
N_ORDER_ONLY = 2
STANDALONE_SHIFT_ROWS = 2048


def _shift_steps(window):
    return DEC_BATCH * window // (SHIFT_CHUNK_ROWS * SHIFT_CHUNKS_PER_STEP)


def _shift_schedule():
    spans, lo = [], 0
    for w in FUSED_SHIFT_WINDOWS:
        spans.append((lo, lo + _shift_steps(w)))
        lo += _shift_steps(w)
    return spans


SHIFT_TOTAL_STEPS = _shift_schedule()[-1][1]


def _shift_caches_together(caches, news, outs):
    n = len(caches)
    w = caches[0].shape[2]
    assert all(c.shape == caches[0].shape for c in caches)
    bb = STANDALONE_SHIFT_ROWS // w
    n_chunks = DEC_BATCH // bb

    def run(buf, sem_in, sem_out):
        def in_copy(a, c, slot):
            return pltpu.make_async_copy(caches[a].at[0, pl.ds(c * bb, bb), pl.ds(1, w - 1)],
                                         buf.at[a, slot, :, pl.ds(0, w - 1)], sem_in.at[a, slot])

        def out_copy(a, c, slot):
            return pltpu.make_async_copy(buf.at[a, slot], outs[a].at[0, pl.ds(c * bb, bb)], sem_out.at[a, slot])

        for a in range(n):
            in_copy(a, 0, 0).start()

        @pl.loop(0, n_chunks)
        def _(c):
            slot = c % 2
            other = 1 - slot

            @pl.when(c + 1 < n_chunks)
            def _():
                @pl.when(c >= 1)
                def _():
                    for a in range(n):
                        out_copy(a, c - 1, other).wait()

                for a in range(n):
                    in_copy(a, c + 1, other).start()

            for a in range(n):
                in_copy(a, c, slot).wait()
                buf[a, slot, :, w - 1] = news[a][pl.ds(c * bb, bb)]
                out_copy(a, c, slot).start()

        for a in range(n):
            out_copy(a, n_chunks - 2, n_chunks % 2).wait()
            out_copy(a, n_chunks - 1, (n_chunks - 1) % 2).wait()

    pl.run_scoped(run, pltpu.VMEM((n, 2, bb, w, HEADS_PER_GROUP, HEAD_DIM), F32),
                  pltpu.SemaphoreType.DMA((n, 2)), pltpu.SemaphoreType.DMA((n, 2)))


def _cache_shift_kernel(n, *refs):
    caches, news, outs = refs[:n], refs[n:2 * n], refs[len(refs) - n:]
    _shift_caches_together(caches, news, outs)


def _cache_shift(caches, news, after, name):
    n = len(caches)
    any_spec = pl.BlockSpec(memory_space=pl.ANY)
    return pl.pallas_call(
        functools.partial(_cache_shift_kernel, n),
        out_shape=[jax.ShapeDtypeStruct(c.shape, c.dtype) for c in caches],
        in_specs=[any_spec] * n + [pl.BlockSpec(memory_space=pltpu.VMEM)] * n + [any_spec],
        out_specs=[any_spec] * n,
        compiler_params=pltpu.CompilerParams(vmem_limit_bytes=VMEM_LIMIT_BYTES),
        name=name,
    )(*caches, *news, after)


def _shift_chunk_copies(cache, out, sbuf, sem_in, sem_out, local_step, p, slot):
    w = cache.shape[2]
    ins, outs, new_rows = [], [], []
    copy = functools.partial(functools.partial, pltpu.make_async_copy)
    if w <= SHIFT_CHUNK_ROWS:
        per_chunk = SHIFT_CHUNK_ROWS // w
        b0 = (local_step * SHIFT_CHUNKS_PER_STEP + p) * per_chunk
        for bl in range(per_chunk):
            ins.append(copy(cache.at[0, b0 + bl, pl.ds(1, w - 1)], sbuf.at[slot, pl.ds(bl * w, w - 1)],
                            sem_in.at[slot]))
            outs.append(copy(sbuf.at[slot, pl.ds(bl * w, w)], out.at[0, b0 + bl], sem_out.at[slot]))
            new_rows.append((bl * w + w - 1, b0 + bl))
    else:
        assert w == SHIFT_CHUNK_ROWS * SHIFT_CHUNKS_PER_STEP
        last = p == SHIFT_CHUNKS_PER_STEP - 1
        n_in = SHIFT_CHUNK_ROWS - 1 if last else SHIFT_CHUNK_ROWS
        ins.append(copy(cache.at[0, local_step, pl.ds(1 + p * SHIFT_CHUNK_ROWS, n_in)],
                        sbuf.at[slot, pl.ds(0, n_in)], sem_in.at[slot]))
        outs.append(copy(sbuf.at[slot], out.at[0, local_step, pl.ds(p * SHIFT_CHUNK_ROWS, SHIFT_CHUNK_ROWS)],
                         sem_out.at[slot]))
        if last:
            new_rows.append((SHIFT_CHUNK_ROWS - 1, local_step))
    return ins, outs, new_rows


def _shift_step(step, caches, news, outs, sbuf, sem_in, sem_out):
    spans = _shift_schedule()

    def for_step(t, fn):
        for a, (lo, hi) in enumerate(spans):
            @pl.when(jnp.logical_and(t >= lo, t < hi))
            def _(a=a, lo=lo):
                for p in range(SHIFT_CHUNKS_PER_STEP):
                    slot = (t % 2) * SHIFT_CHUNKS_PER_STEP + p
                    ins, outs_, new_rows = _shift_chunk_copies(caches[a], outs[a], sbuf, sem_in, sem_out,
                                                               t - lo, p, slot)
                    fn(a, slot, ins, outs_, new_rows)

    def drain(a, slot, ins, outs_, new_rows):
        for make in outs_:
            make().wait()

    def prefetch(a, slot, ins, outs_, new_rows):
        for make in ins:
            make().start(priority=SHIFT_DMA_QUEUE)

    def forward(a, slot, ins, outs_, new_rows):
        for make in ins:
            make().wait()
        for row, b in new_rows:
            sbuf[slot, row] = news[a][b]
        for make in outs_:
            make().start(priority=SHIFT_DMA_QUEUE)

    @pl.when(step == 0)
    def _():
        for_step(step, prefetch)

    for_step(step - 1, drain)
    for_step(step + 1, prefetch)
    for_step(step, forward)


def _rglru_gates(xc, wa_ref, ba_ref, wx_ref, bx_ref, lam_ref):
    r_parts, i_parts = [], []
    for n in range(COL_TILE // RNN_BLOCK_W):
        xb = xc[:, n * RNN_BLOCK_W:(n + 1) * RNN_BLOCK_W].astype(BF16)
        r_parts.append(jnp.dot(xb, wa_ref[n].astype(BF16), preferred_element_type=F32))
        i_parts.append(jnp.dot(xb, wx_ref[n].astype(BF16), preferred_element_type=F32))
    r = jax.nn.sigmoid(jnp.concatenate(r_parts, axis=-1) + ba_ref[...])
    i = jax.nn.sigmoid(jnp.concatenate(i_parts, axis=-1) + bx_ref[...])
    neg_lam = -lam_ref[...]
    softplus = jnp.maximum(neg_lam, 0.0) + jnp.log1p(jnp.exp(-jnp.abs(neg_lam)))
    log_a = -LRU_C * r * softplus
    a = jnp.exp(log_a)
    b = jnp.sqrt(-jnp.tanh(log_a) * (jnp.exp(2.0 * log_a) + 1.0)) * i * xc
    return a, b


RNN_T_TILE = 512
CONV_PAD = SUBLANES


def _rglru_prompt_kernel(x_ref, gate_ref, cw_ref, cb_ref, wa_ref, ba_ref, wx_ref, bx_ref, lam_ref,
                         y_ref, hlast_ref, xbuf, a_scr, b_scr, h_scr, carry):
    t = pl.program_id(1)

    @pl.when(t == 0)
    def _():
        xbuf[0:CONV_PAD, :] = jnp.zeros((CONV_PAD, COL_TILE), F32)
        carry[...] = jnp.zeros((SUBLANES, COL_TILE), F32)

    @pl.when(t > 0)
    def _():
        xbuf[0:CONV_PAD, :] = xbuf[RNN_T_TILE:RNN_T_TILE + CONV_PAD, :]

    xbuf[CONV_PAD:CONV_PAD + RNN_T_TILE, :] = x_ref[...]
    xc = cb_ref[...] + sum(
        xbuf[pl.ds(CONV_PAD - (CONV_W - 1) + j, RNN_T_TILE), :] * cw_ref[j:j + 1, :] for j in range(CONV_W))
    a, b = _rglru_gates(xc, wa_ref, ba_ref, wx_ref, bx_ref, lam_ref)
    a_scr[...] = a
    b_scr[...] = b

    row = lax.broadcasted_iota(I32, (SUBLANES, COL_TILE), 0)

    def chunk(c, h):
        rows = pl.ds(pl.multiple_of(c * SUBLANES, SUBLANES), SUBLANES)
        ac = a_scr[rows, :]
        bc = b_scr[rows, :]
        for s in (1, 2, 4):
            a_sh = jnp.where(row >= s, pltpu.roll(ac, s, 0), 1.0)
            b_sh = jnp.where(row >= s, pltpu.roll(bc, s, 0), 0.0)
            bc = ac * b_sh + bc
            ac = ac * a_sh
        hh = ac * h + bc
        h_scr[rows, :] = hh
        return jnp.broadcast_to(hh[SUBLANES - 1:SUBLANES, :], (SUBLANES, COL_TILE))

    h_end = lax.fori_loop(0, RNN_T_TILE // SUBLANES, chunk, carry[...], unroll=4)
    carry[...] = h_end
    hlast_ref[...] = h_end[0:1, :]
    y_ref[...] = (h_scr[...] * jax.nn.gelu(gate_ref[...])).astype(y_ref.dtype)


def _rnn_param_specs(idx):
    return [
        pl.BlockSpec((None, CONV_W, COL_TILE), lambda *g: (0, 0, idx(*g))),
        pl.BlockSpec((1, COL_TILE), lambda *g: (0, idx(*g))),
        pl.BlockSpec((None, COL_TILE // RNN_BLOCK_W, RNN_BLOCK_W, RNN_BLOCK_W), lambda *g: (0, idx(*g), 0, 0)),
        pl.BlockSpec((1, COL_TILE), lambda *g: (0, idx(*g))),
        pl.BlockSpec((None, COL_TILE // RNN_BLOCK_W, RNN_BLOCK_W, RNN_BLOCK_W), lambda *g: (0, idx(*g), 0, 0)),
        pl.BlockSpec((1, COL_TILE), lambda *g: (0, idx(*g))),
        pl.BlockSpec((1, COL_TILE), lambda *g: (0, idx(*g))),
    ]


def _rglru_prompt(p, conv_w, conv_b, w_rg_a, b_rg_a, w_rg_x, b_rg_x, lru_lambda):
    n_c = D_RNN // COL_TILE
    return pl.pallas_call(
        _rglru_prompt_kernel,
        out_shape=[jax.ShapeDtypeStruct((SEQ, D_RNN), BF16), jax.ShapeDtypeStruct((1, D_RNN), F32)],
        grid=(n_c, SEQ // RNN_T_TILE),
        in_specs=[pl.BlockSpec((RNN_T_TILE, COL_TILE), lambda c, t: (t, COL_RNN_X + c)),
                  pl.BlockSpec((RNN_T_TILE, COL_TILE), lambda c, t: (t, COL_RNN_GATE + c))]
                 + _rnn_param_specs(lambda c, t: c),
        out_specs=[pl.BlockSpec((RNN_T_TILE, COL_TILE), lambda c, t: (t, c)),
                   pl.BlockSpec((1, COL_TILE), lambda c, t: (0, c))],
        scratch_shapes=[pltpu.VMEM((CONV_PAD + RNN_T_TILE, COL_TILE), F32),
                        pltpu.VMEM((RNN_T_TILE, COL_TILE), F32),
                        pltpu.VMEM((RNN_T_TILE, COL_TILE), F32),
                        pltpu.VMEM((RNN_T_TILE, COL_TILE), F32),
                        pltpu.VMEM((SUBLANES, COL_TILE), F32)],
        compiler_params=_params(("parallel", "arbitrary")),
        name="rglru_prompt",
    )(p, p, conv_w, conv_b, w_rg_a, b_rg_a, w_rg_x, b_rg_x, lru_lambda)


def _rglru_sample_kernel(x_ref, gate_ref, hist_ref, h0_ref, cw_ref, cb_ref, wa_ref, ba_ref, wx_ref, bx_ref,
                         lam_ref, y_ref, h_ref):
    xc = cb_ref[...] + x_ref[...] * cw_ref[CONV_W - 1:CONV_W, :]
    for j in range(CONV_W - 1):
        xc = xc + hist_ref[j] * cw_ref[j:j + 1, :]
    a, b = _rglru_gates(xc, wa_ref, ba_ref, wx_ref, bx_ref, lam_ref)
    h = a * h0_ref[...] + b
    h_ref[...] = h
    y_ref[...] = (h * jax.nn.gelu(gate_ref[...])).astype(y_ref.dtype)


def _rglru_sample(p, hist, h0, conv_w, conv_b, w_rg_a, b_rg_a, w_rg_x, b_rg_x, lru_lambda):
    n_c = D_RNN // COL_TILE
    row_blk = SEQ // DEC_BATCH
    return pl.pallas_call(
        _rglru_sample_kernel,
        out_shape=[jax.ShapeDtypeStruct((DEC_BATCH, D_RNN), BF16), jax.ShapeDtypeStruct((DEC_BATCH, D_RNN), F32)],
        grid=(n_c,),
        in_specs=[pl.BlockSpec((DEC_BATCH, COL_TILE), lambda c: (row_blk, COL_RNN_X + c)),
                  pl.BlockSpec((DEC_BATCH, COL_TILE), lambda c: (row_blk, COL_RNN_GATE + c)),
                  pl.BlockSpec((CONV_W - 1, DEC_BATCH, COL_TILE), lambda c: (0, 0, c)),
                  pl.BlockSpec((DEC_BATCH, COL_TILE), lambda c: (0, c))]
                 + _rnn_param_specs(lambda c: c),
        out_specs=[pl.BlockSpec((DEC_BATCH, COL_TILE), lambda c: (0, c)),
                   pl.BlockSpec((DEC_BATCH, COL_TILE), lambda c: (0, c))],
        compiler_params=_params(("parallel",)),
        name="rglru_sample",
    )(p, p, hist, h0, conv_w, conv_b, w_rg_a, b_rg_a, w_rg_x, b_rg_x, lru_lambda)


def _mix_kernel(attn_ref, y_ref, ga_ref, gr_ref, wap_ref, wrp_ref, o_ref, wap_bf, wrp_bf):
    @pl.when(pl.program_id(1) == 0)
    def _():
        wap_bf[...] = wap_ref[...].astype(BF16)
        wrp_bf[...] = wrp_ref[...].astype(BF16)

    attn_d = jnp.dot(attn_ref[...].astype(BF16), wap_bf[...], preferred_element_type=F32)
    rnn_d = jnp.dot(y_ref[...], wrp_bf[...], preferred_element_type=F32)
    mix = jax.nn.sigmoid(ga_ref[...]) * attn_d + jax.nn.sigmoid(gr_ref[...]) * rnn_d
    o_ref[...] = mix.astype(o_ref.dtype)


def _mix(attn, y, p, w_attn_proj, w_rnn_proj):
    return pl.pallas_call(
        _mix_kernel,
        out_shape=jax.ShapeDtypeStruct((T_ALL, D_MODEL), BF16),
        grid=(D_MODEL // COL_TILE, T_ALL // MIX_ROW_TILE),
        in_specs=[pl.BlockSpec((MIX_ROW_TILE, ATTN_OUT), lambda c, i: (i, 0)),
                  pl.BlockSpec((MIX_ROW_TILE, D_RNN), lambda c, i: (i, 0)),
                  pl.BlockSpec((MIX_ROW_TILE, COL_TILE), lambda c, i: (i, COL_GATE_ATTN + c)),
                  pl.BlockSpec((MIX_ROW_TILE, COL_TILE), lambda c, i: (i, COL_GATE_RNN + c)),
                  pl.BlockSpec((None, ATTN_OUT, COL_TILE), lambda c, i: (0, 0, c)),
                  pl.BlockSpec((None, D_RNN, COL_TILE), lambda c, i: (0, 0, c))],
        out_specs=pl.BlockSpec((MIX_ROW_TILE, COL_TILE), lambda c, i: (i, c)),
        scratch_shapes=[pltpu.VMEM((ATTN_OUT, COL_TILE), BF16), pltpu.VMEM((D_RNN, COL_TILE), BF16)],
        compiler_params=_params(("arbitrary", "arbitrary")),
        name="mix",
    )(attn, y, p, p, w_attn_proj, w_rnn_proj)


def _out_proj_kernel(mix_ref, w_ref, x_ref, o_ref, wbf):
    @pl.when(pl.program_id(1) == 0)
    def _():
        wbf[...] = w_ref[...].astype(BF16)

    o_ref[...] = x_ref[...] + jnp.dot(mix_ref[...], wbf[...], preferred_element_type=F32)


def _out_proj(mix, w_out, x):
    return pl.pallas_call(
        _out_proj_kernel,
        out_shape=jax.ShapeDtypeStruct((T_ALL, D_MODEL), F32),
        grid=(D_MODEL // COL_TILE, T_ALL // ROW_TILE),
        in_specs=[pl.BlockSpec((ROW_TILE, D_MODEL), lambda n, i: (i, 0)),
                  pl.BlockSpec((None, D_MODEL, COL_TILE), lambda n, i: (0, 0, n)),
                  pl.BlockSpec((ROW_TILE, COL_TILE), lambda n, i: (i, n))],
        out_specs=pl.BlockSpec((ROW_TILE, COL_TILE), lambda n, i: (i, n)),
        scratch_shapes=[pltpu.VMEM((D_MODEL, COL_TILE), BF16)],
        compiler_params=_params(("arbitrary", "arbitrary")),
        name="out_proj",
    )(mix, w_out, x)


HALF_MODEL = D_MODEL // 2
HIGH_HALF_MASK = -65536


def _pack_bf16_pairs(x):
    hi = pltpu.bitcast(x[:, :HALF_MODEL].astype(BF16).astype(F32), I32)
    lo = pltpu.bitcast(x[:, HALF_MODEL:].astype(BF16).astype(F32), I32)
    return pltpu.bitcast(hi | lax.shift_right_logical(lo, 16), F32)


def _unpack_bf16_pairs(words):
    bits = pltpu.bitcast(words, I32)
    first = pltpu.bitcast(bits & HIGH_HALF_MASK, F32).astype(BF16)
    second = pltpu.bitcast(lax.shift_left(bits, 16), F32).astype(BF16)
    return first, second


def _split_bf16(x):
    hi = x.astype(BF16)
    lo = (x - hi.astype(F32)).astype(BF16)
    return hi, lo


def _router_kernel(x_ref, g_ref, w_ref, b_ref, xn_ref, sel_ref, gate_ref, rank_ref, cnt_ref, carry, tri):
    step = pl.program_id(0)

    @pl.when(step == 0)
    def _():
        carry[...] = jnp.zeros((1, LANES), F32)
        ri = lax.broadcasted_iota(I32, (NORM_TILE, NORM_TILE), 0)
        ci = lax.broadcasted_iota(I32, (NORM_TILE, NORM_TILE), 1)
        tri[...] = jnp.where(ci < ri, 1.0, 0.0).astype(BF16)

    x = x_ref[...]
    xn = x * lax.rsqrt(jnp.mean(x * x, axis=-1, keepdims=True) + EPS) * g_ref[...]
    xn_ref[...] = _pack_bf16_pairs(xn)

    x_hi, x_lo = _split_bf16(xn)
    w_hi, w_lo = _split_bf16(w_ref[...])
    logits = (jnp.dot(x_hi, w_hi, preferred_element_type=F32)
              + jnp.dot(x_hi, w_lo, preferred_element_type=F32)
              + jnp.dot(x_lo, w_hi, preferred_element_type=F32)) + b_ref[...]

    lane = lax.broadcasted_iota(I32, (NORM_TILE, LANES), 1)
    work = logits
    vals, idxs = [], []
    for _ in range(TOP_K):
        mk = jnp.max(work, axis=-1, keepdims=True)
        ik = jnp.min(jnp.where(work == mk, lane, LANES), axis=-1, keepdims=True)
        vals.append(mk)
        idxs.append(ik)
        work = jnp.where(lane == ik, -jnp.inf, work)
    exps = [jnp.exp(v - vals[0]) for v in vals]
    den = exps[0] + exps[1] + exps[2] + exps[3]

    member = jnp.zeros((NORM_TILE, LANES), F32)
    for ik in idxs:
        member = member + jnp.where(lane == ik, 1.0, 0.0)
    before = jnp.dot(tri[...], member.astype(BF16), preferred_element_type=F32) + carry[...]
    sel = jnp.zeros((NORM_TILE, LANES), I32)
    gates = jnp.zeros((NORM_TILE, LANES), F32)
    ranks = jnp.zeros((NORM_TILE, LANES), F32)
    for k in range(TOP_K):
        rk = jnp.sum(jnp.where(lane == idxs[k], before, 0.0), axis=-1, keepdims=True)
        sel = jnp.where(lane == k, idxs[k], sel)
        gates = jnp.where(lane == k, exps[k] / den, gates)
        ranks = jnp.where(lane == k, rk, ranks)
    sel_ref[...] = sel
    gate_ref[...] = gates
    rank_ref[...] = ranks.astype(I32)
    carry[...] = carry[...] + jnp.sum(member, axis=0, keepdims=True)
    cnt_ref[...] = carry[...].astype(I32)


def _router(x2, g_ffn, w_router_pad, b_router_pad):
    tile_spec = pl.BlockSpec((NORM_TILE, LANES), lambda i: (i, 0))
    return pl.pallas_call(
        _router_kernel,
        out_shape=[jax.ShapeDtypeStruct((T_ALL, HALF_MODEL), F32),
                   jax.ShapeDtypeStruct((T_ALL, LANES), I32),
                   jax.ShapeDtypeStruct((T_ALL, LANES), F32),
                   jax.ShapeDtypeStruct((T_ALL, LANES), I32),
                   jax.ShapeDtypeStruct((1, LANES), I32)],
        grid=(T_ALL // NORM_TILE,),
        in_specs=[pl.BlockSpec((NORM_TILE, D_MODEL), lambda i: (i, 0)),
                  pl.BlockSpec((1, D_MODEL), lambda i: (0, 0)),
                  pl.BlockSpec((D_MODEL, LANES), lambda i: (0, 0)),
                  pl.BlockSpec((1, LANES), lambda i: (0, 0))],
        out_specs=[pl.BlockSpec((NORM_TILE, HALF_MODEL), lambda i: (i, 0)),
                   tile_spec, tile_spec, tile_spec,
                   pl.BlockSpec((1, LANES), lambda i: (0, 0))],
        scratch_shapes=[pltpu.VMEM((1, LANES), F32), pltpu.VMEM((NORM_TILE, NORM_TILE), BF16)],
        compiler_params=_params(("arbitrary",)),
        name="router",
    )(x2, g_ffn, w_router_pad, b_router_pad)


def _experts_kernel(item_e, item_row0, item_nblk, used_blocks, *rest):
    xs_parts, rest = rest[:XS_PARTS], rest[XS_PARTS:]
    wg_ref, wu_ref, wd_ref, bg_ref, bu_ref, bd_ref = rest[:6]
    rest = rest[6:]
    caches, news = rest[:N_FUSED_SHIFT], rest[N_FUSED_SHIFT:2 * N_FUSED_SHIFT]
    rest = rest[2 * N_FUSED_SHIFT + N_ORDER_ONLY:]
    out_hbm = rest[0]
    cache_outs = rest[1:N_FUSED_SHIFT + 1]
    (xbuf, acc, wg_bf, wu_bf, wd_bf, sem_in, sem_out, sbuf, sem_shift_in,
     sem_shift_out) = rest[N_FUSED_SHIFT + 1:]
    i = pl.program_id(0)
    j = pl.program_id(1)
    n_j = pl.num_programs(1)
    nblk = item_nblk[i]

    _shift_step(i * n_j + j, caches, news, cache_outs, sbuf, sem_shift_in, sem_shift_out)

    @pl.when(jnp.logical_and(i == 0, j == 0))
    def _():
        acc[0:EXPERT_BLOCK, :] = jnp.zeros((EXPERT_BLOCK, D_MODEL), F32)

        def zero_copy(c):
            dst = pl.ds(pl.multiple_of((used_blocks[0] + c) * EXPERT_BLOCK, EXPERT_BLOCK), EXPERT_BLOCK)
            return pltpu.make_async_copy(acc.at[0:EXPERT_BLOCK, :], out_hbm.at[dst, :], sem_out)

        n_slack = N_SORTED_BLOCKS - used_blocks[0]
        pl.loop(0, n_slack)(lambda c: zero_copy(c).start())
        pl.loop(0, n_slack)(lambda c: zero_copy(c).wait())

    n_items = pl.num_programs(0)
    x_slot = i % 2

    def x_copy(item, c, action):
        rows = pl.ds(pl.multiple_of(c * EXPERT_BLOCK, EXPERT_BLOCK), EXPERT_BLOCK)
        block = item_row0[item] // EXPERT_BLOCK + c
        for part, xs_hbm in enumerate(xs_parts):
            @pl.when(block // XS_PART_BLOCKS == part)
            def _(part=part, xs_hbm=xs_hbm):
                src = pl.ds(pl.multiple_of((block - part * XS_PART_BLOCKS) * EXPERT_BLOCK, EXPERT_BLOCK),
                            EXPERT_BLOCK)
                cp = pltpu.make_async_copy(xs_hbm.at[src, :], xbuf.at[item % 2, rows, :], sem_in.at[item % 2])
                getattr(cp, action)()

    def out_copy(item, c):
        rows = pl.ds(pl.multiple_of(c * EXPERT_BLOCK, EXPERT_BLOCK), EXPERT_BLOCK)
        dst = pl.ds(pl.multiple_of(item_row0[item] + c * EXPERT_BLOCK, EXPERT_BLOCK), EXPERT_BLOCK)
        return pltpu.make_async_copy(acc.at[rows, :], out_hbm.at[dst, :], sem_out)

    @pl.when(j == 0)
    def _():
        @pl.when(i == 0)
        def _():
            pl.loop(0, nblk)(lambda c: x_copy(i, c, "start"))

        pl.loop(0, nblk)(lambda c: x_copy(i, c, "wait"))
        nxt = jnp.minimum(i + 1, n_items - 1)

        @pl.when(i + 1 < n_items)
        def _():
            pl.loop(0, item_nblk[nxt])(lambda c: x_copy(nxt, c, "start"))

    @pl.when(nblk > 0)
    def _():
        wg_bf[...] = wg_ref[...].astype(BF16)
        wu_bf[...] = wu_ref[...].astype(BF16)
        wd_bf[...] = wd_ref[...].astype(BF16)

    prev = jnp.maximum(i - 1, 0)

    @pl.when(jnp.logical_and(j == 0, i > 0))
    def _():
        pl.loop(0, item_nblk[prev])(lambda c: out_copy(prev, c).wait())

    @pl.when(nblk > 0)
    def _():
        @pl.when(j == 0)
        def _():
            @pl.loop(0, nblk)
            def _(c):
                rows = pl.ds(pl.multiple_of(c * EXPERT_BLOCK, EXPERT_BLOCK), EXPERT_BLOCK)
                acc[rows, :] = jnp.broadcast_to(bd_ref[...], (EXPERT_BLOCK, D_MODEL))

        def process(start, size):
            rows = pl.ds(pl.multiple_of(start, EXPERT_BLOCK), size)
            xa, xb = _unpack_bf16_pairs(xbuf[x_slot, rows, :])
            gt = (jnp.dot(xa, wg_bf[0:HALF_MODEL, :], preferred_element_type=F32)
                  + jnp.dot(xb, wg_bf[HALF_MODEL:D_MODEL, :], preferred_element_type=F32)) + bg_ref[...]
            up = (jnp.dot(xa, wu_bf[0:HALF_MODEL, :], preferred_element_type=F32)
                  + jnp.dot(xb, wu_bf[HALF_MODEL:D_MODEL, :], preferred_element_type=F32)) + bu_ref[...]
            gt = jnp.minimum(gt, SWIGLU_LIMIT)
            up = jnp.clip(up, -SWIGLU_LIMIT, SWIGLU_LIMIT)
            act = gt * jax.nn.sigmoid(SWIGLU_ALPHA * gt) * (up + 1.0)
            acc[rows, :] += jnp.dot(act.astype(BF16), wd_bf[...], preferred_element_type=F32)

        quad = 4 * EXPERT_BLOCK

        @pl.loop(0, nblk // 4)
        def _(c):
            process(c * quad, 2 * EXPERT_BLOCK)
            process(c * quad + 2 * EXPERT_BLOCK, 2 * EXPERT_BLOCK)

        tail = (nblk // 4) * quad

        @pl.when(nblk % 4 >= 2)
        def _():
            process(tail, 2 * EXPERT_BLOCK)

        @pl.when(nblk % 2 == 1)
        def _():
            process((nblk - 1) * EXPERT_BLOCK, EXPERT_BLOCK)

    @pl.when(j == n_j - 1)
    def _():
        pl.loop(0, nblk)(lambda c: out_copy(i, c).start())

        @pl.when(i == n_items - 1)
        def _():
            pl.loop(0, nblk)(lambda c: out_copy(i, c).wait())


def _experts(item_e, item_row0, item_nblk, used_blocks, xs_parts, w_gate, w_up, w_down, b_gate, b_up, b_down,
             caches, news, run_after):
    assert len(run_after) == N_ORDER_ONLY
    n_j = D_FF // FF_TILE
    assert N_WORK_ITEMS * n_j > SHIFT_TOTAL_STEPS

    def jj(i, j, nblk):
        return jnp.where(nblk[i] > 0, j, n_j - 1)

    any_spec = pl.BlockSpec(memory_space=pl.ANY)
    grid_spec = pltpu.PrefetchScalarGridSpec(
        num_scalar_prefetch=4,
        grid=(N_WORK_ITEMS, n_j),
        in_specs=[any_spec] * XS_PARTS + [
            pl.BlockSpec((None, None, D_MODEL, FF_TILE), lambda i, j, e, r, n, u: (0, e[i], 0, jj(i, j, n))),
            pl.BlockSpec((None, None, D_MODEL, FF_TILE), lambda i, j, e, r, n, u: (0, e[i], 0, jj(i, j, n))),
            pl.BlockSpec((None, None, FF_TILE, D_MODEL), lambda i, j, e, r, n, u: (0, e[i], jj(i, j, n), 0)),
            pl.BlockSpec((None, 1, FF_TILE), lambda i, j, e, r, n, u: (e[i], 0, jj(i, j, n))),
            pl.BlockSpec((None, 1, FF_TILE), lambda i, j, e, r, n, u: (e[i], 0, jj(i, j, n))),
            pl.BlockSpec((None, 1, D_MODEL), lambda i, j, e, r, n, u: (e[i], 0, 0)),
        ] + [any_spec] * N_FUSED_SHIFT + [pl.BlockSpec(memory_space=pltpu.VMEM)] * N_FUSED_SHIFT
          + [any_spec] * N_ORDER_ONLY,
        out_specs=[any_spec] * (1 + N_FUSED_SHIFT),
        scratch_shapes=[pltpu.VMEM((2, EXPERT_CAP, HALF_MODEL), F32),
                        pltpu.VMEM((EXPERT_CAP, D_MODEL), F32),
                        pltpu.VMEM((D_MODEL, FF_TILE), BF16),
                        pltpu.VMEM((D_MODEL, FF_TILE), BF16),
                        pltpu.VMEM((FF_TILE, D_MODEL), BF16),
                        pltpu.SemaphoreType.DMA((2,)),
                        pltpu.SemaphoreType.DMA(()),
                        pltpu.VMEM((SHIFT_SLOTS, SHIFT_CHUNK_ROWS, HEADS_PER_GROUP, HEAD_DIM), F32),
                        pltpu.SemaphoreType.DMA((SHIFT_SLOTS,)),
                        pltpu.SemaphoreType.DMA((SHIFT_SLOTS,))],
    )
    results = pl.pallas_call(
        _experts_kernel,
        out_shape=[jax.ShapeDtypeStruct((N_SORTED_ROWS, D_MODEL), F32)]
                  + [jax.ShapeDtypeStruct(c.shape, c.dtype) for c in caches],
        grid_spec=grid_spec,
        compiler_params=_params(("arbitrary", "arbitrary")),
        name="experts",
    )(item_e, item_row0, item_nblk, used_blocks, *xs_parts, w_gate, w_up, w_down, b_gate, b_up, b_down,
      *caches, *news, *run_after)
    return results[0], results[1:]


def _combine_kernel(x_ref, gate_ref, o0, o1, o2, o3, *rest):
    yp_ref, ys_ref = rest[N_ORDER_ONLY:]
    gates = gate_ref[...]
    y = x_ref[...]
    for k, o_ref in enumerate((o0, o1, o2, o3)):
        y = y + gates[:, k:k + 1] * o_ref[...]
    is_prompt = pl.program_id(0) < SEQ // COMBINE_TILE

    @pl.when(is_prompt)
    def _():
        yp_ref[...] = y

    @pl.when(jnp.logical_not(is_prompt))
    def _():
        ys_ref[...] = y


def _combine(x2, gates, outs, run_after):
    assert DEC_BATCH == COMBINE_TILE and len(run_after) == N_ORDER_ONLY
    n_prompt = SEQ // COMBINE_TILE
    row = pl.BlockSpec((COMBINE_TILE, D_MODEL), lambda i: (i, 0))
    return pl.pallas_call(
        _combine_kernel,
        out_shape=[jax.ShapeDtypeStruct((SEQ, D_MODEL), F32), jax.ShapeDtypeStruct((DEC_BATCH, D_MODEL), F32)],
        grid=(n_prompt + 1,),
        in_specs=[row, pl.BlockSpec((COMBINE_TILE, LANES), lambda i: (i, 0)), row, row, row, row]
                 + [pl.BlockSpec(memory_space=pl.ANY)] * N_ORDER_ONLY,
        out_specs=[pl.BlockSpec((COMBINE_TILE, D_MODEL), lambda i: (jnp.minimum(i, n_prompt - 1), 0)),
                   pl.BlockSpec((COMBINE_TILE, D_MODEL), lambda i: (0, 0))],
        compiler_params=_params(("arbitrary",)),
        name="combine",
    )(x2, gates, *outs, *run_after)


def _rope_tables():
    half = ROT_DIM // 2
    inv = ROPE_THETA ** (-2.0 * jnp.arange(half, dtype=F32) / ROT_DIM)
    pos = jnp.concatenate([jnp.arange(SEQ), jnp.full((DEC_BATCH,), PAST_LEN)]).astype(F32)
    ang = pos[:, None] * inv[None, :]
    cos, sin = jnp.cos(ang), jnp.sin(ang)
    rest = HEAD_DIM - ROT_DIM
    cos_t = jnp.concatenate([cos, cos, jnp.ones((T_ALL, rest), F32)], axis=-1)
    sin_t = jnp.concatenate([-sin, sin, jnp.zeros((T_ALL, rest), F32)], axis=-1)
    return cos_t, sin_t


def _dispatch_plan(sel, ranks, counts):
    nb = (counts + EXPERT_BLOCK - 1) // EXPERT_BLOCK
    padded = nb * EXPERT_BLOCK
    pad_start = jnp.cumsum(padded) - padded
    dest = pad_start[sel] + ranks
    n_items = (nb + EXPERT_CAP_BLOCKS - 1) // EXPERT_CAP_BLOCKS
    item_end = jnp.cumsum(n_items)
    item_start = item_end - n_items
    w = jnp.arange(N_WORK_ITEMS, dtype=I32)
    total = item_end[-1]
    w_eff = jnp.minimum(w, total - 1)
    e_w = jnp.minimum(jnp.searchsorted(item_end, w_eff, side='right'), N_EXPERTS - 1).astype(I32)
    k_w = w_eff - item_start[e_w]
    row0 = pad_start[e_w] + k_w * EXPERT_CAP
    nblk = jnp.clip(nb[e_w] - k_w * EXPERT_CAP_BLOCKS, 0, EXPERT_CAP_BLOCKS)
    nblk = jnp.where(w < total, nblk, 0)
    used_blocks = jnp.sum(nb).astype(I32).reshape(1)
    return dest, e_w, row0.astype(I32), nblk.astype(I32), used_blocks


def kernel(x_prompt, x_sample, cache_k_w128, cache_v_w128, cache_k_w512, cache_v_w512, cache_k_w2048,
           cache_v_w2048, state_conv, state_rglru, g_mix, w_in, q_norm, k_norm, w_attn_proj, conv_w, conv_b,
           w_rg_a, b_rg_a, w_rg_x, b_rg_x, lru_lambda, w_rnn_proj, w_out, g_ffn, w_router, b_router,
           w_gate, b_gate, w_up, b_up, w_down, b_down):
    caches = (cache_k_w128, cache_v_w128, cache_k_w512, cache_v_w512, cache_k_w2048, cache_v_w2048)
    x = jnp.concatenate([x_prompt[0], x_sample[:, 0]], axis=0)

    xn = _rmsnorm_bf16(x, g_mix)
    cos_t, sin_t = _rope_tables()
    gains = jnp.stack([q_norm, k_norm])
    p = _in_proj(xn, w_in, cos_t, sin_t, gains)

    attn_p = _attn_prompt(p)
    qkv_s = p[SEQ:, :3 * ATTN_WIDTH].reshape(DEC_BATCH, 3, N_HEADS, HEAD_DIM)
    q_s, k_s, v_s = qkv_s[:, 0], qkv_s[:, 1], qkv_s[:, 2]
    views = [c.reshape(DEC_BATCH, ATTN_BLOCK, DILATIONS[n // 2], HEADS_PER_GROUP, HEAD_DIM)
             for n, c in enumerate(caches)]
    attn_s = _attn_sample(q_s, k_s, v_s, views).reshape(DEC_BATCH, ATTN_OUT)
    news = []
    for g in range(N_GROUPS):
        heads = slice(g * HEADS_PER_GROUP, (g + 1) * HEADS_PER_GROUP)
        news += [k_s[:, heads], v_s[:, heads]]
    attn = jnp.concatenate([attn_p, attn_s], axis=0)

    y_p, h_p = _rglru_prompt(p, conv_w, conv_b, w_rg_a, b_rg_a, w_rg_x, b_rg_x, lru_lambda)
    hist = jnp.transpose(state_conv[0], (1, 0, 2))
    y_s, h_s = _rglru_sample(p, hist, state_rglru[0], conv_w, conv_b, w_rg_a, b_rg_a, w_rg_x, b_rg_x,
                             lru_lambda)
    y = jnp.concatenate([y_p, y_s], axis=0)

    mix = _mix(attn, y, p, w_attn_proj, w_rnn_proj)
    x2 = _out_proj(mix, w_out, x)

    w_router_pad = jnp.pad(w_router[0], ((0, 0), (0, LANES - N_EXPERTS)))
    b_router_pad = jnp.pad(b_router, ((0, 0), (0, LANES - N_EXPERTS)), constant_values=NEG_BIG)
    xn2, sel, gates, ranks, counts = _router(x2, g_ffn, w_router_pad, b_router_pad)
    dest, item_e, item_row0, item_nblk, used_blocks = _dispatch_plan(
        sel[:, :TOP_K], ranks[:, :TOP_K], counts[0, :N_EXPERTS])
    tok = jnp.repeat(jnp.arange(T_ALL, dtype=I32), TOP_K)
    src_tok = jnp.zeros((N_SORTED_ROWS,), I32).at[dest.reshape(-1)].set(
        tok, unique_indices=True, mode="promise_in_bounds")
    part_rows = XS_PART_BLOCKS * EXPERT_BLOCK
    xs_parts = [xn2[src_tok[k * part_rows:(k + 1) * part_rows]] for k in range(XS_PARTS)]
    mid_caches = _cache_shift(caches[2:4], news[2:4], counts, "cache_shift_mid")
    out_sorted, big_caches = _experts(item_e, item_row0, item_nblk, used_blocks, xs_parts, w_gate, w_up, w_down,
                                      b_gate.reshape(N_EXPERTS, 1, D_FF), b_up.reshape(N_EXPERTS, 1, D_FF),
                                      b_down.reshape(N_EXPERTS, 1, D_MODEL), caches[4:], news[4:], mid_caches)
    small_caches = _cache_shift(caches[:2], news[:2], out_sorted, "cache_shift_small")
    new_caches = (*small_caches, *mid_caches, *big_caches)
    outs = [out_sorted[dest[:, k]] for k in range(TOP_K)]
    y_p, y_s = _combine(x2, gates, outs, small_caches)

    y_prompt = y_p.reshape(1, SEQ, D_MODEL)
    y_sample = y_s.reshape(DEC_BATCH, 1, D_MODEL)
    states_p = []
    for g, w in enumerate(WINDOWS):
        keep = min(w, SEQ)
        for off in (ATTN_WIDTH, 2 * ATTN_WIDTH):
            c0 = off + g * ATTN_OUT
            states_p.append(p[SEQ - keep:SEQ, c0:c0 + ATTN_OUT].reshape(1, 1, keep, HEADS_PER_GROUP, HEAD_DIM))
    conv_p = p[SEQ - (CONV_W - 1):SEQ, 3 * ATTN_WIDTH:3 * ATTN_WIDTH + D_RNN].reshape(1, 1, CONV_W - 1, D_RNN)
    rglru_p = h_p.reshape(1, 1, D_RNN)
    rnn_x_s = p[SEQ:, 3 * ATTN_WIDTH:3 * ATTN_WIDTH + D_RNN]
    conv_s = jnp.concatenate([state_conv[0][:, 1:], rnn_x_s[:, None, :]], axis=1)[None]
    rglru_s = h_s[None]
    return (y_prompt, y_sample, *states_p, conv_p, rglru_p, *new_caches, conv_s, rglru_s)
```

```python
import functools

import jax
import jax.numpy as jnp
from jax import lax
from jax.experimental import pallas as pl
from jax.experimental.pallas import tpu as pltpu

F32 = jnp.float32
BF16 = jnp.bfloat16
I32 = jnp.int32

D_MODEL = 2048
SEQ = 8192
DEC_BATCH = 128
PAST_LEN = 2048
T_ALL = SEQ + DEC_BATCH

HEAD_DIM = 128
HEADS_PER_GROUP = 4
WINDOWS = (128, 512, 2048)
DILATIONS = (1, 4, 16)
N_GROUPS = 3
N_HEADS = N_GROUPS * HEADS_PER_GROUP
ATTN_WIDTH = N_HEADS * HEAD_DIM
ATTN_OUT = HEADS_PER_GROUP * HEAD_DIM
ATTN_BLOCK = 128
ROT_DIM = HEAD_DIM // 4
ROPE_THETA = 500000.0
D_RNN = D_MODEL
RNN_BLOCK_W = 128
CONV_W = 4
LRU_C = 8.0
N_EXPERTS = 32
TOP_K = 4
D_FF = D_MODEL
SWIGLU_LIMIT = 7.0
SWIGLU_ALPHA = 1.702
EPS = 1e-6
IN_WIDTH = 3 * ATTN_WIDTH + 2 * D_RNN + 2 * D_MODEL

LANES = 128
SUBLANES = 8
VMEM_LIMIT_BYTES = 56 * 1024 * 1024

COL_TILE = 512
COL_RNN_X = (3 * ATTN_WIDTH) // COL_TILE
COL_RNN_GATE = (3 * ATTN_WIDTH + D_RNN) // COL_TILE
COL_GATE_ATTN = (3 * ATTN_WIDTH + 2 * D_RNN) // COL_TILE
COL_GATE_RNN = (3 * ATTN_WIDTH + 2 * D_RNN + D_MODEL) // COL_TILE

ROW_TILE = 1664
MIX_ROW_TILE = 832
NORM_TILE = 640
COMBINE_TILE = 128
SPAN = 2048
NEG_BIG = -1e30

EXPERT_BLOCK = 128
EXPERT_CAP_BLOCKS = 12
EXPERT_CAP = EXPERT_BLOCK * EXPERT_CAP_BLOCKS
FF_TILE = 256
N_SLOTS = T_ALL * TOP_K
N_SORTED_ROWS = N_SLOTS + N_EXPERTS * EXPERT_BLOCK
N_SORTED_BLOCKS = N_SORTED_ROWS // EXPERT_BLOCK
N_WORK_ITEMS = N_EXPERTS + -(-N_SORTED_BLOCKS // EXPERT_CAP_BLOCKS)
XS_PARTS = 4
XS_PART_BLOCKS = N_SORTED_BLOCKS // XS_PARTS
assert XS_PART_BLOCKS * XS_PARTS == N_SORTED_BLOCKS


def _params(semantics, vmem=VMEM_LIMIT_BYTES):
    return pltpu.CompilerParams(dimension_semantics=semantics, vmem_limit_bytes=vmem)


def _rmsnorm_kernel(x_ref, g_ref, o_ref):
    x = x_ref[...]
    y = x * lax.rsqrt(jnp.mean(x * x, axis=-1, keepdims=True) + EPS)
    o_ref[...] = (y * g_ref[...]).astype(o_ref.dtype)


def _rmsnorm_bf16(x, g):
    t = x.shape[0]
    return pl.pallas_call(
        _rmsnorm_kernel,
        out_shape=jax.ShapeDtypeStruct((t, D_MODEL), BF16),
        grid=(t // NORM_TILE,),
        in_specs=[pl.BlockSpec((NORM_TILE, D_MODEL), lambda i: (i, 0)),
                  pl.BlockSpec((1, D_MODEL), lambda i: (0, 0))],
        out_specs=pl.BlockSpec((NORM_TILE, D_MODEL), lambda i: (i, 0)),
        compiler_params=_params(("parallel",)),
        name="rmsnorm_bf16",
    )(x, g)


def _in_proj_kernel(x_ref, w_ref, cos_ref, sin_ref, gain_ref, o_ref, wbf_ref):
    j = pl.program_id(0)

    @pl.when(pl.program_id(1) == 0)
    def _():
        wbf_ref[...] = w_ref[...].astype(BF16)

    o_ref[...] = jnp.dot(x_ref[...], wbf_ref[...], preferred_element_type=F32)

    @pl.when(j < 2 * ATTN_WIDTH // COL_TILE)
    def _():
        lane = lax.broadcasted_iota(I32, (ROW_TILE, HEAD_DIM), 1)
        first_half = lane < ROT_DIM // 2
        gain = gain_ref[...]
        cos = cos_ref[...]
        sin = sin_ref[...]
        for h in range(COL_TILE // HEAD_DIM):
            cols = slice(h * HEAD_DIM, (h + 1) * HEAD_DIM)
            xh = o_ref[:, cols]
            y = xh * lax.rsqrt(jnp.mean(xh * xh, axis=-1, keepdims=True) + EPS) * gain
            partner = jnp.where(first_half,
                                pltpu.roll(y, HEAD_DIM - ROT_DIM // 2, 1),
                                pltpu.roll(y, ROT_DIM // 2, 1))
            o_ref[:, cols] = y * cos + partner * sin


def _in_proj(xn, w_in, cos_t, sin_t, gains):
    n_tiles = IN_WIDTH // COL_TILE
    qk_tiles = ATTN_WIDTH // COL_TILE
    return pl.pallas_call(
        _in_proj_kernel,
        out_shape=jax.ShapeDtypeStruct((T_ALL, IN_WIDTH), F32),
        grid=(n_tiles, T_ALL // ROW_TILE),
        in_specs=[
            pl.BlockSpec((ROW_TILE, D_MODEL), lambda j, i: (i, 0)),
            pl.BlockSpec((None, D_MODEL, COL_TILE), lambda j, i: (0, 0, j)),
            pl.BlockSpec((ROW_TILE, HEAD_DIM), lambda j, i: (i, 0)),
            pl.BlockSpec((ROW_TILE, HEAD_DIM), lambda j, i: (i, 0)),
            pl.BlockSpec((None, 1, HEAD_DIM), lambda j, i: (jnp.minimum(j // qk_tiles, 1), 0, 0)),
        ],
        out_specs=pl.BlockSpec((ROW_TILE, COL_TILE), lambda j, i: (i, j)),
        scratch_shapes=[pltpu.VMEM((D_MODEL, COL_TILE), BF16)],
        compiler_params=_params(("arbitrary", "arbitrary")),
        name="in_proj",
    )(xn, w_in, cos_t, sin_t, gains)


def _dot_nt(a, b):
    return lax.dot_general(a, b, (((1,), (1,)), ((), ())), preferred_element_type=F32)


def _attn_prompt_kernel(*refs):
    ins = refs[:15]
    o_ref = refs[15]
    og_ref, lse_ref = refs[16], refs[17]
    span_idx = pl.program_id(0)
    n_sub = SPAN // ATTN_BLOCK
    shape3 = (n_sub, ATTN_BLOCK, ATTN_BLOCK)
    sub = lax.broadcasted_iota(I32, shape3, 0)
    qi = lax.broadcasted_iota(I32, shape3, 1)
    kj = lax.broadcasted_iota(I32, shape3, 2)
    mask_cur = qi >= kj
    mask_prev_band = kj >= qi
    scale = HEAD_DIM ** -0.5

    def bmm_nt(a, b):
        return lax.dot_general(a, b, (((2,), (2,)), ((0,), (0,))), preferred_element_type=F32)

    def bmm(a, b):
        return lax.dot_general(a, b, (((2,), (1,)), ((0,), (0,))), preferred_element_type=F32)

    for g, dil in enumerate(DILATIONS):
        q_ref, kc_ref, vc_ref, kp_ref, vp_ref = ins[5 * g:5 * g + 5]
        blk = ATTN_BLOCK * dil
        blocks = [(m, r) for m in range(SPAN // blk) for r in range(dil)]

        def rows_of(m, r, dil=dil, blk=blk):
            return pl.ds(m * blk + r, ATTN_BLOCK, stride=dil) if dil > 1 else pl.ds(m * blk, ATTN_BLOCK)

        q3 = jnp.stack([(q_ref[rows_of(m, r), :] * scale).astype(BF16) for m, r in blocks])
        kc = [kc_ref[rows_of(m, r), :].astype(BF16) for m, r in blocks]
        vc = [vc_ref[rows_of(m, r), :].astype(BF16) for m, r in blocks]
        kp = [kp_ref[rows_of(0, r), :].astype(BF16) if m == 0 else kc[(m - 1) * dil + r] for m, r in blocks]
        vp = [vp_ref[rows_of(0, r), :].astype(BF16) if m == 0 else vc[(m - 1) * dil + r] for m, r in blocks]
        mask_prev = jnp.logical_and(mask_prev_band, jnp.logical_or(sub >= dil, span_idx > 0))
        s_cur = jnp.where(mask_cur, bmm_nt(q3, jnp.stack(kc)), NEG_BIG)
        s_prev = jnp.where(mask_prev, bmm_nt(q3, jnp.stack(kp)), NEG_BIG)
        mx = jnp.maximum(jnp.max(s_cur, axis=-1, keepdims=True), jnp.max(s_prev, axis=-1, keepdims=True))
        p_cur = jnp.exp(s_cur - mx)
        p_prev = jnp.exp(s_prev - mx)
        den = jnp.sum(p_cur, axis=-1, keepdims=True) + jnp.sum(p_prev, axis=-1, keepdims=True)
        pv = bmm(p_cur.astype(BF16), jnp.stack(vc)) + bmm(p_prev.astype(BF16), jnp.stack(vp))
        out3 = pv / den
        lse3 = jnp.broadcast_to(mx + jnp.log(den), (n_sub, ATTN_BLOCK, HEAD_DIM))
        for n, (m, r) in enumerate(blocks):
            og_ref[g, rows_of(m, r), :] = out3[n]
            lse_ref[g, rows_of(m, r), :] = lse3[n]

    lse = lse_ref[...]
    top = jnp.max(lse, axis=0)
    w = jnp.exp(lse - top[None])
    o_ref[...] = jnp.sum(w * og_ref[...], axis=0) / jnp.sum(w, axis=0)


def _attn_prompt(p):
    q_cols, k_cols, v_cols = 0, N_HEADS, 2 * N_HEADS
    in_specs = []
    for g, dil in enumerate(DILATIONS):
        blk = ATTN_BLOCK * dil
        per_span = SPAN // blk

        def cur(off, g=g):
            return pl.BlockSpec((SPAN, HEAD_DIM), lambda s, h: (s, off + g * HEADS_PER_GROUP + h))

        def prev(off, g=g, blk=blk, per_span=per_span):
            return pl.BlockSpec((blk, HEAD_DIM),
                                lambda s, h: (jnp.maximum(s * per_span - 1, 0), off + g * HEADS_PER_GROUP + h))

        in_specs += [cur(q_cols), cur(k_cols), cur(v_cols), prev(k_cols), prev(v_cols)]
    return pl.pallas_call(
        _attn_prompt_kernel,
        out_shape=jax.ShapeDtypeStruct((SEQ, ATTN_OUT), F32),
        grid=(SEQ // SPAN, HEADS_PER_GROUP),
        in_specs=in_specs,
        out_specs=pl.BlockSpec((SPAN, HEAD_DIM), lambda s, h: (s, h)),
        scratch_shapes=[pltpu.VMEM((N_GROUPS, SPAN, HEAD_DIM), F32),
                        pltpu.VMEM((N_GROUPS, SPAN, HEAD_DIM), F32)],
        compiler_params=_params(("parallel", "parallel")),
        name="attn_prompt",
    )(*([p] * 15))


SAMPLE_BB = 8


def _attn_sample_kernel(q_ref, k_ref, v_ref, ck0, cv0, ck1, cv1, ck2, cv2, o_ref):
    caches = ((ck0, cv0), (ck1, cv1), (ck2, cv2))
    scale = HEAD_DIM ** -0.5
    for b in range(SAMPLE_BB):
        outs, lses = [], []
        for g in range(N_GROUPS):
            heads = slice(g * HEADS_PER_GROUP, (g + 1) * HEADS_PER_GROUP)
            q = q_ref[b, heads, :] * scale
            k_new = k_ref[b, heads, :]
            v_new = v_ref[b, heads, :]
            k_old = caches[g][0][b]
            v_old = caches[g][1][b]
            s_old = jnp.sum(k_old * q[None], axis=-1, keepdims=True)
            s_new = jnp.sum(k_new * q, axis=-1, keepdims=True)
            mx = jnp.maximum(jnp.max(s_old, axis=0), s_new)
            p_old = jnp.exp(s_old - mx[None])
            p_new = jnp.exp(s_new - mx)
            den = jnp.sum(p_old, axis=0) + p_new
            pv = jnp.sum(p_old * v_old, axis=0) + p_new * v_new
            outs.append(pv / den)
            lses.append(mx + jnp.log(den))
        top = jnp.maximum(jnp.maximum(lses[0], lses[1]), lses[2])
        ws = [jnp.exp(l - top) for l in lses]
        o_ref[b] = (ws[0] * outs[0] + ws[1] * outs[1] + ws[2] * outs[2]) / (ws[0] + ws[1] + ws[2])


def _attn_sample(q_s, k_s, v_s, cache_views):
    row = pl.BlockSpec((SAMPLE_BB, N_HEADS, HEAD_DIM), lambda b: (b, 0, 0))
    cache_spec = pl.BlockSpec((SAMPLE_BB, ATTN_BLOCK, None, HEADS_PER_GROUP, HEAD_DIM),
                              lambda b: (b, 0, 0, 0, 0))
    return pl.pallas_call(
        _attn_sample_kernel,
        out_shape=jax.ShapeDtypeStruct((DEC_BATCH, HEADS_PER_GROUP, HEAD_DIM), F32),
        grid=(DEC_BATCH // SAMPLE_BB,),
        in_specs=[row, row, row] + [cache_spec] * 6,
        out_specs=pl.BlockSpec((SAMPLE_BB, HEADS_PER_GROUP, HEAD_DIM), lambda b: (b, 0, 0)),
        compiler_params=_params(("parallel",)),
        name="attn_sample",
    )(q_s, k_s, v_s, *cache_views)


SHIFT_CHUNK_ROWS = 1024
SHIFT_CHUNKS_PER_STEP = 2
SHIFT_SLOTS = 2 * SHIFT_CHUNKS_PER_STEP
SHIFT_DMA_QUEUE = 1
FUSED_SHIFT_WINDOWS = (WINDOWS[2], WINDOWS[2])
N_FUSED_SHIFT = len(FUSED_SHIFT_WINDOWS)
N_ORDER_ONLY = 2
STANDALONE_SHIFT_ROWS = 2048


def _shift_steps(window):
    return DEC_BATCH * window // (SHIFT_CHUNK_ROWS * SHIFT_CHUNKS_PER_STEP)


def _shift_schedule():
    spans, lo = [], 0
    for w in FUSED_SHIFT_WINDOWS:
        spans.append((lo, lo + _shift_steps(w)))
        lo += _shift_steps(w)
    return spans


SHIFT_TOTAL_STEPS = _shift_schedule()[-1][1]


def _shift_caches_together(caches, news, outs):
    n = len(caches)
    w = caches[0].shape[2]
    assert all(c.shape == caches[0].shape for c in caches)
    bb = STANDALONE_SHIFT_ROWS // w
    n_chunks = DEC_BATCH // bb

    def run(buf, sem_in, sem_out):
        def in_copy(a, c, slot):
            return pltpu.make_async_copy(caches[a].at[0, pl.ds(c * bb, bb), pl.ds(1, w - 1)],
                                         buf.at[a, slot, :, pl.ds(0, w - 1)], sem_in.at[a, slot])

        def out_copy(a, c, slot):
            return pltpu.make_async_copy(buf.at[a, slot], outs[a].at[0, pl.ds(c * bb, bb)], sem_out.at[a, slot])

        for a in range(n):
            in_copy(a, 0, 0).start()

        @pl.loop(0, n_chunks)
        def _(c):
            slot = c % 2
            other = 1 - slot

            @pl.when(c + 1 < n_chunks)
            def _():
                @pl.when(c >= 1)
                def _():
                    for a in range(n):
                        out_copy(a, c - 1, other).wait()

                for a in range(n):
                    in_copy(a, c + 1, other).start()

            for a in range(n):
                in_copy(a, c, slot).wait()
                buf[a, slot, :, w - 1] = news[a][pl.ds(c * bb, bb)]
                out_copy(a, c, slot).start()

        for a in range(n):
            out_copy(a, n_chunks - 2, n_chunks % 2).wait()
            out_copy(a, n_chunks - 1, (n_chunks - 1) % 2).wait()

    pl.run_scoped(run, pltpu.VMEM((n, 2, bb, w, HEADS_PER_GROUP, HEAD_DIM), F32),
                  pltpu.SemaphoreType.DMA((n, 2)), pltpu.SemaphoreType.DMA((n, 2)))


def _cache_shift_kernel(n, *refs):
    caches, news, outs = refs[:n], refs[n:2 * n], refs[len(refs) - n:]
    _shift_caches_together(caches, news, outs)


def _cache_shift(caches, news, after, name):
    n = len(caches)
    any_spec = pl.BlockSpec(memory_space=pl.ANY)
    return pl.pallas_call(
        functools.partial(_cache_shift_kernel, n),
        out_shape=[jax.ShapeDtypeStruct(c.shape, c.dtype) for c in caches],
        in_specs=[any_spec] * n + [pl.BlockSpec(memory_space=pltpu.VMEM)] * n + [any_spec],
        out_specs=[any_spec] * n,
        compiler_params=pltpu.CompilerParams(vmem_limit_bytes=VMEM_LIMIT_BYTES),
        name=name,
    )(*caches, *news, after)


def _shift_chunk_copies(cache, out, sbuf, sem_in, sem_out, local_step, p, slot):
    w = cache.shape[2]
    ins, outs, new_rows = [], [], []
    copy = functools.partial(functools.partial, pltpu.make_async_copy)
    if w <= SHIFT_CHUNK_ROWS:
        per_chunk = SHIFT_CHUNK_ROWS // w
        b0 = (local_step * SHIFT_CHUNKS_PER_STEP + p) * per_chunk
        for bl in range(per_chunk):
            ins.append(copy(cache.at[0, b0 + bl, pl.ds(1, w - 1)], sbuf.at[slot, pl.ds(bl * w, w - 1)],
                            sem_in.at[slot]))
            outs.append(copy(sbuf.at[slot, pl.ds(bl * w, w)], out.at[0, b0 + bl], sem_out.at[slot]))
            new_rows.append((bl * w + w - 1, b0 + bl))
    else:
        assert w == SHIFT_CHUNK_ROWS * SHIFT_CHUNKS_PER_STEP
        last = p == SHIFT_CHUNKS_PER_STEP - 1
        n_in = SHIFT_CHUNK_ROWS - 1 if last else SHIFT_CHUNK_ROWS
        ins.append(copy(cache.at[0, local_step, pl.ds(1 + p * SHIFT_CHUNK_ROWS, n_in)],
                        sbuf.at[slot, pl.ds(0, n_in)], sem_in.at[slot]))
        outs.append(copy(sbuf.at[slot], out.at[0, local_step, pl.ds(p * SHIFT_CHUNK_ROWS, SHIFT_CHUNK_ROWS)],
                         sem_out.at[slot]))
        if last:
            new_rows.append((SHIFT_CHUNK_ROWS - 1, local_step))
    return ins, outs, new_rows


def _shift_step(step, caches, news, outs, sbuf, sem_in, sem_out):
    spans = _shift_schedule()

    def for_step(t, fn):
        for a, (lo, hi) in enumerate(spans):
            @pl.when(jnp.logical_and(t >= lo, t < hi))
            def _(a=a, lo=lo):
                for p in range(SHIFT_CHUNKS_PER_STEP):
                    slot = (t % 2) * SHIFT_CHUNKS_PER_STEP + p
                    ins, outs_, new_rows = _shift_chunk_copies(caches[a], outs[a], sbuf, sem_in, sem_out,
                                                               t - lo, p, slot)
                    fn(a, slot, ins, outs_, new_rows)

    def drain(a, slot, ins, outs_, new_rows):
        for make in outs_:
            make().wait()

    def prefetch(a, slot, ins, outs_, new_rows):
        for make in ins:
            make().start(priority=SHIFT_DMA_QUEUE)

    def forward(a, slot, ins, outs_, new_rows):
        for make in ins:
            make().wait()
        for row, b in new_rows:
            sbuf[slot, row] = news[a][b]
        for make in outs_:
            make().start(priority=SHIFT_DMA_QUEUE)

    @pl.when(step == 0)
    def _():
        for_step(step, prefetch)

    for_step(step - 1, drain)
    for_step(step + 1, prefetch)
    for_step(step, forward)


def _rglru_gates(xc, wa_ref, ba_ref, wx_ref, bx_ref, lam_ref):
    r_parts, i_parts = [], []
    for n in range(COL_TILE // RNN_BLOCK_W):
        xb = xc[:, n * RNN_BLOCK_W:(n + 1) * RNN_BLOCK_W].astype(BF16)
        r_parts.append(jnp.dot(xb, wa_ref[n].astype(BF16), preferred_element_type=F32))
        i_parts.append(jnp.dot(xb, wx_ref[n].astype(BF16), preferred_element_type=F32))
    r = jax.nn.sigmoid(jnp.concatenate(r_parts, axis=-1) + ba_ref[...])
    i = jax.nn.sigmoid(jnp.concatenate(i_parts, axis=-1) + bx_ref[...])
    neg_lam = -lam_ref[...]
    softplus = jnp.maximum(neg_lam, 0.0) + jnp.log1p(jnp.exp(-jnp.abs(neg_lam)))
    log_a = -LRU_C * r * softplus
    a = jnp.exp(log_a)
    b = jnp.sqrt(-jnp.tanh(log_a) * (jnp.exp(2.0 * log_a) + 1.0)) * i * xc
    return a, b


RNN_T_TILE = 512
CONV_PAD = SUBLANES


def _rglru_prompt_kernel(x_ref, gate_ref, cw_ref, cb_ref, wa_ref, ba_ref, wx_ref, bx_ref, lam_ref,
                         y_ref, hlast_ref, xbuf, a_scr, b_scr, h_scr, carry):
    t = pl.program_id(1)

    @pl.when(t == 0)
    def _():
        xbuf[0:CONV_PAD, :] = jnp.zeros((CONV_PAD, COL_TILE), F32)
        carry[...] = jnp.zeros((SUBLANES, COL_TILE), F32)

    @pl.when(t > 0)
    def _():
        xbuf[0:CONV_PAD, :] = xbuf[RNN_T_TILE:RNN_T_TILE + CONV_PAD, :]

    xbuf[CONV_PAD:CONV_PAD + RNN_T_TILE, :] = x_ref[...]
    xc = cb_ref[...] + sum(
        xbuf[pl.ds(CONV_PAD - (CONV_W - 1) + j, RNN_T_TILE), :] * cw_ref[j:j + 1, :] for j in range(CONV_W))
    a, b = _rglru_gates(xc, wa_ref, ba_ref, wx_ref, bx_ref, lam_ref)
    a_scr[...] = a
    b_scr[...] = b

    row = lax.broadcasted_iota(I32, (SUBLANES, COL_TILE), 0)

    def chunk(c, h):
        rows = pl.ds(pl.multiple_of(c * SUBLANES, SUBLANES), SUBLANES)
        ac = a_scr[rows, :]
        bc = b_scr[rows, :]
        for s in (1, 2, 4):
            a_sh = jnp.where(row >= s, pltpu.roll(ac, s, 0), 1.0)
            b_sh = jnp.where(row >= s, pltpu.roll(bc, s, 0), 0.0)
            bc = ac * b_sh + bc
            ac = ac * a_sh
        hh = ac * h + bc
        h_scr[rows, :] = hh
        return jnp.broadcast_to(hh[SUBLANES - 1:SUBLANES, :], (SUBLANES, COL_TILE))

    h_end = lax.fori_loop(0, RNN_T_TILE // SUBLANES, chunk, carry[...], unroll=4)
    carry[...] = h_end
    hlast_ref[...] = h_end[0:1, :]
    y_ref[...] = (h_scr[...] * jax.nn.gelu(gate_ref[...])).astype(y_ref.dtype)


def _rnn_param_specs(idx):
    return [
        pl.BlockSpec((None, CONV_W, COL_TILE), lambda *g: (0, 0, idx(*g))),
        pl.BlockSpec((1, COL_TILE), lambda *g: (0, idx(*g))),
        pl.BlockSpec((None, COL_TILE // RNN_BLOCK_W, RNN_BLOCK_W, RNN_BLOCK_W), lambda *g: (0, idx(*g), 0, 0)),
        pl.BlockSpec((1, COL_TILE), lambda *g: (0, idx(*g))),
        pl.BlockSpec((None, COL_TILE // RNN_BLOCK_W, RNN_BLOCK_W, RNN_BLOCK_W), lambda *g: (0, idx(*g), 0, 0)),
        pl.BlockSpec((1, COL_TILE), lambda *g: (0, idx(*g))),
        pl.BlockSpec((1, COL_TILE), lambda *g: (0, idx(*g))),
    ]


def _rglru_prompt(p, conv_w, conv_b, w_rg_a, b_rg_a, w_rg_x, b_rg_x, lru_lambda):
    n_c = D_RNN // COL_TILE
    return pl.pallas_call(
        _rglru_prompt_kernel,
        out_shape=[jax.ShapeDtypeStruct((SEQ, D_RNN), BF16), jax.ShapeDtypeStruct((1, D_RNN), F32)],
        grid=(n_c, SEQ // RNN_T_TILE),
        in_specs=[pl.BlockSpec((RNN_T_TILE, COL_TILE), lambda c, t: (t, COL_RNN_X + c)),
                  pl.BlockSpec((RNN_T_TILE, COL_TILE), lambda c, t: (t, COL_RNN_GATE + c))]
                 + _rnn_param_specs(lambda c, t: c),
        out_specs=[pl.BlockSpec((RNN_T_TILE, COL_TILE), lambda c, t: (t, c)),
                   pl.BlockSpec((1, COL_TILE), lambda c, t: (0, c))],
        scratch_shapes=[pltpu.VMEM((CONV_PAD + RNN_T_TILE, COL_TILE), F32),
                        pltpu.VMEM((RNN_T_TILE, COL_TILE), F32),
                        pltpu.VMEM((RNN_T_TILE, COL_TILE), F32),
                        pltpu.VMEM((RNN_T_TILE, COL_TILE), F32),
                        pltpu.VMEM((SUBLANES, COL_TILE), F32)],
        compiler_params=_params(("parallel", "arbitrary")),
        name="rglru_prompt",
    )(p, p, conv_w, conv_b, w_rg_a, b_rg_a, w_rg_x, b_rg_x, lru_lambda)


def _rglru_sample_kernel(x_ref, gate_ref, hist_ref, h0_ref, cw_ref, cb_ref, wa_ref, ba_ref, wx_ref, bx_ref,
                         lam_ref, y_ref, h_ref):
    xc = cb_ref[...] + x_ref[...] * cw_ref[CONV_W - 1:CONV_W, :]
    for j in range(CONV_W - 1):
        xc = xc + hist_ref[j] * cw_ref[j:j + 1, :]
    a, b = _rglru_gates(xc, wa_ref, ba_ref, wx_ref, bx_ref, lam_ref)
    h = a * h0_ref[...] + b
    h_ref[...] = h
    y_ref[...] = (h * jax.nn.gelu(gate_ref[...])).astype(y_ref.dtype)


def _rglru_sample(p, hist, h0, conv_w, conv_b, w_rg_a, b_rg_a, w_rg_x, b_rg_x, lru_lambda):
    n_c = D_RNN // COL_TILE
    row_blk = SEQ // DEC_BATCH
    return pl.pallas_call(
        _rglru_sample_kernel,
        out_shape=[jax.ShapeDtypeStruct((DEC_BATCH, D_RNN), BF16), jax.ShapeDtypeStruct((DEC_BATCH, D_RNN), F32)],
        grid=(n_c,),
        in_specs=[pl.BlockSpec((DEC_BATCH, COL_TILE), lambda c: (row_blk, COL_RNN_X + c)),
                  pl.BlockSpec((DEC_BATCH, COL_TILE), lambda c: (row_blk, COL_RNN_GATE + c)),
                  pl.BlockSpec((CONV_W - 1, DEC_BATCH, COL_TILE), lambda c: (0, 0, c)),
                  pl.BlockSpec((DEC_BATCH, COL_TILE), lambda c: (0, c))]
                 + _rnn_param_specs(lambda c: c),
        out_specs=[pl.BlockSpec((DEC_BATCH, COL_TILE), lambda c: (0, c)),
                   pl.BlockSpec((DEC_BATCH, COL_TILE), lambda c: (0, c))],
        compiler_params=_params(("parallel",)),
        name="rglru_sample",
    )(p, p, hist, h0, conv_w, conv_b, w_rg_a, b_rg_a, w_rg_x, b_rg_x, lru_lambda)


def _mix_kernel(attn_ref, y_ref, ga_ref, gr_ref, wap_ref, wrp_ref, o_ref, wap_bf, wrp_bf):
    @pl.when(pl.program_id(1) == 0)
    def _():
        wap_bf[...] = wap_ref[...].astype(BF16)
        wrp_bf[...] = wrp_ref[...].astype(BF16)

    attn_d = jnp.dot(attn_ref[...].astype(BF16), wap_bf[...], preferred_element_type=F32)
    rnn_d = jnp.dot(y_ref[...], wrp_bf[...], preferred_element_type=F32)
    mix = jax.nn.sigmoid(ga_ref[...]) * attn_d + jax.nn.sigmoid(gr_ref[...]) * rnn_d
    o_ref[...] = mix.astype(o_ref.dtype)


def _mix(attn, y, p, w_attn_proj, w_rnn_proj):
    return pl.pallas_call(
        _mix_kernel,
        out_shape=jax.ShapeDtypeStruct((T_ALL, D_MODEL), BF16),
        grid=(D_MODEL // COL_TILE, T_ALL // MIX_ROW_TILE),
        in_specs=[pl.BlockSpec((MIX_ROW_TILE, ATTN_OUT), lambda c, i: (i, 0)),
                  pl.BlockSpec((MIX_ROW_TILE, D_RNN), lambda c, i: (i, 0)),
                  pl.BlockSpec((MIX_ROW_TILE, COL_TILE), lambda c, i: (i, COL_GATE_ATTN + c)),
                  pl.BlockSpec((MIX_ROW_TILE, COL_TILE), lambda c, i: (i, COL_GATE_RNN + c)),
                  pl.BlockSpec((None, ATTN_OUT, COL_TILE), lambda c, i: (0, 0, c)),
                  pl.BlockSpec((None, D_RNN, COL_TILE), lambda c, i: (0, 0, c))],
        out_specs=pl.BlockSpec((MIX_ROW_TILE, COL_TILE), lambda c, i: (i, c)),
        scratch_shapes=[pltpu.VMEM((ATTN_OUT, COL_TILE), BF16), pltpu.VMEM((D_RNN, COL_TILE), BF16)],
        compiler_params=_params(("arbitrary", "arbitrary")),
        name="mix",
    )(attn, y, p, p, w_attn_proj, w_rnn_proj)


def _out_proj_kernel(mix_ref, w_ref, x_ref, o_ref, wbf):
    @pl.when(pl.program_id(1) == 0)
    def _():
        wbf[...] = w_ref[...].astype(BF16)

    o_ref[...] = x_ref[...] + jnp.dot(mix_ref[...], wbf[...], preferred_element_type=F32)


def _out_proj(mix, w_out, x):
    return pl.pallas_call(
        _out_proj_kernel,
        out_shape=jax.ShapeDtypeStruct((T_ALL, D_MODEL), F32),
        grid=(D_MODEL // COL_TILE, T_ALL // ROW_TILE),
        in_specs=[pl.BlockSpec((ROW_TILE, D_MODEL), lambda n, i: (i, 0)),
                  pl.BlockSpec((None, D_MODEL, COL_TILE), lambda n, i: (0, 0, n)),
                  pl.BlockSpec((ROW_TILE, COL_TILE), lambda n, i: (i, n))],
        out_specs=pl.BlockSpec((ROW_TILE, COL_TILE), lambda n, i: (i, n)),
        scratch_shapes=[pltpu.VMEM((D_MODEL, COL_TILE), BF16)],
        compiler_params=_params(("arbitrary", "arbitrary")),
        name="out_proj",
    )(mix, w_out, x)


HALF_MODEL = D_MODEL // 2
HIGH_HALF_MASK = -65536


def _pack_bf16_pairs(x):
    hi = pltpu.bitcast(x[:, :HALF_MODEL].astype(BF16).astype(F32), I32)
    lo = pltpu.bitcast(x[:, HALF_MODEL:].astype(BF16).astype(F32), I32)
    return pltpu.bitcast(hi | lax.shift_right_logical(lo, 16), F32)


def _unpack_bf16_pairs(words):
    bits = pltpu.bitcast(words, I32)
    first = pltpu.bitcast(bits & HIGH_HALF_MASK, F32).astype(BF16)
    second = pltpu.bitcast(lax.shift_left(bits, 16), F32).astype(BF16)
    return first, second


def _split_bf16(x):
    hi = x.astype(BF16)
    lo = (x - hi.astype(F32)).astype(BF16)
    return hi, lo


def _router_kernel(x_ref, g_ref, w_ref, b_ref, xn_ref, sel_ref, gate_ref, rank_ref, cnt_ref, carry, tri):
    step = pl.program_id(0)

    @pl.when(step == 0)
    def _():
        carry[...] = jnp.zeros((1, LANES), F32)
        ri = lax.broadcasted_iota(I32, (NORM_TILE, NORM_TILE), 0)
        ci = lax.broadcasted_iota(I32, (NORM_TILE, NORM_TILE), 1)
        tri[...] = jnp.where(ci < ri, 1.0, 0.0).astype(BF16)

    x = x_ref[...]
    xn = x * lax.rsqrt(jnp.mean(x * x, axis=-1, keepdims=True) + EPS) * g_ref[...]
    xn_ref[...] = _pack_bf16_pairs(xn)

    x_hi, x_lo = _split_bf16(xn)
    w_hi, w_lo = _split_bf16(w_ref[...])
    logits = (jnp.dot(x_hi, w_hi, preferred_element_type=F32)
              + jnp.dot(x_hi, w_lo, preferred_element_type=F32)
              + jnp.dot(x_lo, w_hi, preferred_element_type=F32)) + b_ref[...]

    lane = lax.broadcasted_iota(I32, (NORM_TILE, LANES), 1)
    work = logits
    vals, idxs = [], []
    for _ in range(TOP_K):
        mk = jnp.max(work, axis=-1, keepdims=True)
        ik = jnp.min(jnp.where(work == mk, lane, LANES), axis=-1, keepdims=True)
        vals.append(mk)
        idxs.append(ik)
        work = jnp.where(lane == ik, -jnp.inf, work)
    exps = [jnp.exp(v - vals[0]) for v in vals]
    den = exps[0] + exps[1] + exps[2] + exps[3]

    member = jnp.zeros((NORM_TILE, LANES), F32)
    for ik in idxs:
        member = member + jnp.where(lane == ik, 1.0, 0.0)
    before = jnp.dot(tri[...], member.astype(BF16), preferred_element_type=F32) + carry[...]
    sel = jnp.zeros((NORM_TILE, LANES), I32)
    gates = jnp.zeros((NORM_TILE, LANES), F32)
    ranks = jnp.zeros((NORM_TILE, LANES), F32)
    for k in range(TOP_K):
        rk = jnp.sum(jnp.where(lane == idxs[k], before, 0.0), axis=-1, keepdims=True)
        sel = jnp.where(lane == k, idxs[k], sel)
        gates = jnp.where(lane == k, exps[k] / den, gates)
        ranks = jnp.where(lane == k, rk, ranks)
    sel_ref[...] = sel
    gate_ref[...] = gates
    rank_ref[...] = ranks.astype(I32)
    carry[...] = carry[...] + jnp.sum(member, axis=0, keepdims=True)
    cnt_ref[...] = carry[...].astype(I32)


def _router(x2, g_ffn, w_router_pad, b_router_pad):
    tile_spec = pl.BlockSpec((NORM_TILE, LANES), lambda i: (i, 0))
    return pl.pallas_call(
        _router_kernel,
        out_shape=[jax.ShapeDtypeStruct((T_ALL, HALF_MODEL), F32),
                   jax.ShapeDtypeStruct((T_ALL, LANES), I32),
                   jax.ShapeDtypeStruct((T_ALL, LANES), F32),
                   jax.ShapeDtypeStruct((T_ALL, LANES), I32),
                   jax.ShapeDtypeStruct((1, LANES), I32)],
        grid=(T_ALL // NORM_TILE,),
        in_specs=[pl.BlockSpec((NORM_TILE, D_MODEL), lambda i: (i, 0)),
                  pl.BlockSpec((1, D_MODEL), lambda i: (0, 0)),
                  pl.BlockSpec((D_MODEL, LANES), lambda i: (0, 0)),
                  pl.BlockSpec((1, LANES), lambda i: (0, 0))],
        out_specs=[pl.BlockSpec((NORM_TILE, HALF_MODEL), lambda i: (i, 0)),
                   tile_spec, tile_spec, tile_spec,
                   pl.BlockSpec((1, LANES), lambda i: (0, 0))],
        scratch_shapes=[pltpu.VMEM((1, LANES), F32), pltpu.VMEM((NORM_TILE, NORM_TILE), BF16)],
        compiler_params=_params(("arbitrary",)),
        name="router",
    )(x2, g_ffn, w_router_pad, b_router_pad)


def _experts_kernel(item_e, item_row0, item_nblk, used_blocks, *rest):
    xs_parts, rest = rest[:XS_PARTS], rest[XS_PARTS:]
    wg_ref, wu_ref, wd_ref, bg_ref, bu_ref, bd_ref = rest[:6]
    rest = rest[6:]
    caches, news = rest[:N_FUSED_SHIFT], rest[N_FUSED_SHIFT:2 * N_FUSED_SHIFT]
    rest = rest[2 * N_FUSED_SHIFT + N_ORDER_ONLY:]
    out_hbm = rest[0]
    cache_outs = rest[1:N_FUSED_SHIFT + 1]
    (xbuf, acc, wg_bf, wu_bf, wd_bf, sem_in, sem_out, sbuf, sem_shift_in,
     sem_shift_out) = rest[N_FUSED_SHIFT + 1:]
    i = pl.program_id(0)
    j = pl.program_id(1)
    n_j = pl.num_programs(1)
    nblk = item_nblk[i]

    _shift_step(i * n_j + j, caches, news, cache_outs, sbuf, sem_shift_in, sem_shift_out)

    @pl.when(jnp.logical_and(i == 0, j == 0))
    def _():
        acc[0:EXPERT_BLOCK, :] = jnp.zeros((EXPERT_BLOCK, D_MODEL), F32)

        def zero_copy(c):
            dst = pl.ds(pl.multiple_of((used_blocks[0] + c) * EXPERT_BLOCK, EXPERT_BLOCK), EXPERT_BLOCK)
            return pltpu.make_async_copy(acc.at[0:EXPERT_BLOCK, :], out_hbm.at[dst, :], sem_out)

        n_slack = N_SORTED_BLOCKS - used_blocks[0]
        pl.loop(0, n_slack)(lambda c: zero_copy(c).start())
        pl.loop(0, n_slack)(lambda c: zero_copy(c).wait())

    n_items = pl.num_programs(0)
    x_slot = i % 2

    def x_copy(item, c, action):
        rows = pl.ds(pl.multiple_of(c * EXPERT_BLOCK, EXPERT_BLOCK), EXPERT_BLOCK)
        block = item_row0[item] // EXPERT_BLOCK + c
        for part, xs_hbm in enumerate(xs_parts):
            @pl.when(block // XS_PART_BLOCKS == part)
            def _(part=part, xs_hbm=xs_hbm):
                src = pl.ds(pl.multiple_of((block - part * XS_PART_BLOCKS) * EXPERT_BLOCK, EXPERT_BLOCK),
                            EXPERT_BLOCK)
                cp = pltpu.make_async_copy(xs_hbm.at[src, :], xbuf.at[item % 2, rows, :], sem_in.at[item % 2])
                getattr(cp, action)()

    def out_copy(item, c):
        rows = pl.ds(pl.multiple_of(c * EXPERT_BLOCK, EXPERT_BLOCK), EXPERT_BLOCK)
        dst = pl.ds(pl.multiple_of(item_row0[item] + c * EXPERT_BLOCK, EXPERT_BLOCK), EXPERT_BLOCK)
        return pltpu.make_async_copy(acc.at[rows, :], out_hbm.at[dst, :], sem_out)

    @pl.when(j == 0)
    def _():
        @pl.when(i == 0)
        def _():
            pl.loop(0, nblk)(lambda c: x_copy(i, c, "start"))

        pl.loop(0, nblk)(lambda c: x_copy(i, c, "wait"))
        nxt = jnp.minimum(i + 1, n_items - 1)

        @pl.when(i + 1 < n_items)
        def _():
            pl.loop(0, item_nblk[nxt])(lambda c: x_copy(nxt, c, "start"))

    @pl.when(nblk > 0)
    def _():
        wg_bf[...] = wg_ref[...].astype(BF16)
        wu_bf[...] = wu_ref[...].astype(BF16)
        wd_bf[...] = wd_ref[...].astype(BF16)

    prev = jnp.maximum(i - 1, 0)

    @pl.when(jnp.logical_and(j == 0, i > 0))
    def _():
        pl.loop(0, item_nblk[prev])(lambda c: out_copy(prev, c).wait())

    @pl.when(nblk > 0)
    def _():
        @pl.when(j == 0)
        def _():
            @pl.loop(0, nblk)
            def _(c):
                rows = pl.ds(pl.multiple_of(c * EXPERT_BLOCK, EXPERT_BLOCK), EXPERT_BLOCK)
                acc[rows, :] = jnp.broadcast_to(bd_ref[...], (EXPERT_BLOCK, D_MODEL))

        def process(start, size):
            rows = pl.ds(pl.multiple_of(start, EXPERT_BLOCK), size)
            xa, xb = _unpack_bf16_pairs(xbuf[x_slot, rows, :])
            gt = (jnp.dot(xa, wg_bf[0:HALF_MODEL, :], preferred_element_type=F32)
                  + jnp.dot(xb, wg_bf[HALF_MODEL:D_MODEL, :], preferred_element_type=F32)) + bg_ref[...]
            up = (jnp.dot(xa, wu_bf[0:HALF_MODEL, :], preferred_element_type=F32)
                  + jnp.dot(xb, wu_bf[HALF_MODEL:D_MODEL, :], preferred_element_type=F32)) + bu_ref[...]
            gt = jnp.minimum(gt, SWIGLU_LIMIT)
            up = jnp.clip(up, -SWIGLU_LIMIT, SWIGLU_LIMIT)
            act = gt * jax.nn.sigmoid(SWIGLU_ALPHA * gt) * (up + 1.0)
            acc[rows, :] += jnp.dot(act.astype(BF16), wd_bf[...], preferred_element_type=F32)

        quad = 4 * EXPERT_BLOCK

        @pl.loop(0, nblk // 4)
        def _(c):
            process(c * quad, 2 * EXPERT_BLOCK)
            process(c * quad + 2 * EXPERT_BLOCK, 2 * EXPERT_BLOCK)

        tail = (nblk // 4) * quad

        @pl.when(nblk % 4 >= 2)
        def _():
            process(tail, 2 * EXPERT_BLOCK)

        @pl.when(nblk % 2 == 1)
        def _():
            process((nblk - 1) * EXPERT_BLOCK, EXPERT_BLOCK)

    @pl.when(j == n_j - 1)
    def _():
        pl.loop(0, nblk)(lambda c: out_copy(i, c).start())

        @pl.when(i == n_items - 1)
        def _():
            pl.loop(0, nblk)(lambda c: out_copy(i, c).wait())


def _experts(item_e, item_row0, item_nblk, used_blocks, xs_parts, w_gate, w_up, w_down, b_gate, b_up, b_down,
             caches, news, run_after):
    assert len(run_after) == N_ORDER_ONLY
    n_j = D_FF // FF_TILE
    assert N_WORK_ITEMS * n_j > SHIFT_TOTAL_STEPS

    def jj(i, j, nblk):
        return jnp.where(nblk[i] > 0, j, n_j - 1)

    any_spec = pl.BlockSpec(memory_space=pl.ANY)
    grid_spec = pltpu.PrefetchScalarGridSpec(
        num_scalar_prefetch=4,
        grid=(N_WORK_ITEMS, n_j),
        in_specs=[any_spec] * XS_PARTS + [
            pl.BlockSpec((None, None, D_MODEL, FF_TILE), lambda i, j, e, r, n, u: (0, e[i], 0, jj(i, j, n))),
            pl.BlockSpec((None, None, D_MODEL, FF_TILE), lambda i, j, e, r, n, u: (0, e[i], 0, jj(i, j, n))),
            pl.BlockSpec((None, None, FF_TILE, D_MODEL), lambda i, j, e, r, n, u: (0, e[i], jj(i, j, n), 0)),
            pl.BlockSpec((None, 1, FF_TILE), lambda i, j, e, r, n, u: (e[i], 0, jj(i, j, n))),
            pl.BlockSpec((None, 1, FF_TILE), lambda i, j, e, r, n, u: (e[i], 0, jj(i, j, n))),
            pl.BlockSpec((None, 1, D_MODEL), lambda i, j, e, r, n, u: (e[i], 0, 0)),
        ] + [any_spec] * N_FUSED_SHIFT + [pl.BlockSpec(memory_space=pltpu.VMEM)] * N_FUSED_SHIFT
          + [any_spec] * N_ORDER_ONLY,
        out_specs=[any_spec] * (1 + N_FUSED_SHIFT),
        scratch_shapes=[pltpu.VMEM((2, EXPERT_CAP, HALF_MODEL), F32),
                        pltpu.VMEM((EXPERT_CAP, D_MODEL), F32),
                        pltpu.VMEM((D_MODEL, FF_TILE), BF16),
                        pltpu.VMEM((D_MODEL, FF_TILE), BF16),
                        pltpu.VMEM((FF_TILE, D_MODEL), BF16),
                        pltpu.SemaphoreType.DMA((2,)),
                        pltpu.SemaphoreType.DMA(()),
                        pltpu.VMEM((SHIFT_SLOTS, SHIFT_CHUNK_ROWS, HEADS_PER_GROUP, HEAD_DIM), F32),
                        pltpu.SemaphoreType.DMA((SHIFT_SLOTS,)),
                        pltpu.SemaphoreType.DMA((SHIFT_SLOTS,))],
    )
    results = pl.pallas_call(
        _experts_kernel,
        out_shape=[jax.ShapeDtypeStruct((N_SORTED_ROWS, D_MODEL), F32)]
                  + [jax.ShapeDtypeStruct(c.shape, c.dtype) for c in caches],
        grid_spec=grid_spec,
        compiler_params=_params(("arbitrary", "arbitrary")),
        name="experts",
    )(item_e, item_row0, item_nblk, used_blocks, *xs_parts, w_gate, w_up, w_down, b_gate, b_up, b_down,
      *caches, *news, *run_after)
    return results[0], results[1:]


def _row_tokens_kernel(dest_ref, out_ref):
    def clear(r, carry):
        out_ref[r] = 0
        return carry

    def place(s, carry):
        out_ref[dest_ref[s]] = s // TOP_K
        return carry

    lax.fori_loop(0, N_SORTED_ROWS, clear, 0, unroll=8)
    lax.fori_loop(0, N_SLOTS, place, 0, unroll=8)


def _row_tokens(dest_flat):
    return pl.pallas_call(
        _row_tokens_kernel,
        out_shape=jax.ShapeDtypeStruct((N_SORTED_ROWS,), I32),
        in_specs=[pl.BlockSpec(memory_space=pltpu.SMEM)],
        out_specs=pl.BlockSpec(memory_space=pltpu.SMEM),
        name="row_tokens",
    )(dest_flat)


def _combine_kernel(x_ref, gate_ref, o0, o1, o2, o3, *rest):
    yp_ref, ys_ref = rest[N_ORDER_ONLY:]
    gates = gate_ref[...]
    y = x_ref[...]
    for k, o_ref in enumerate((o0, o1, o2, o3)):
        y = y + gates[:, k:k + 1] * o_ref[...]
    is_prompt = pl.program_id(0) < SEQ // COMBINE_TILE

    @pl.when(is_prompt)
    def _():
        yp_ref[...] = y

    @pl.when(jnp.logical_not(is_prompt))
    def _():
        ys_ref[...] = y


def _combine(x2, gates, outs, run_after):
    assert DEC_BATCH == COMBINE_TILE and len(run_after) == N_ORDER_ONLY
    n_prompt = SEQ // COMBINE_TILE
    row = pl.BlockSpec((COMBINE_TILE, D_MODEL), lambda i: (i, 0))
    return pl.pallas_call(
        _combine_kernel,
        out_shape=[jax.ShapeDtypeStruct((SEQ, D_MODEL), F32), jax.ShapeDtypeStruct((DEC_BATCH, D_MODEL), F32)],
        grid=(n_prompt + 1,),
        in_specs=[row, pl.BlockSpec((COMBINE_TILE, LANES), lambda i: (i, 0)), row, row, row, row]
                 + [pl.BlockSpec(memory_space=pl.ANY)] * N_ORDER_ONLY,
        out_specs=[pl.BlockSpec((COMBINE_TILE, D_MODEL), lambda i: (jnp.minimum(i, n_prompt - 1), 0)),
                   pl.BlockSpec((COMBINE_TILE, D_MODEL), lambda i: (0, 0))],
        compiler_params=_params(("arbitrary",)),
        name="combine",
    )(x2, gates, *outs, *run_after)


def _rope_tables():
    half = ROT_DIM // 2
    inv = ROPE_THETA ** (-2.0 * jnp.arange(half, dtype=F32) / ROT_DIM)
    pos = jnp.concatenate([jnp.arange(SEQ), jnp.full((DEC_BATCH,), PAST_LEN)]).astype(F32)
    ang = pos[:, None] * inv[None, :]
    cos, sin = jnp.cos(ang), jnp.sin(ang)
    rest = HEAD_DIM - ROT_DIM
    cos_t = jnp.concatenate([cos, cos, jnp.ones((T_ALL, rest), F32)], axis=-1)
    sin_t = jnp.concatenate([-sin, sin, jnp.zeros((T_ALL, rest), F32)], axis=-1)
    return cos_t, sin_t


def _dispatch_plan(sel, ranks, counts):
    nb = (counts + EXPERT_BLOCK - 1) // EXPERT_BLOCK
    padded = nb * EXPERT_BLOCK
    pad_start = jnp.cumsum(padded) - padded
    dest = pad_start[sel] + ranks
    n_items = (nb + EXPERT_CAP_BLOCKS - 1) // EXPERT_CAP_BLOCKS
    item_end = jnp.cumsum(n_items)
    item_start = item_end - n_items
    w = jnp.arange(N_WORK_ITEMS, dtype=I32)
    total = item_end[-1]
    w_eff = jnp.minimum(w, total - 1)
    e_w = jnp.minimum(jnp.searchsorted(item_end, w_eff, side='right'), N_EXPERTS - 1).astype(I32)
    k_w = w_eff - item_start[e_w]
    row0 = pad_start[e_w] + k_w * EXPERT_CAP
    nblk = jnp.clip(nb[e_w] - k_w * EXPERT_CAP_BLOCKS, 0, EXPERT_CAP_BLOCKS)
    nblk = jnp.where(w < total, nblk, 0)
    used_blocks = jnp.sum(nb).astype(I32).reshape(1)
    return dest, e_w, row0.astype(I32), nblk.astype(I32), used_blocks


def kernel(x_prompt, x_sample, cache_k_w128, cache_v_w128, cache_k_w512, cache_v_w512, cache_k_w2048,
           cache_v_w2048, state_conv, state_rglru, g_mix, w_in, q_norm, k_norm, w_attn_proj, conv_w, conv_b,
           w_rg_a, b_rg_a, w_rg_x, b_rg_x, lru_lambda, w_rnn_proj, w_out, g_ffn, w_router, b_router,
           w_gate, b_gate, w_up, b_up, w_down, b_down):
    caches = (cache_k_w128, cache_v_w128, cache_k_w512, cache_v_w512, cache_k_w2048, cache_v_w2048)
    x = jnp.concatenate([x_prompt[0], x_sample[:, 0]], axis=0)

    xn = _rmsnorm_bf16(x, g_mix)
    cos_t, sin_t = _rope_tables()
    gains = jnp.stack([q_norm, k_norm])
    p = _in_proj(xn, w_in, cos_t, sin_t, gains)

    attn_p = _attn_prompt(p)
    qkv_s = p[SEQ:, :3 * ATTN_WIDTH].reshape(DEC_BATCH, 3, N_HEADS, HEAD_DIM)
    q_s, k_s, v_s = qkv_s[:, 0], qkv_s[:, 1], qkv_s[:, 2]
    views = [c.reshape(DEC_BATCH, ATTN_BLOCK, DILATIONS[n // 2], HEADS_PER_GROUP, HEAD_DIM)
             for n, c in enumerate(caches)]
    attn_s = _attn_sample(q_s, k_s, v_s, views).reshape(DEC_BATCH, ATTN_OUT)
    news = []
    for g in range(N_GROUPS):
        heads = slice(g * HEADS_PER_GROUP, (g + 1) * HEADS_PER_GROUP)
        news += [k_s[:, heads], v_s[:, heads]]
    attn = jnp.concatenate([attn_p, attn_s], axis=0)

    y_p, h_p = _rglru_prompt(p, conv_w, conv_b, w_rg_a, b_rg_a, w_rg_x, b_rg_x, lru_lambda)
    hist = jnp.transpose(state_conv[0], (1, 0, 2))
    y_s, h_s = _rglru_sample(p, hist, state_rglru[0], conv_w, conv_b, w_rg_a, b_rg_a, w_rg_x, b_rg_x,
                             lru_lambda)
    y = jnp.concatenate([y_p, y_s], axis=0)

    mix = _mix(attn, y, p, w_attn_proj, w_rnn_proj)
    x2 = _out_proj(mix, w_out, x)

    w_router_pad = jnp.pad(w_router[0], ((0, 0), (0, LANES - N_EXPERTS)))
    b_router_pad = jnp.pad(b_router, ((0, 0), (0, LANES - N_EXPERTS)), constant_values=NEG_BIG)
    xn2, sel, gates, ranks, counts = _router(x2, g_ffn, w_router_pad, b_router_pad)
    dest, item_e, item_row0, item_nblk, used_blocks = _dispatch_plan(
        sel[:, :TOP_K], ranks[:, :TOP_K], counts[0, :N_EXPERTS])
    src_tok = _row_tokens(dest.reshape(-1))
    part_rows = XS_PART_BLOCKS * EXPERT_BLOCK
    xs_parts = [xn2[src_tok[k * part_rows:(k + 1) * part_rows]] for k in range(XS_PARTS)]
    mid_caches = _cache_shift(caches[2:4], news[2:4], counts, "cache_shift_mid")
    out_sorted, big_caches = _experts(item_e, item_row0, item_nblk, used_blocks, xs_parts, w_gate, w_up, w_down,
                                      b_gate.reshape(N_EXPERTS, 1, D_FF), b_up.reshape(N_EXPERTS, 1, D_FF),
                                      b_down.reshape(N_EXPERTS, 1, D_MODEL), caches[4:], news[4:], mid_caches)
    small_caches = _cache_shift(caches[:2], news[:2], out_sorted, "cache_shift_small")
    new_caches = (*small_caches, *mid_caches, *big_caches)
    outs = [out_sorted[dest[:, k]] for k in range(TOP_K)]
    y_p, y_s = _combine(x2, gates, outs, small_caches)

    y_prompt = y_p.reshape(1, SEQ, D_MODEL)
    y_sample = y_s.reshape(DEC_BATCH, 1, D_MODEL)
    states_p = []
    for g, w in enumerate(WINDOWS):
        keep = min(w, SEQ)
        for off in (ATTN_WIDTH, 2 * ATTN_WIDTH):
            c0 = off + g * ATTN_OUT
            states_p.append(p[SEQ - keep:SEQ, c0:c0 + ATTN_OUT].reshape(1, 1, keep, HEADS_PER_GROUP, HEAD_DIM))
    conv_p = p[SEQ - (CONV_W - 1):SEQ, 3 * ATTN_WIDTH:3 * ATTN_WIDTH + D_RNN].reshape(1, 1, CONV_W - 1, D_RNN)
    rglru_p = h_p.reshape(1, 1, D_RNN)
    rnn_x_s = p[SEQ:, 3 * ATTN_WIDTH:3 * ATTN_WIDTH + D_RNN]
    conv_s = jnp.concatenate([state_conv[0][:, 1:], rnn_x_s[:, None, :]], axis=1)[None]
    rglru_s = h_s[None]
    return (y_prompt, y_sample, *states_p, conv_p, rglru_p, *new_caches, conv_s, rglru_s)
```
